```python
import math
import jax
import jax.numpy as jnp
from jax import lax
import numpy as np

D_MODEL = 1024
BATCH = 8
SEQ = 2048
DEPTH = 4
DEC_BATCH = 8
DEC_SEQ = 64
PAST_LEN = 4096

CHUNK = 64
QBLOCK = 128
HEAD_DIM = 64
GROUP_WIDTH = D_MODEL // 4
N_GH = GROUP_WIDTH // HEAD_DIM
RMS_EPS = 1e-6
ROPE_THETA = 10000.0
LRU_WIDTH = GROUP_WIDTH
LRU_BLOCKS = N_GH
LRU_BLOCK = LRU_WIDTH // LRU_BLOCKS
CONV_W = 4
LRU_C = 8.0
MLA_HEADS = N_GH
MLA_Q_RANK = GROUP_WIDTH
MLA_KV_RANK = GROUP_WIDTH // 2
MLA_NOPE = HEAD_DIM
MLA_ROPE = HEAD_DIM // 2
MLA_V = HEAD_DIM
RET_HEADS = N_GH
RET_DK = HEAD_DIM
RET_DV = HEAD_DIM
MLSTM_HEADS = N_GH
MLSTM_DK = HEAD_DIM
MLSTM_DV = HEAD_DIM
FFN_HIDDEN = -(-8 * D_MODEL // (3 * 256)) * 256

IN_SIZES = (LRU_WIDTH, LRU_WIDTH,
            MLA_Q_RANK, MLA_KV_RANK, MLA_ROPE,
            RET_HEADS * RET_DK, RET_HEADS * RET_DK, RET_HEADS * RET_DV, RET_HEADS * RET_DV,
            MLSTM_HEADS * MLSTM_DK, MLSTM_HEADS * MLSTM_DK, MLSTM_HEADS * MLSTM_DV, MLSTM_HEADS * MLSTM_DV,
            2 * MLSTM_HEADS)
IN_DIM = sum(IN_SIZES)

kernel_name = 'hybrid_parallel_groups_stream_step'


def rms(x, g):
    xf = x.astype(jnp.float32)
    return xf * lax.rsqrt(jnp.mean(xf * xf, axis=-1, keepdims=True) + RMS_EPS) * g


def head_rms(x, g):
    shp = x.shape
    xh = x.reshape(shp[:-1] + (shp[-1] // HEAD_DIM, HEAD_DIM))
    return rms(xh, g.reshape(-1, HEAD_DIM)).reshape(shp)


def rope(x, pos):
    half = x.shape[-1] // 2
    inv = ROPE_THETA ** (-jnp.arange(half, dtype=jnp.float32) / half)
    ang = pos[:, None] * inv[None, :]
    cos = jnp.cos(ang)[None, :, None, :]
    sin = jnp.sin(ang)[None, :, None, :]
    x1, x2 = x[..., :half], x[..., half:]
    return jnp.concatenate([x1 * cos - x2 * sin, x1 * sin + x2 * cos], axis=-1)


def to_chunks(x, L):
    B, T = x.shape[:2]
    return x.reshape((B, T // L, L) + x.shape[2:]).swapaxes(0, 1)


def from_chunks(y):
    n, B, L = y.shape[:3]
    return y.swapaxes(0, 1).reshape((B, n * L) + y.shape[3:])


def causal_conv(x, buf, w, b):
    T = x.shape[1]
    xp = jnp.concatenate([buf.astype(x.dtype), x], axis=1)
    y = b + w[0] * xp[:, 0:T]
    for j in range(1, CONV_W):
        y = y + w[j] * xp[:, j:j + T]
    return y, xp[:, -(CONV_W - 1):]


def rglru(xc, h0, wa, ba, wx, bx, lam):
    B, T, _ = xc.shape
    xf = xc.astype(jnp.float32)
    xb = xf.reshape(B, T, LRU_BLOCKS, LRU_BLOCK)
    r = jax.nn.sigmoid(jnp.einsum('btni,nij->btnj', xb, wa).reshape(B, T, LRU_WIDTH) + ba)
    i = jax.nn.sigmoid(jnp.einsum('btni,nij->btnj', xb, wx).reshape(B, T, LRU_WIDTH) + bx)
    log_a = -LRU_C * r * jax.nn.softplus(-lam)
    a = jnp.exp(log_a)
    u = jnp.sqrt(-jnp.expm1(2.0 * log_a)) * i * xf
    u = u.at[:, 0].add(a[:, 0] * h0.astype(jnp.float32))

    def comb(left, right):
        return (left[0] * right[0], right[0] * left[1] + right[1])

    _, h = lax.associative_scan(comb, (a, u), axis=1)
    return h, h[:, -1]


def mla_attend(q_nope, q_rope, k_nope, k_rope, v, P):
    T = q_nope.shape[1]
    q_chunk = np.arange(P, P + T) // CHUNK
    k_chunk = np.arange(P + T) // CHUNK
    scale = (MLA_NOPE + MLA_ROPE) ** -0.5
    outs = []
    for s in range(0, T, QBLOCK):
        e = min(s + QBLOCK, T)
        kend = P + min(T, -(-e // CHUNK) * CHUNK)
        sc = (jnp.einsum('bqhd,bkhd->bhqk', q_nope[:, s:e], k_nope[:, :kend])
              + jnp.einsum('bqhd,bkd->bhqk', q_rope[:, s:e], k_rope[:, :kend])) * scale
        mask = jnp.asarray(k_chunk[None, :kend] <= q_chunk[s:e, None])
        sc = jnp.where(mask, sc.astype(jnp.float32), -jnp.inf)
        p = jax.nn.softmax(sc, axis=-1)
        outs.append(jnp.einsum('bhqk,bkhd->bqhd', p, v[:, :kend]))
    return jnp.concatenate(outs, axis=1)


def retention(q, k, v, S0):
    T = q.shape[1]
    L = min(CHUNK, T)
    lg = jnp.log(1.0 - 2.0 ** (-5.0 - jnp.arange(RET_HEADS, dtype=jnp.float32)))
    idx = jnp.arange(L, dtype=jnp.float32)
    diff = idx[:, None] - idx[None, :]
    causal = diff >= 0
    dmat = jnp.where(causal[None], jnp.exp(jnp.where(causal, diff, 0.0)[None] * lg[:, None, None]), 0.0)
    cross = jnp.exp((idx[:, None] + 1.0) * lg[None, :])
    tail = jnp.exp((L - 1.0 - idx)[None, :] * lg[:, None])
    whole = jnp.exp(L * lg)

    def step(S, inp):
        qc, kc, vc = inp
        att = jnp.einsum('bihd,bjhd->bhij', qc, kc) * dmat
        y = (jnp.einsum('bhij,bjhe->bihe', att, vc)
             + jnp.einsum('bihd,bhde->bihe', qc, S) * cross[None, :, :, None])
        S = whole[None, :, None, None] * S + jnp.einsum('bjhd,bjhe,hj->bhde', kc, vc, tail)
        return S, y

    xs = (to_chunks(q.astype(jnp.float32), L), to_chunks(k.astype(jnp.float32), L),
          to_chunks(v.astype(jnp.float32), L))
    S, ys = lax.scan(step, S0.astype(jnp.float32), xs)
    return from_chunks(ys), S


def mlstm(q, k, v, ig, lf, C0, n0, m0):
    T = q.shape[1]
    L = min(CHUNK, T)
    tril = jnp.tril(jnp.ones((L, L), dtype=bool))

    def step(carry, inp):
        C, n, m = carry
        qc, kc, vc, igc, lfc = inp
        bh = jnp.cumsum(lfc, axis=1).transpose(0, 2, 1)
        igh = igc.transpose(0, 2, 1)
        dm = jnp.where(tril, bh[..., :, None] - bh[..., None, :] + igh[..., None, :], -jnp.inf)
        m_state = bh + m[..., None]
        m_t = jnp.maximum(m_state, jnp.max(dm, axis=-1))
        sc = jnp.einsum('bthd,bshd->bhts', qc, kc) * jnp.exp(dm - m_t[..., None])
        g = jnp.exp(m_state - m_t)
        num = jnp.einsum('bhts,bshe->bhte', sc, vc) + g[..., None] * jnp.einsum('bthd,bhed->bhte', qc, C)
        den = jnp.sum(sc, axis=-1) + g * jnp.einsum('bthd,bhd->bht', qc, n)
        hh = num / jnp.maximum(jnp.abs(den), jnp.exp(-m_t))[..., None]
        m_new = m_t[..., -1]
        ws = jnp.exp(bh[..., -1:] - bh + igh - m_new[..., None])
        gl = jnp.exp(bh[..., -1] + m - m_new)
        C = gl[..., None, None] * C + jnp.einsum('bhs,bshe,bshd->bhed', ws, vc, kc)
        n = gl[..., None] * n + jnp.einsum('bhs,bshd->bhd', ws, kc)
        return (C, n, m_new), hh.transpose(0, 2, 1, 3)

    carry0 = (C0.astype(jnp.float32), n0.astype(jnp.float32), m0.astype(jnp.float32))
    xs = tuple(to_chunks(a.astype(jnp.float32), L) for a in (q, k, v, ig, lf))
    (C, n, m), hs = lax.scan(step, carry0, xs)
    return from_chunks(hs), (C, n, m)


def layer(x, past_ckv, past_krope, lru_h, lru_conv, ret_S, m_C, m_n, m_m, lp):
    B, T, _ = x.shape
    P = past_ckv.shape[1]
    pos = jnp.arange(P, P + T, dtype=jnp.float32)
    h = rms(x, lp['norm_mix'])
    proj = h @ lp['w_in']
    (lru_x, lru_g, q_lat, kv_lat, k_rp, r_q, r_k, r_v, r_g,
     m_q, m_k, m_v, m_o, m_if) = jnp.split(proj, np.cumsum(IN_SIZES)[:-1].tolist(), axis=-1)

    xc, conv_new = causal_conv(lru_x, lru_conv, lp['lru_conv_w'], lp['lru_conv_b'])
    y_a, h_new = rglru(xc, lru_h, lp['lru_wa'], lp['lru_ba'], lp['lru_wx'], lp['lru_bx'], lp['lru_lambda'])

    q = (rms(q_lat, lp['mla_q_norm']) @ lp['mla_wq_b']).reshape(B, T, MLA_HEADS, MLA_NOPE + MLA_ROPE)
    q_nope = rms(q[..., :MLA_NOPE], lp['mla_qn_norm'])
    q_rope = rope(rms(q[..., MLA_NOPE:], lp['mla_qr_norm']), pos)
    ckv_new = rms(kv_lat, lp['mla_kv_norm'])
    krope_new = rope(rms(k_rp, lp['mla_kr_norm'])[:, :, None, :], pos)[:, :, 0, :]
    ckv_all = jnp.concatenate([past_ckv.astype(jnp.float32), ckv_new], axis=1)
    kr_all = jnp.concatenate([past_krope.astype(jnp.float32), krope_new], axis=1)
    kv = (ckv_all @ lp['mla_wkv_b']).reshape(B, P + T, MLA_HEADS, MLA_NOPE + MLA_V)
    k_nope = rms(kv[..., :MLA_NOPE], lp['mla_kn_norm'])
    y_b = mla_attend(q_nope, q_rope, k_nope, kr_all, kv[..., MLA_NOPE:], P).reshape(B, T, MLA_HEADS * MLA_V)

    rq = rope(r_q.reshape(B, T, RET_HEADS, RET_DK), pos)
    rk = rope(r_k.reshape(B, T, RET_HEADS, RET_DK), pos) * RET_DK ** -0.5
    y_c, S_new = retention(rq, rk, r_v.reshape(B, T, RET_HEADS, RET_DV), ret_S)

    b_if = lp['mlstm_b_if']
    ig = m_if[..., :MLSTM_HEADS] + b_if[:MLSTM_HEADS]
    lf = jax.nn.log_sigmoid(m_if[..., MLSTM_HEADS:] + b_if[MLSTM_HEADS:])
    y_d, (C_new, n_new, mm_new) = mlstm(
        m_q.reshape(B, T, MLSTM_HEADS, MLSTM_DK),
        m_k.reshape(B, T, MLSTM_HEADS, MLSTM_DK) * MLSTM_DK ** -0.5,
        m_v.reshape(B, T, MLSTM_HEADS, MLSTM_DV), ig, lf, m_C, m_n, m_m)

    on = lp['out_norm']
    G = GROUP_WIDTH
    y_a = head_rms(y_a, on[:G]) * jax.nn.gelu(lru_g)
    y_b = head_rms(y_b, on[G:2 * G])
    y_c = head_rms(y_c.reshape(B, T, -1), on[2 * G:3 * G]) * jax.nn.silu(r_g)
    y_d = head_rms(y_d.reshape(B, T, -1), on[3 * G:]) * jax.nn.sigmoid(m_o)
    x = x + jnp.concatenate([y_a, y_b, y_c, y_d], axis=-1) @ lp['w_out']

    gu = rms(x, lp['norm_ffn']) @ lp['w_gu']
    x = x + (jax.nn.silu(gu[..., :FFN_HIDDEN]) * gu[..., FFN_HIDDEN:]) @ lp['w_down']
    return x, (ckv_new, krope_new, h_new, conv_new, S_new, C_new, n_new, mm_new)


def setup_inputs(seed: int = 0) -> dict:
    key = jax.random.key(seed)
    keys = jax.random.split(key, 48)
    cnt = [0]
    f32 = jnp.float32

    def nxt():
        cnt[0] += 1
        return keys[cnt[0] - 1]

    def nrm(shape, scale):
        return jax.random.normal(nxt(), shape, f32) * scale

    def gain(shape):
        return 1.0 + 0.05 * jax.random.normal(nxt(), shape, f32)

    def unif(shape, lo, hi):
        return jax.random.uniform(nxt(), shape, f32, lo, hi)

    res = 1.0 / math.sqrt(2 * DEPTH)
    inp = {}
    inp['x_prompt'] = nrm((BATCH, SEQ, D_MODEL), 1.0)
    inp['x_sample'] = nrm((DEC_BATCH, DEC_SEQ, D_MODEL), 1.0)
    inp['cache_mla_ckv'] = nrm((DEPTH, DEC_BATCH, PAST_LEN, MLA_KV_RANK), 1.0)
    inp['cache_mla_krope'] = nrm((DEPTH, DEC_BATCH, PAST_LEN, MLA_ROPE), 1.0)
    inp['state_lru_h'] = nrm((DEPTH, DEC_BATCH, LRU_WIDTH), 0.5)
    inp['state_lru_conv'] = nrm((DEPTH, DEC_BATCH, CONV_W - 1, LRU_WIDTH), 1.0)
    inp['state_ret'] = nrm((DEPTH, DEC_BATCH, RET_HEADS, RET_DK, RET_DV), 0.3)
    inp['state_mlstm_C'] = nrm((DEPTH, DEC_BATCH, MLSTM_HEADS, MLSTM_DV, MLSTM_DK), 0.3)
    inp['state_mlstm_n'] = nrm((DEPTH, DEC_BATCH, MLSTM_HEADS, MLSTM_DK), 0.3)
    inp['state_mlstm_m'] = unif((DEPTH, DEC_BATCH, MLSTM_HEADS), 0.0, 2.0)
    inp['norm_mix'] = gain((DEPTH, D_MODEL))
    inp['w_in'] = nrm((DEPTH, D_MODEL, IN_DIM), D_MODEL ** -0.5)
    inp['lru_conv_w'] = nrm((DEPTH, CONV_W, LRU_WIDTH), CONV_W ** -0.5)
    inp['lru_conv_b'] = nrm((DEPTH, LRU_WIDTH), 0.01)
    inp['lru_wa'] = nrm((DEPTH, LRU_BLOCKS, LRU_BLOCK, LRU_BLOCK), LRU_BLOCK ** -0.5)
    inp['lru_ba'] = nrm((DEPTH, LRU_WIDTH), 0.01)
    inp['lru_wx'] = nrm((DEPTH, LRU_BLOCKS, LRU_BLOCK, LRU_BLOCK), LRU_BLOCK ** -0.5)
    inp['lru_bx'] = nrm((DEPTH, LRU_WIDTH), 0.01)
    u = unif((DEPTH, LRU_WIDTH), 0.9, 0.999) ** (1.0 / LRU_C)
    inp['lru_lambda'] = jnp.log(u) - jnp.log1p(-u)
    inp['mla_q_norm'] = gain((DEPTH, MLA_Q_RANK))
    inp['mla_wq_b'] = nrm((DEPTH, MLA_Q_RANK, MLA_HEADS * (MLA_NOPE + MLA_ROPE)), MLA_Q_RANK ** -0.5)
    inp['mla_qn_norm'] = gain((DEPTH, MLA_NOPE))
    inp['mla_qr_norm'] = gain((DEPTH, MLA_ROPE))
    inp['mla_kv_norm'] = gain((DEPTH, MLA_KV_RANK))
    inp['mla_kr_norm'] = gain((DEPTH, MLA_ROPE))
    inp['mla_wkv_b'] = nrm((DEPTH, MLA_KV_RANK, MLA_HEADS * (MLA_NOPE + MLA_V)), MLA_KV_RANK ** -0.5)
    inp['mla_kn_norm'] = gain((DEPTH, MLA_NOPE))
    inp['mlstm_b_if'] = jnp.concatenate([nrm((DEPTH, MLSTM_HEADS), 0.1),
                                         unif((DEPTH, MLSTM_HEADS), 3.0, 6.0)], axis=-1)
    inp['out_norm'] = gain((DEPTH, D_MODEL))
    inp['w_out'] = nrm((DEPTH, D_MODEL, D_MODEL), D_MODEL ** -0.5 * res)
    inp['norm_ffn'] = gain((DEPTH, D_MODEL))
    inp['w_gu'] = nrm((DEPTH, D_MODEL, 2 * FFN_HIDDEN), D_MODEL ** -0.5)
    inp['w_down'] = nrm((DEPTH, FFN_HIDDEN, D_MODEL), FFN_HIDDEN ** -0.5 * res)
    return inp


def reference(x_prompt, x_sample, cache_mla_ckv, cache_mla_krope, state_lru_h, state_lru_conv,
              state_ret, state_mlstm_C, state_mlstm_n, state_mlstm_m,
              norm_mix, w_in, lru_conv_w, lru_conv_b, lru_wa, lru_ba, lru_wx, lru_bx, lru_lambda,
              mla_q_norm, mla_wq_b, mla_qn_norm, mla_qr_norm, mla_kv_norm, mla_kr_norm, mla_wkv_b,
              mla_kn_norm, mlstm_b_if, out_norm, w_out, norm_ffn, w_gu, w_down):
    f32 = jnp.float32
    Bp = x_prompt.shape[0]
    yp = x_prompt
    ys = x_sample
    p_acc = [[] for _ in range(8)]
    s_acc = [[] for _ in range(8)]
    for l in range(DEPTH):
        lp = {'norm_mix': norm_mix[l], 'w_in': w_in[l],
              'lru_conv_w': lru_conv_w[l], 'lru_conv_b': lru_conv_b[l],
              'lru_wa': lru_wa[l], 'lru_ba': lru_ba[l], 'lru_wx': lru_wx[l], 'lru_bx': lru_bx[l],
              'lru_lambda': lru_lambda[l],
              'mla_q_norm': mla_q_norm[l], 'mla_wq_b': mla_wq_b[l], 'mla_qn_norm': mla_qn_norm[l],
              'mla_qr_norm': mla_qr_norm[l], 'mla_kv_norm': mla_kv_norm[l], 'mla_kr_norm': mla_kr_norm[l],
              'mla_wkv_b': mla_wkv_b[l], 'mla_kn_norm': mla_kn_norm[l],
              'mlstm_b_if': mlstm_b_if[l], 'out_norm': out_norm[l], 'w_out': w_out[l],
              'norm_ffn': norm_ffn[l], 'w_gu': w_gu[l], 'w_down': w_down[l]}
        yp, st_p = layer(yp,
                         jnp.zeros((Bp, 0, MLA_KV_RANK), f32), jnp.zeros((Bp, 0, MLA_ROPE), f32),
                         jnp.zeros((Bp, LRU_WIDTH), f32), jnp.zeros((Bp, CONV_W - 1, LRU_WIDTH), f32),
                         jnp.zeros((Bp, RET_HEADS, RET_DK, RET_DV), f32),
                         jnp.zeros((Bp, MLSTM_HEADS, MLSTM_DV, MLSTM_DK), f32),
                         jnp.zeros((Bp, MLSTM_HEADS, MLSTM_DK), f32),
                         jnp.zeros((Bp, MLSTM_HEADS), f32), lp)
        ys, st_s = layer(ys, cache_mla_ckv[l], cache_mla_krope[l], state_lru_h[l], state_lru_conv[l],
                         state_ret[l], state_mlstm_C[l], state_mlstm_n[l], state_mlstm_m[l], lp)
        for acc, s in zip(p_acc, st_p):
            acc.append(s)
        for acc, s in zip(s_acc, st_s):
            acc.append(s)
    p_ckv, p_krope, p_lru_h, p_lru_conv, p_ret, p_mC, p_mn, p_mm = [jnp.stack(a) for a in p_acc]
    s_ckv, s_krope, s_lru_h, s_lru_conv, s_ret, s_mC, s_mn, s_mm = [jnp.stack(a) for a in s_acc]
    return (yp, ys, p_ckv, p_krope, p_lru_h, p_lru_conv, p_ret, p_mC, p_mn, p_mm,
            s_ckv, s_krope, s_lru_h, s_lru_conv, s_ret, s_mC, s_mn, s_mm)
```

```python
import functools
import math

import jax
import jax.numpy as jnp
import numpy as np
from jax import lax
from jax.experimental import pallas as pl
from jax.experimental.pallas import tpu as pltpu

F32 = jnp.float32
BF16 = jnp.bfloat16

D_MODEL = 1024
CHUNK = 64
HEAD_DIM = 64
GROUP = 256
N_HEADS = 4
RMS_EPS = 1e-6
ROPE_THETA = 10000.0
CONV_W = 4
LRU_C = 8.0
KV_RANK = 128
ROPE_DIM = 32
FFN_HIDDEN = 2816
IN_PAD = 3072

COL_LRU_X, COL_LRU_G, COL_Q_LAT = 0, 256, 512
COL_KV_LAT, COL_MISC = 768, 896
COL_R_Q, COL_R_K, COL_R_V, COL_R_G = 1024, 1280, 1536, 1792
COL_M_Q, COL_M_K, COL_M_V, COL_M_O = 2048, 2304, 2560, 2816
MISC_IF = 32
ATT_SCALE = (HEAD_DIM + ROPE_DIM) ** -0.5
RET_LOG_DECAY = tuple(math.log(1.0 - 2.0 ** (-5.0 - h)) for h in range(N_HEADS))

VMEM_LIMIT = 56 * 1024 * 1024


def _dot(a, b):
    return jnp.dot(a.astype(BF16), b.astype(BF16), preferred_element_type=F32)


def _dot_nt(a, b):
    return lax.dot_general(a.astype(BF16), b.astype(BF16), (((1,), (1,)), ((), ())),
                           preferred_element_type=F32)


def _dot_tn(a, b):
    return lax.dot_general(a.astype(BF16), b.astype(BF16), (((0,), (0,)), ((), ())),
                           preferred_element_type=F32)


def _split3(x):
    hi = x.astype(BF16)
    r1 = x - hi.astype(F32)
    mid = r1.astype(BF16)
    lo = (r1 - mid.astype(F32)).astype(BF16)
    return hi, mid, lo


def _dot_f32_lhs(x, m):
    hi, mid, lo = _split3(x)
    return (jnp.dot(hi, m, preferred_element_type=F32) + jnp.dot(mid, m, preferred_element_type=F32)
            + jnp.dot(lo, m, preferred_element_type=F32))


def _rms_rows(x, g):
    return x * lax.rsqrt(jnp.mean(x * x, axis=-1, keepdims=True) + RMS_EPS) * g


def _lane_head(width, head_width):
    return lax.broadcasted_iota(jnp.int32, (1, width), 1) // head_width


def _head_mean_matrix(width, head_width):
    r = lax.broadcasted_iota(jnp.int32, (width, width), 0) // head_width
    c = lax.broadcasted_iota(jnp.int32, (width, width), 1) // head_width
    return jnp.where(r == c, 1.0 / head_width, 0.0).astype(BF16)


def _head_rms(y, g):
    ms = _dot_f32_lhs(y * y, _head_mean_matrix(GROUP, HEAD_DIM))
    return y * lax.rsqrt(ms + RMS_EPS) * g


def _shift_rows(x, d, fill):
    rows = lax.broadcasted_iota(jnp.int32, x.shape, 0)
    return jnp.where(rows >= d, pltpu.roll(x, d, 0), fill)


def _cumsum_rows(x):
    d = 1
    while d < x.shape[0]:
        x = x + _shift_rows(x, d, 0.0)
        d *= 2
    return x


def _softplus(z):
    return jnp.maximum(z, 0.0) + jnp.log1p(jnp.exp(-jnp.abs(z)))


def _sigmoid(z):
    return 1.0 / (1.0 + jnp.exp(-z))


def _gelu_tanh(z):
    return 0.5 * z * (1.0 + jnp.tanh(math.sqrt(2.0 / math.pi) * (z + 0.044715 * (z * z * z))))


def _rope_lanes(x, c, sp, sm, half):
    w = x.shape[1]
    return x * c + pltpu.roll(x, half, 1) * sp + pltpu.roll(x, w - half, 1) * sm


def _tile_lanes(t, n):
    return jnp.concatenate([t] * n, axis=1) if n > 1 else t


def _in_proj_kernel(x_ref, g_ref, w_ref, o_ref):
    h = _rms_rows(x_ref[...], g_ref[...])
    o_ref[...] = jnp.dot(h.astype(BF16), w_ref[...], preferred_element_type=F32)


def _in_proj(x, norm_mix, w_in_pad, layer, tm):
    rows = x.shape[0]
    return pl.pallas_call(
        _in_proj_kernel,
        grid=(rows // tm,),
        in_specs=[pl.BlockSpec((tm, D_MODEL), lambda i: (i, 0)),
                  pl.BlockSpec((None, 1, D_MODEL), lambda i: (layer, 0, 0)),
                  pl.BlockSpec((None, D_MODEL, IN_PAD), lambda i: (layer, 0, 0))],
        out_specs=pl.BlockSpec((tm, IN_PAD), lambda i: (i, 0)),
        out_shape=jax.ShapeDtypeStruct((rows, IN_PAD), F32),
        compiler_params=pltpu.CompilerParams(dimension_semantics=("arbitrary",),
                                             vmem_limit_bytes=VMEM_LIMIT),
        name="in_proj",
    )(x, norm_mix, w_in_pad)


def _lru_kernel(x_ref, g_ref, conv0_ref, h0_ref, cw_ref, cb_ref, wg_ref, bg_ref, lam_ref, on_ref,
                *rest, aliased):
    y_ref, hout_ref, convout_ref, xp_scr, h_scr = rest[1:] if aliased else rest
    j = pl.program_id(1)
    tb = x_ref.shape[0]

    @pl.when(j == 0)
    def _():
        xp_scr[0:8, :] = conv0_ref[...]
        h_scr[...] = h0_ref[...]

    x = x_ref[...]
    xp_scr[8:8 + tb, :] = x
    cw = cw_ref[...]
    xc = (cb_ref[...] + cw[3:4] * x + cw[2:3] * xp_scr[7:7 + tb, :]
          + cw[1:2] * xp_scr[6:6 + tb, :] + cw[0:1] * xp_scr[5:5 + tb, :])
    tail = xp_scr[tb:tb + 8, :]
    xp_scr[0:8, :] = tail
    convout_ref[...] = tail

    gates = _dot(xc, wg_ref[...]) + bg_ref[...]
    r = _sigmoid(gates[:, :GROUP])
    i = _sigmoid(gates[:, GROUP:])
    log_a = -LRU_C * r * _softplus(-lam_ref[...])
    a = jnp.exp(log_a)
    u = jnp.sqrt(-jnp.tanh(log_a) * (a * a + 1.0)) * i * xc

    d = 1
    while d < tb:
        u = a * _shift_rows(u, d, 0.0) + u
        a = a * _shift_rows(a, d, 1.0)
        d *= 2
    h = u + a * h_scr[...]
    h_last = h[tb - 1:tb, :]
    h_scr[...] = h_last
    hout_ref[...] = h_last
    y_ref[...] = (_head_rms(h, on_ref[...]) * _gelu_tanh(g_ref[...])).astype(BF16)


def _mla_seg_matrix():
    r = lax.broadcasted_iota(jnp.int32, (512, 512), 0)
    c = lax.broadcasted_iota(jnp.int32, (512, 512), 1)
    same = (r // 128) == (c // 128)
    rl, cl = r % 128, c % 128
    nope = same & (rl < HEAD_DIM) & (cl < HEAD_DIM)
    rope = same & (rl >= HEAD_DIM) & (rl < HEAD_DIM + ROPE_DIM) & (cl >= HEAD_DIM) & (cl < HEAD_DIM + ROPE_DIM)
    return jnp.where(nope, 1.0 / HEAD_DIM, jnp.where(rope, 1.0 / ROPE_DIM, 0.0)).astype(BF16)


def _mla_pre_kernel(qlat_ref, kvlat_ref, misc_ref, c_ref, sp_ref, sm_ref,
                    qn_ref, wq_ref, gq_ref, kvn_ref, krn_ref, wkv_ref, gk_ref,
                    q_out, k_out, v_out, ckv_out, kr_out):
    seg = _mla_seg_matrix()
    c, sp, sm = c_ref[...], sp_ref[...], sm_ref[...]
    half = ROPE_DIM // 2

    qraw = _dot(_rms_rows(qlat_ref[...], qn_ref[...]), wq_ref[...])
    qh = qraw * lax.rsqrt(_dot_f32_lhs(qraw * qraw, seg) + RMS_EPS) * gq_ref[...]
    q = _rope_lanes(qh, _tile_lanes(c, 4), _tile_lanes(sp, 4), _tile_lanes(sm, 4), half)
    q_out[...] = q.astype(BF16)

    ckv = _rms_rows(kvlat_ref[...], kvn_ref[...])
    ckv_out[...] = ckv

    misc = misc_ref[...]
    lane = lax.broadcasted_iota(jnp.int32, misc.shape, 1)
    kr = jnp.where(lane < ROPE_DIM, misc, 0.0)
    kr = kr * lax.rsqrt(jnp.sum(kr * kr, axis=-1, keepdims=True) * (1.0 / ROPE_DIM) + RMS_EPS) * krn_ref[...]
    kr = _rope_lanes(pltpu.roll(kr, HEAD_DIM, 1), c, sp, sm, half)
    kr_out[...] = pltpu.roll(kr, 128 - HEAD_DIM, 1)[:, :ROPE_DIM]

    kv = _dot(ckv, wkv_ref[...])
    kraw = kv[:, :512]
    kn = kraw * lax.rsqrt(_dot_f32_lhs(kraw * kraw, seg) + RMS_EPS) * gk_ref[...]
    k_out[...] = (kn + _tile_lanes(kr, 4)).astype(BF16)
    v_out[...] = kv[:, 512:].astype(BF16)


def _kv_past_kernel(ckv_ref, kr_ref, wkv_ref, gk_ref, k_out, v_out):
    seg = _mla_seg_matrix()
    kv = _dot(ckv_ref[...], wkv_ref[...])
    kraw = kv[:, :512]
    kn = kraw * lax.rsqrt(_dot_f32_lhs(kraw * kraw, seg) + RMS_EPS) * gk_ref[...]
    r = lax.broadcasted_iota(jnp.int32, (ROPE_DIM, 128), 0)
    cidx = lax.broadcasted_iota(jnp.int32, (ROPE_DIM, 128), 1)
    place = jnp.where(cidx == r + HEAD_DIM, 1.0, 0.0).astype(BF16)
    kr = jnp.dot(kr_ref[...].astype(BF16), place, preferred_element_type=F32)
    k_out[...] = (kn + _tile_lanes(kr, 4)).astype(BF16)
    v_out[...] = kv[:, 512:].astype(BF16)


def _attn_kernel(q_ref, ka_ref, va_ref, kb_ref, vb_ref, on_ref, *rest, aliased, tka, n_past_static):
    y_ref = rest[1] if aliased else rest[0]
    j = pl.program_id(1)
    tq = q_ref.shape[0]
    n_past = j * (tq // tka) if n_past_static is None else n_past_static
    row_c = lax.broadcasted_iota(jnp.int32, (tq, tq), 0) // CHUNK
    col_c = lax.broadcasted_iota(jnp.int32, (tq, tq), 1) // CHUNK
    visible = col_c <= row_c

    outs = []
    for h in range(N_HEADS):
        hs = slice(128 * h, 128 * h + 128)
        qh = q_ref[:, hs]
        s = jnp.where(visible, _dot_nt(qh, kb_ref[:, hs]) * ATT_SCALE, -jnp.inf)
        m = jnp.max(s, axis=-1, keepdims=True)
        p = jnp.exp(s - m)
        l = jnp.sum(p, axis=-1, keepdims=True)
        acc = _dot(p, vb_ref[:, hs])

        def body(t, carry, hs=hs, qh=qh):
            m, l, acc = carry
            off = pl.multiple_of(t * tka, tka)
            s = _dot_nt(qh, ka_ref[pl.ds(off, tka), hs]) * ATT_SCALE
            m_new = jnp.maximum(m, jnp.max(s, axis=-1, keepdims=True))
            p = jnp.exp(s - m_new)
            alpha = jnp.exp(m - m_new)
            l = alpha * l + jnp.sum(p, axis=-1, keepdims=True)
            acc = alpha * acc + _dot(p, va_ref[pl.ds(off, tka), hs])
            return m_new, l, acc

        m, l, acc = lax.fori_loop(0, n_past, body, (m, l, acc))
        outs.append(acc / l)

    o = jnp.concatenate([outs[0] + pltpu.roll(outs[1], HEAD_DIM, 1),
                         outs[2] + pltpu.roll(outs[3], HEAD_DIM, 1)], axis=1)
    y_ref[...] = _head_rms(o, on_ref[...]).astype(BF16)


def _ret_kernel(q_ref, k_ref, v_ref, g_ref, c_ref, sp_ref, sm_ref, s0_ref, on_ref, *rest, aliased):
    y_ref, sout_ref, s_scr = rest[1:] if aliased else rest
    j = pl.program_id(1)
    lc = q_ref.shape[0]

    @pl.when(j == 0)
    def _():
        s_scr[...] = s0_ref[...]

    c, sp, sm = (_tile_lanes(t[...], 2) for t in (c_ref, sp_ref, sm_ref))
    half = HEAD_DIM // 2
    q = _rope_lanes(q_ref[...], c, sp, sm, half)
    k = _rope_lanes(k_ref[...], c, sp, sm, half) * (HEAD_DIM ** -0.5)
    v = v_ref[...].astype(BF16)

    lane_head = _lane_head(GROUP, HEAD_DIM)
    lg_lane = jnp.zeros((1, GROUP), F32)
    for h in range(N_HEADS):
        lg_lane = jnp.where(lane_head == h, RET_LOG_DECAY[h], lg_lane)
    t_col = lax.broadcasted_iota(jnp.int32, (lc, 1), 0).astype(F32)
    cross = jnp.exp((t_col + 1.0) * lg_lane)
    tail = jnp.exp((lc - 1.0 - t_col) * lg_lane)
    whole = jnp.exp(float(lc) * lg_lane)
    ti = lax.broadcasted_iota(jnp.int32, (lc, lc), 0)
    si = lax.broadcasted_iota(jnp.int32, (lc, lc), 1)
    causal = ti >= si
    diff = jnp.where(causal, ti - si, 0).astype(F32)

    s_old = s_scr[...]
    y = _dot(q, s_old) * cross
    kb = k.astype(BF16)
    for h in range(N_HEADS):
        head = lane_head == h
        att = _dot_nt(jnp.where(head, q, 0.0), kb)
        dmat = jnp.where(causal, jnp.exp(diff * RET_LOG_DECAY[h]), 0.0)
        y = y + jnp.where(head, _dot(att * dmat, v), 0.0)

    r_head = lax.broadcasted_iota(jnp.int32, (GROUP, GROUP), 0) // HEAD_DIM
    c_head = lax.broadcasted_iota(jnp.int32, (GROUP, GROUP), 1) // HEAD_DIM
    s_new = s_old * whole + jnp.where(r_head == c_head, _dot_tn(k * tail, v), 0.0)
    s_scr[...] = s_new
    sout_ref[...] = s_new
    y_ref[...] = (_head_rms(y, on_ref[...]) * (g_ref[...] * _sigmoid(g_ref[...]))).astype(BF16)


def _mlstm_kernel(q_ref, k_ref, v_ref, o_ref, misc_ref, bif_ref, c0_ref, n0_ref, m0_ref, on_ref,
                  *rest, aliased):
    y_ref, cout_ref, nout_ref, mout_ref, c_scr, n_scr, m_scr = rest[1:] if aliased else rest
    j = pl.program_id(1)
    lc = q_ref.shape[0]

    @pl.when(j == 0)
    def _():
        c_scr[...] = c0_ref[...]
        n_scr[...] = n0_ref[...]
        m_scr[...] = m0_ref[...]

    q = q_ref[...]
    k = k_ref[...] * (HEAD_DIM ** -0.5)
    kb = k.astype(BF16)
    v = v_ref[...]
    vb = v.astype(BF16)
    gates = misc_ref[...] + bif_ref[...]
    bh_all = _cumsum_rows(-_softplus(-gates))
    c_old, n_old, m_old = c_scr[...], n_scr[...], m_scr[...]

    lane_head = _lane_head(GROUP, HEAD_DIM)
    lane128 = lax.broadcasted_iota(jnp.int32, (lc, 128), 1)
    ti = lax.broadcasted_iota(jnp.int32, (lc, lc), 0)
    si = lax.broadcasted_iota(jnp.int32, (lc, lc), 1)
    tril = ti >= si
    q_c = _dot_nt(q, c_old)

    hh = jnp.zeros((lc, GROUP), F32)
    ws_full = jnp.zeros((lc, GROUP), F32)
    gl_row = jnp.zeros((1, GROUP), F32)
    m_new_row = m_old
    for h in range(N_HEADS):
        head = lane_head == h
        ig = gates[:, MISC_IF + h:MISC_IF + h + 1]
        bh = bh_all[:, MISC_IF + N_HEADS + h:MISC_IF + N_HEADS + h + 1]
        m_prev = m_old[:, h:h + 1]
        b3 = [t.astype(F32) for t in _split3(bh)]
        c3 = [t.astype(F32) for t in _split3(ig - bh)]
        lhs = jnp.where(lane128 == 0, b3[0], jnp.where(lane128 == 1, b3[1], jnp.where(
            lane128 == 2, b3[2], jnp.where(lane128 < 6, 1.0, 0.0)))).astype(BF16)
        rhs = jnp.where(lane128 == 3, c3[0], jnp.where(lane128 == 4, c3[1], jnp.where(
            lane128 == 5, c3[2], jnp.where(lane128 < 3, 1.0, 0.0)))).astype(BF16)
        dm = jnp.where(tril, lax.dot_general(lhs, rhs, (((1,), (1,)), ((), ())),
                                             preferred_element_type=F32), -jnp.inf)
        m_state = bh + m_prev
        m_t = jnp.maximum(m_state, jnp.max(dm, axis=-1, keepdims=True))
        sc = _dot_nt(jnp.where(head, q, 0.0), kb) * jnp.exp(dm - m_t)
        g = jnp.exp(m_state - m_t)
        qn = jnp.sum(jnp.where(head, q * n_old, 0.0), axis=-1, keepdims=True)
        den = jnp.sum(sc, axis=-1, keepdims=True) + g * qn
        num = _dot(sc, vb) + g * q_c
        hh = hh + jnp.where(head, num / jnp.maximum(jnp.abs(den), jnp.exp(-m_t)), 0.0)
        m_new = m_t[lc - 1:lc, :]
        bh_last = bh[lc - 1:lc, :]
        ws = jnp.exp(bh_last - bh + ig - m_new)
        gl = jnp.exp(bh_last + m_prev - m_new)
        ws_full = ws_full + jnp.where(head, ws, 0.0)
        gl_row = gl_row + jnp.where(head, gl, 0.0)
        m_new_row = jnp.where(lane128[0:1, :] == h, m_new, m_new_row)

    r_head = lax.broadcasted_iota(jnp.int32, (GROUP, GROUP), 0) // HEAD_DIM
    c_head = lax.broadcasted_iota(jnp.int32, (GROUP, GROUP), 1) // HEAD_DIM
    c_new = c_old * gl_row + jnp.where(r_head == c_head, _dot_tn(v * ws_full, kb), 0.0)
    n_new = gl_row * n_old + jnp.sum(ws_full * k, axis=0, keepdims=True)
    c_scr[...] = c_new
    n_scr[...] = n_new
    m_scr[...] = m_new_row
    cout_ref[...] = c_new
    nout_ref[...] = n_new
    mout_ref[...] = m_new_row
    y_ref[...] = (_head_rms(hh, on_ref[...]) * _sigmoid(o_ref[...])).astype(BF16)


FFN_CHUNKS = ((0, 1024), (1024, 2048), (2048, FFN_HIDDEN))


def _out_kernel(x_ref, ya_ref, yb_ref, yc_ref, yd_ref, wo_ref, nf_ref, wgu_ref, wd_ref, o_ref):
    y = jnp.concatenate([r[...] for r in (ya_ref, yb_ref, yc_ref, yd_ref)], axis=1)
    x1 = x_ref[...] + jnp.dot(y, wo_ref[...], preferred_element_type=F32)
    h = _rms_rows(x1, nf_ref[...]).astype(BF16)
    ffn = None
    for lo, hi in FFN_CHUNKS:
        g = jnp.dot(h, wgu_ref[:, lo:hi], preferred_element_type=F32)
        u = jnp.dot(h, wgu_ref[:, FFN_HIDDEN + lo:FFN_HIDDEN + hi], preferred_element_type=F32)
        a = (g * _sigmoid(g) * u).astype(BF16)
        d = jnp.dot(a, wd_ref[lo:hi, :], preferred_element_type=F32)
        ffn = d if ffn is None else ffn + d
    o_ref[...] = x1 + ffn


def _out_proj_ffn(x, ys, w_out, norm_ffn, w_gu, w_down, layer, tm):
    rows = x.shape[0]
    ytile = pl.BlockSpec((tm, GROUP), lambda i: (i, 0))
    return pl.pallas_call(
        _out_kernel,
        grid=(rows // tm,),
        in_specs=[pl.BlockSpec((tm, D_MODEL), lambda i: (i, 0)), ytile, ytile, ytile, ytile,
                  pl.BlockSpec((None, D_MODEL, D_MODEL), lambda i: (layer, 0, 0)),
                  pl.BlockSpec((None, 1, D_MODEL), lambda i: (layer, 0, 0)),
                  pl.BlockSpec((None, D_MODEL, 2 * FFN_HIDDEN), lambda i: (layer, 0, 0)),
                  pl.BlockSpec((None, FFN_HIDDEN, D_MODEL), lambda i: (layer, 0, 0))],
        out_specs=pl.BlockSpec((tm, D_MODEL), lambda i: (i, 0)),
        out_shape=jax.ShapeDtypeStruct((rows, D_MODEL), F32),
        compiler_params=pltpu.CompilerParams(dimension_semantics=("arbitrary",),
                                             vmem_limit_bytes=VMEM_LIMIT),
        name="out_proj_ffn",
    )(x, *ys, w_out, norm_ffn, w_gu, w_down)


def _seq_params():
    return pltpu.CompilerParams(dimension_semantics=("arbitrary", "arbitrary"),
                                vmem_limit_bytes=VMEM_LIMIT)


def _row_block(row0, seq, tb):
    base, per_seq = row0 // tb, seq // tb
    return lambda col: (lambda b, j: (base + b * per_seq + j, col))


def _lru_call(proj, y_prev, conv0, h0, lw, layer, nseq, seq, row0, tb):
    rb = _row_block(row0, seq, tb)
    wl = lambda shape: pl.BlockSpec((None,) + shape, lambda b, j: (layer,) + (0,) * len(shape))
    per_seq = lambda shape: pl.BlockSpec((None,) + shape, lambda b, j: (b,) + (0,) * len(shape))
    in_specs = [pl.BlockSpec((tb, GROUP), rb(COL_LRU_X // GROUP)),
                pl.BlockSpec((tb, GROUP), rb(COL_LRU_G // GROUP)),
                per_seq((8, GROUP)), per_seq((1, GROUP)),
                wl((CONV_W, GROUP)), wl((1, GROUP)), wl((GROUP, 2 * GROUP)), wl((1, 2 * GROUP)),
                wl((1, GROUP)), wl((1, GROUP))]
    args = [proj, proj, conv0, h0, lw['conv_w'], lw['conv_b'], lw['lru_wg'], lw['lru_bg'],
            lw['lru_lambda'], lw['on_a']]
    aliases = {}
    if y_prev is not None:
        in_specs.append(pl.BlockSpec(memory_space=pl.ANY))
        args.append(y_prev)
        aliases = {len(args) - 1: 0}
    rows = proj.shape[0]
    return pl.pallas_call(
        functools.partial(_lru_kernel, aliased=y_prev is not None),
        grid=(nseq, seq // tb),
        in_specs=in_specs,
        out_specs=[pl.BlockSpec((tb, GROUP), rb(0)), per_seq((1, GROUP)), per_seq((8, GROUP))],
        out_shape=[jax.ShapeDtypeStruct((rows, GROUP), BF16),
                   jax.ShapeDtypeStruct((nseq, 1, GROUP), F32),
                   jax.ShapeDtypeStruct((nseq, 8, GROUP), F32)],
        scratch_shapes=[pltpu.VMEM((tb + 8, GROUP), F32), pltpu.VMEM((1, GROUP), F32)],
        input_output_aliases=aliases,
        compiler_params=_seq_params(),
        name="rglru",
    )(*args)


def _mla_pre_call(proj, tabs, lw, layer, tm):
    rows = proj.shape[0]
    wl = lambda shape: pl.BlockSpec((None,) + shape, lambda i: (layer,) + (0,) * len(shape))
    tab = pl.BlockSpec((tm, 128), lambda i: (i, 0))
    return pl.pallas_call(
        _mla_pre_kernel,
        grid=(rows // tm,),
        in_specs=[pl.BlockSpec((tm, GROUP), lambda i: (i, COL_Q_LAT // GROUP)),
                  pl.BlockSpec((tm, 128), lambda i: (i, COL_KV_LAT // 128)),
                  pl.BlockSpec((tm, 128), lambda i: (i, COL_MISC // 128)),
                  tab, tab, tab,
                  wl((1, GROUP)), wl((GROUP, 512)), wl((1, 512)), wl((1, KV_RANK)), wl((1, 128)),
                  wl((KV_RANK, 1024)), wl((1, 512))],
        out_specs=[pl.BlockSpec((tm, 512), lambda i: (i, 0))] * 3
                  + [pl.BlockSpec((tm, KV_RANK), lambda i: (i, 0)),
                     pl.BlockSpec((tm, ROPE_DIM), lambda i: (i, 0))],
        out_shape=[jax.ShapeDtypeStruct((rows, 512), BF16)] * 3
                  + [jax.ShapeDtypeStruct((rows, KV_RANK), F32),
                     jax.ShapeDtypeStruct((rows, ROPE_DIM), F32)],
        compiler_params=pltpu.CompilerParams(dimension_semantics=("arbitrary",),
                                             vmem_limit_bytes=VMEM_LIMIT),
        name="mla_pre",
    )(proj, proj, proj, *tabs, lw['q_norm'], lw['wq'], lw['gq'], lw['kv_norm'], lw['kr_norm'],
      lw['wkv'], lw['gk'])


def _kv_past_call(ckv, krope, lw, tr):
    depth, rows, _ = ckv.shape
    wl = lambda shape: pl.BlockSpec((None,) + shape, lambda l, i: (l,) + (0,) * len(shape))
    return pl.pallas_call(
        _kv_past_kernel,
        grid=(depth, rows // tr),
        in_specs=[pl.BlockSpec((None, tr, KV_RANK), lambda l, i: (l, i, 0)),
                  pl.BlockSpec((None, tr, ROPE_DIM), lambda l, i: (l, i, 0)),
                  wl((KV_RANK, 1024)), wl((1, 512))],
        out_specs=[pl.BlockSpec((None, tr, 512), lambda l, i: (l, i, 0))] * 2,
        out_shape=[jax.ShapeDtypeStruct((depth, rows, 512), BF16)] * 2,
        compiler_params=_seq_params(),
        name="kv_past",
    )(ckv, krope, lw['wkv'], lw['gk'])


def _attn_call(q, k, v, k_past, v_past, y_prev, lw, layer, nseq, seq, row0, tq, tka):
    rb = _row_block(row0, seq, tq)
    wl = lambda shape: pl.BlockSpec((None,) + shape, lambda b, j: (layer,) + (0,) * len(shape))
    qspec = pl.BlockSpec((tq, 512), rb(0))
    if k_past is None:
        assert row0 == 0
        ka, va = k, v
        past_spec = pl.BlockSpec((seq, 512), lambda b, j: (b, 0))
        n_past_static = None
    else:
        ka, va = k_past, v_past
        past_len = k_past.shape[1] // nseq
        past_spec = pl.BlockSpec((None, past_len, 512), lambda b, j: (layer, b, 0))
        n_past_static = past_len // tka
    in_specs = [qspec, past_spec, past_spec, qspec, qspec, wl((1, GROUP))]
    args = [q, ka, va, k, v, lw['on_b']]
    aliases = {}
    if y_prev is not None:
        in_specs.append(pl.BlockSpec(memory_space=pl.ANY))
        args.append(y_prev)
        aliases = {len(args) - 1: 0}
    return pl.pallas_call(
        functools.partial(_attn_kernel, aliased=y_prev is not None, tka=tka, n_past_static=n_past_static),
        grid=(nseq, seq // tq),
        in_specs=in_specs,
        out_specs=pl.BlockSpec((tq, GROUP), rb(0)),
        out_shape=jax.ShapeDtypeStruct((q.shape[0], GROUP), BF16),
        input_output_aliases=aliases,
        compiler_params=_seq_params(),
        name="mla_attn",
    )(*args)


def _ret_call(proj, tabs, y_prev, s0, lw, layer, nseq, seq, row0, lc):
    rb = _row_block(row0, seq, lc)
    wl = lambda shape: pl.BlockSpec((None,) + shape, lambda b, j: (layer,) + (0,) * len(shape))
    per_seq = lambda shape: pl.BlockSpec((None,) + shape, lambda b, j: (b,) + (0,) * len(shape))
    blk = lambda col: pl.BlockSpec((lc, GROUP), rb(col // GROUP))
    tab = pl.BlockSpec((lc, 128), rb(0))
    in_specs = [blk(COL_R_Q), blk(COL_R_K), blk(COL_R_V), blk(COL_R_G), tab, tab, tab,
                per_seq((GROUP, GROUP)), wl((1, GROUP))]
    args = [proj, proj, proj, proj, *tabs, s0, lw['on_c']]
    aliases = {}
    if y_prev is not None:
        in_specs.append(pl.BlockSpec(memory_space=pl.ANY))
        args.append(y_prev)
        aliases = {len(args) - 1: 0}
    return pl.pallas_call(
        functools.partial(_ret_kernel, aliased=y_prev is not None),
        grid=(nseq, seq // lc),
        in_specs=in_specs,
        out_specs=[pl.BlockSpec((lc, GROUP), rb(0)), per_seq((GROUP, GROUP))],
        out_shape=[jax.ShapeDtypeStruct((proj.shape[0], GROUP), BF16),
                   jax.ShapeDtypeStruct((nseq, GROUP, GROUP), F32)],
        scratch_shapes=[pltpu.VMEM((GROUP, GROUP), F32)],
        input_output_aliases=aliases,
        compiler_params=_seq_params(),
        name="retention",
    )(*args)


def _mlstm_call(proj, y_prev, c0, n0, m0, lw, layer, nseq, seq, row0, lc):
    rb = _row_block(row0, seq, lc)
    wl = lambda shape: pl.BlockSpec((None,) + shape, lambda b, j: (layer,) + (0,) * len(shape))
    per_seq = lambda shape: pl.BlockSpec((None,) + shape, lambda b, j: (b,) + (0,) * len(shape))
    blk = lambda col: pl.BlockSpec((lc, GROUP), rb(col // GROUP))
    in_specs = [blk(COL_M_Q), blk(COL_M_K), blk(COL_M_V), blk(COL_M_O),
                pl.BlockSpec((lc, 128), rb(COL_MISC // 128)), wl((1, 128)),
                per_seq((GROUP, GROUP)), per_seq((1, GROUP)), per_seq((1, 128)), wl((1, GROUP))]
    args = [proj, proj, proj, proj, proj, lw['b_if'], c0, n0, m0, lw['on_d']]
    aliases = {}
    if y_prev is not None:
        in_specs.append(pl.BlockSpec(memory_space=pl.ANY))
        args.append(y_prev)
        aliases = {len(args) - 1: 0}
    return pl.pallas_call(
        functools.partial(_mlstm_kernel, aliased=y_prev is not None),
        grid=(nseq, seq // lc),
        in_specs=in_specs,
        out_specs=[pl.BlockSpec((lc, GROUP), rb(0)), per_seq((GROUP, GROUP)), per_seq((1, GROUP)),
                   per_seq((1, 128))],
        out_shape=[jax.ShapeDtypeStruct((proj.shape[0], GROUP), BF16),
                   jax.ShapeDtypeStruct((nseq, GROUP, GROUP), F32),
                   jax.ShapeDtypeStruct((nseq, 1, GROUP), F32),
                   jax.ShapeDtypeStruct((nseq, 1, 128), F32)],
        scratch_shapes=[pltpu.VMEM((GROUP, GROUP), F32), pltpu.VMEM((1, GROUP), F32),
                        pltpu.VMEM((1, 128), F32)],
        input_output_aliases=aliases,
        compiler_params=_seq_params(),
        name="mlstm",
    )(*args)


def _block_diag(s):
    b, h, d, e = s.shape
    eye = jnp.eye(h, dtype=s.dtype)
    return (s[:, :, :, None, :] * eye[None, :, None, :, None]).reshape(b, h * d, h * e)


def _block_diag_extract(s, h):
    b, n, _ = s.shape
    d = n // h
    s5 = s.reshape(b, h, d, h, d)
    return jnp.stack([s5[:, i, :, i, :] for i in range(h)], axis=1)


def _rope_tables(pos, half, lanes, lo):
    inv = ROPE_THETA ** (-jnp.arange(half, dtype=F32) / half)
    ang = pos[:, None] * inv[None, :]
    cos, sin = jnp.cos(ang), jnp.sin(ang)
    n = pos.shape[0]
    c = jnp.ones((n, lanes), F32).at[:, lo:lo + 2 * half].set(jnp.concatenate([cos, cos], axis=1))
    sp = jnp.zeros((n, lanes), F32).at[:, lo + half:lo + 2 * half].set(sin)
    sm = jnp.zeros((n, lanes), F32).at[:, lo:lo + half].set(-sin)
    return c, sp, sm


def _prep_weights(norm_mix, w_in, lru_conv_w, lru_conv_b, lru_wa, lru_ba, lru_wx, lru_bx, lru_lambda,
                  mla_q_norm, mla_wq_b, mla_qn_norm, mla_qr_norm, mla_kv_norm, mla_kr_norm, mla_wkv_b,
                  mla_kn_norm, mlstm_b_if, out_norm, w_out, norm_ffn, w_gu, w_down):
    depth = w_in.shape[0]
    row = lambda a: a.reshape(depth, 1, -1)
    n_if = 2 * N_HEADS
    split = COL_MISC + ROPE_DIM
    w_in_pad = jnp.concatenate(
        [w_in[:, :, :split], w_in[:, :, -n_if:],
         jnp.zeros((depth, D_MODEL, COL_R_Q - split - n_if), w_in.dtype), w_in[:, :, split:-n_if]],
        axis=2).astype(BF16)

    def bd(w):
        eye = jnp.eye(N_HEADS, dtype=w.dtype)
        return (w[:, :, :, None, :] * eye[None, :, None, :, None]).reshape(depth, GROUP, GROUP)

    lru_wg = jnp.concatenate([bd(lru_wa), bd(lru_wx)], axis=2).astype(BF16)
    lru_bg = jnp.concatenate([lru_ba, lru_bx], axis=1)

    wq = mla_wq_b.reshape(depth, GROUP, N_HEADS, HEAD_DIM + ROPE_DIM)
    wq = jnp.pad(wq, ((0, 0), (0, 0), (0, 0), (0, 128 - HEAD_DIM - ROPE_DIM))).reshape(depth, GROUP, 512)
    gq = jnp.concatenate([mla_qn_norm, mla_qr_norm, jnp.zeros((depth, 32), F32)], axis=1)
    gq = jnp.tile(gq, (1, N_HEADS))
    wkv = mla_wkv_b.reshape(depth, KV_RANK, N_HEADS, 2 * HEAD_DIM)
    pad_head = lambda w: jnp.pad(w, ((0, 0), (0, 0), (0, 0), (0, 128 - HEAD_DIM))).reshape(depth, KV_RANK, 512)
    wkv = jnp.concatenate([pad_head(wkv[..., :HEAD_DIM]), pad_head(wkv[..., HEAD_DIM:])], axis=2)
    gk = jnp.tile(jnp.concatenate([mla_kn_norm, jnp.zeros((depth, 128 - HEAD_DIM), F32)], axis=1), (1, N_HEADS))
    kr_norm = jnp.pad(mla_kr_norm, ((0, 0), (0, 128 - ROPE_DIM)))
    b_if = jnp.pad(mlstm_b_if, ((0, 0), (MISC_IF, 128 - MISC_IF - n_if)))

    return dict(
        norm_mix=row(norm_mix), w_in=w_in_pad,
        conv_w=lru_conv_w, conv_b=row(lru_conv_b), lru_wg=lru_wg, lru_bg=row(lru_bg), lru_lambda=row(lru_lambda),
        q_norm=row(mla_q_norm), wq=wq.astype(BF16), gq=row(gq), kv_norm=row(mla_kv_norm), kr_norm=row(kr_norm),
        wkv=wkv.astype(BF16), gk=row(gk), b_if=row(b_if),
        on_a=row(out_norm[:, :GROUP]), on_b=row(out_norm[:, GROUP:2 * GROUP]),
        on_c=row(out_norm[:, 2 * GROUP:3 * GROUP]), on_d=row(out_norm[:, 3 * GROUP:]),
        w_out=w_out.astype(BF16), norm_ffn=row(norm_ffn), w_gu=w_gu.astype(BF16), w_down=w_down.astype(BF16))


def _tile_rows(rows, cap):
    t = cap
    while rows % t:
        t //= 2
    return t


def kernel(x_prompt, x_sample, cache_mla_ckv, cache_mla_krope, state_lru_h, state_lru_conv, state_ret, state_mlstm_C, state_mlstm_n, state_mlstm_m, norm_mix, w_in, lru_conv_w, lru_conv_b, lru_wa, lru_ba, lru_wx, lru_bx, lru_lambda, mla_q_norm, mla_wq_b, mla_qn_norm, mla_qr_norm, mla_kv_norm, mla_kr_norm, mla_wkv_b, mla_kn_norm, mlstm_b_if, out_norm, w_out, norm_ffn, w_gu, w_down):
    bp, tp, _ = x_prompt.shape
    bs, ts, _ = x_sample.shape
    depth, _, past, _ = cache_mla_ckv.shape
    rows_p, rows_s = bp * tp, bs * ts
    rows = rows_p + rows_s
    lw = _prep_weights(norm_mix, w_in, lru_conv_w, lru_conv_b, lru_wa, lru_ba, lru_wx, lru_bx, lru_lambda,
                       mla_q_norm, mla_wq_b, mla_qn_norm, mla_qr_norm, mla_kv_norm, mla_kr_norm, mla_wkv_b,
                       mla_kn_norm, mlstm_b_if, out_norm, w_out, norm_ffn, w_gu, w_down)

    pos = jnp.concatenate([jnp.tile(jnp.arange(tp, dtype=F32), bp),
                           jnp.tile(jnp.arange(past, past + ts, dtype=F32), bs)])
    tabs_mla = _rope_tables(pos, ROPE_DIM // 2, 128, HEAD_DIM)
    c64, sp64, sm64 = _rope_tables(pos, HEAD_DIM // 2, HEAD_DIM, 0)
    tabs_ret = tuple(jnp.concatenate([t, t], axis=1) for t in (c64, sp64, sm64))

    tm = _tile_rows(rows, 512)
    tb_p, tb_s = min(tp, 256), min(ts, 256)
    lc_p, lc_s = min(tp, 128), min(ts, 128)
    tq_p, tq_s = min(tp, 128), min(ts, 128)
    tka_s = _tile_rows(past, 512)

    k_past, v_past = _kv_past_call(cache_mla_ckv.reshape(depth, bs * past, KV_RANK),
                                   cache_mla_krope.reshape(depth, bs * past, ROPE_DIM), lw,
                                   _tile_rows(bs * past, 1024))

    zeros = lambda *shape: jnp.zeros(shape, F32)
    pad_m = lambda m: jnp.pad(m, ((0, 0), (0, 128 - N_HEADS)))[:, None, :]
    x = jnp.concatenate([x_prompt.reshape(rows_p, D_MODEL), x_sample.reshape(rows_s, D_MODEL)], axis=0)
    p_acc = [[] for _ in range(8)]
    s_acc = [[] for _ in range(8)]
    for l in range(depth):
        proj = _in_proj(x, lw['norm_mix'], lw['w_in'], l, tm)

        conv_s = jnp.pad(state_lru_conv[l], ((0, 0), (8 - (CONV_W - 1), 0), (0, 0)))
        ya, h_p, conv_p = _lru_call(proj, None, zeros(bp, 8, GROUP), zeros(bp, 1, GROUP), lw, l, bp, tp, 0, tb_p)
        ya, h_s, conv_o = _lru_call(proj, ya, conv_s, state_lru_h[l][:, None, :], lw, l, bs, ts, rows_p, tb_s)

        q, k, v, ckv, krope = _mla_pre_call(proj, tabs_mla, lw, l, tm)
        yb = _attn_call(q, k, v, None, None, None, lw, l, bp, tp, 0, tq_p, tq_p)
        yb = _attn_call(q, k, v, k_past, v_past, yb, lw, l, bs, ts, rows_p, tq_s, tka_s)

        yc, s_p = _ret_call(proj, tabs_ret, None, zeros(bp, GROUP, GROUP), lw, l, bp, tp, 0, lc_p)
        yc, s_s = _ret_call(proj, tabs_ret, yc, _block_diag(state_ret[l]), lw, l, bs, ts, rows_p, lc_s)

        yd, c_p, n_p, m_p = _mlstm_call(proj, None, zeros(bp, GROUP, GROUP), zeros(bp, 1, GROUP),
                                        zeros(bp, 1, 128), lw, l, bp, tp, 0, lc_p)
        yd, c_s, n_s, m_s = _mlstm_call(proj, yd, _block_diag(state_mlstm_C[l]), state_mlstm_n[l].reshape(bs, 1, GROUP),
                                        pad_m(state_mlstm_m[l]), lw, l, bs, ts, rows_p, lc_s)

        x = _out_proj_ffn(x, (ya, yb, yc, yd), lw['w_out'], lw['norm_ffn'], lw['w_gu'], lw['w_down'], l, tm)

        outs_p = (ckv[:rows_p].reshape(bp, tp, KV_RANK), krope[:rows_p].reshape(bp, tp, ROPE_DIM),
                  h_p[:, 0], conv_p[:, 8 - (CONV_W - 1):], _block_diag_extract(s_p, N_HEADS),
                  _block_diag_extract(c_p, N_HEADS), n_p.reshape(bp, N_HEADS, HEAD_DIM), m_p[:, 0, :N_HEADS])
        outs_s = (ckv[rows_p:].reshape(bs, ts, KV_RANK), krope[rows_p:].reshape(bs, ts, ROPE_DIM),
                  h_s[:, 0], conv_o[:, 8 - (CONV_W - 1):], _block_diag_extract(s_s, N_HEADS),
                  _block_diag_extract(c_s, N_HEADS), n_s.reshape(bs, N_HEADS, HEAD_DIM), m_s[:, 0, :N_HEADS])
        for acc, o in zip(p_acc, outs_p):
            acc.append(o)
        for acc, o in zip(s_acc, outs_s):
            acc.append(o)

    yp = x[:rows_p].reshape(bp, tp, D_MODEL)
    ys = x[rows_p:].reshape(bs, ts, D_MODEL)
    return (yp, ys) + tuple(jnp.stack(a) for a in p_acc) + tuple(jnp.stack(a) for a in s_acc)
```

```python
import functools
import math

import jax
import jax.numpy as jnp
import numpy as np
from jax import lax
from jax.experimental import pallas as pl
from jax.experimental.pallas import tpu as pltpu

F32 = jnp.float32
BF16 = jnp.bfloat16

D_MODEL = 1024
CHUNK = 64
HEAD_DIM = 64
GROUP = 256
N_HEADS = 4
RMS_EPS = 1e-6
ROPE_THETA = 10000.0
CONV_W = 4
LRU_C = 8.0
KV_RANK = 128
ROPE_DIM = 32
FFN_HIDDEN = 2816
IN_PAD = 3072

COL_LRU_X, COL_LRU_G, COL_Q_LAT = 0, 256, 512
COL_KV_LAT, COL_MISC = 768, 896
COL_R_Q, COL_R_K, COL_R_V, COL_R_G = 1024, 1280, 1536, 1792
COL_M_Q, COL_M_K, COL_M_V, COL_M_O = 2048, 2304, 2560, 2816
MISC_IF = 32
ATT_SCALE = (HEAD_DIM + ROPE_DIM) ** -0.5
RET_LOG_DECAY = tuple(math.log(1.0 - 2.0 ** (-5.0 - h)) for h in range(N_HEADS))

VMEM_LIMIT = 56 * 1024 * 1024


def _dot(a, b):
    return jnp.dot(a.astype(BF16), b.astype(BF16), preferred_element_type=F32)


def _dot_nt(a, b):
    return lax.dot_general(a.astype(BF16), b.astype(BF16), (((1,), (1,)), ((), ())),
                           preferred_element_type=F32)


def _dot_tn(a, b):
    return lax.dot_general(a.astype(BF16), b.astype(BF16), (((0,), (0,)), ((), ())),
                           preferred_element_type=F32)


def _split3(x):
    hi = x.astype(BF16)
    r1 = x - hi.astype(F32)
    mid = r1.astype(BF16)
    lo = (r1 - mid.astype(F32)).astype(BF16)
    return hi, mid, lo


def _dot_f32_lhs(x, m):
    hi, mid, lo = _split3(x)
    return (jnp.dot(hi, m, preferred_element_type=F32) + jnp.dot(mid, m, preferred_element_type=F32)
            + jnp.dot(lo, m, preferred_element_type=F32))


def _rms_rows(x, g):
    return x * lax.rsqrt(jnp.mean(x * x, axis=-1, keepdims=True) + RMS_EPS) * g


def _lane_head(width, head_width):
    return lax.broadcasted_iota(jnp.int32, (1, width), 1) // head_width


def _head_mean_matrix(width, head_width):
    r = lax.broadcasted_iota(jnp.int32, (width, width), 0) // head_width
    c = lax.broadcasted_iota(jnp.int32, (width, width), 1) // head_width
    return jnp.where(r == c, 1.0 / head_width, 0.0).astype(BF16)


def _head_rms(y, g):
    ms = _dot_f32_lhs(y * y, _head_mean_matrix(GROUP, HEAD_DIM))
    return y * lax.rsqrt(ms + RMS_EPS) * g


def _shift_rows(x, d, fill):
    rows = lax.broadcasted_iota(jnp.int32, x.shape, 0)
    return jnp.where(rows >= d, pltpu.roll(x, d, 0), fill)


def _cumsum_rows(x):
    d = 1
    while d < x.shape[0]:
        x = x + _shift_rows(x, d, 0.0)
        d *= 2
    return x


def _softplus(z):
    return jnp.maximum(z, 0.0) + jnp.log1p(jnp.exp(-jnp.abs(z)))


def _sigmoid(z):
    return 1.0 / (1.0 + jnp.exp(-z))


def _gelu_tanh(z):
    return 0.5 * z * (1.0 + jnp.tanh(math.sqrt(2.0 / math.pi) * (z + 0.044715 * (z * z * z))))


def _rope_lanes(x, c, sp, sm, half):
    w = x.shape[1]
    return x * c + pltpu.roll(x, half, 1) * sp + pltpu.roll(x, w - half, 1) * sm


def _tile_lanes(t, n):
    return jnp.concatenate([t] * n, axis=1) if n > 1 else t


def _in_proj_kernel(x_ref, g_ref, w_ref, o_ref):
    h = _rms_rows(x_ref[...], g_ref[...])
    o_ref[...] = jnp.dot(h.astype(BF16), w_ref[...], preferred_element_type=F32)


def _in_proj(x, norm_mix, w_in_pad, layer, tm):
    rows = x.shape[0]
    return pl.pallas_call(
        _in_proj_kernel,
        grid=(rows // tm,),
        in_specs=[pl.BlockSpec((tm, D_MODEL), lambda i: (i, 0)),
                  pl.BlockSpec((None, 1, D_MODEL), lambda i: (layer, 0, 0)),
                  pl.BlockSpec((None, D_MODEL, IN_PAD), lambda i: (layer, 0, 0))],
        out_specs=pl.BlockSpec((tm, IN_PAD), lambda i: (i, 0)),
        out_shape=jax.ShapeDtypeStruct((rows, IN_PAD), F32),
        compiler_params=pltpu.CompilerParams(dimension_semantics=("arbitrary",),
                                             vmem_limit_bytes=VMEM_LIMIT),
        name="in_proj",
    )(x, norm_mix, w_in_pad)


def _lru_kernel(x_ref, g_ref, conv0_ref, h0_ref, cw_ref, cb_ref, wg_ref, bg_ref, lam_ref, on_ref,
                *rest, aliased):
    y_ref, hout_ref, convout_ref, xp_scr, h_scr = rest[1:] if aliased else rest
    j = pl.program_id(1)
    tb = x_ref.shape[0]

    @pl.when(j == 0)
    def _():
        xp_scr[0:8, :] = conv0_ref[...]
        h_scr[...] = h0_ref[...]

    x = x_ref[...]
    xp_scr[8:8 + tb, :] = x
    cw = cw_ref[...]
    xc = (cb_ref[...] + cw[3:4] * x + cw[2:3] * xp_scr[7:7 + tb, :]
          + cw[1:2] * xp_scr[6:6 + tb, :] + cw[0:1] * xp_scr[5:5 + tb, :])
    tail = xp_scr[tb:tb + 8, :]
    xp_scr[0:8, :] = tail
    convout_ref[...] = tail

    gates = _dot(xc, wg_ref[...]) + bg_ref[...]
    r = _sigmoid(gates[:, :GROUP])
    i = _sigmoid(gates[:, GROUP:])
    log_a = -LRU_C * r * _softplus(-lam_ref[...])
    a = jnp.exp(log_a)
    u = jnp.sqrt(-jnp.tanh(log_a) * (a * a + 1.0)) * i * xc

    d = 1
    while d < tb:
        u = a * _shift_rows(u, d, 0.0) + u
        a = a * _shift_rows(a, d, 1.0)
        d *= 2
    h = u + a * h_scr[...]
    h_last = h[tb - 1:tb, :]
    h_scr[...] = h_last
    hout_ref[...] = h_last
    y_ref[...] = (_head_rms(h, on_ref[...]) * _gelu_tanh(g_ref[...])).astype(BF16)


def _mla_seg_matrix():
    r = lax.broadcasted_iota(jnp.int32, (512, 512), 0)
    c = lax.broadcasted_iota(jnp.int32, (512, 512), 1)
    same = (r // 128) == (c // 128)
    rl, cl = r % 128, c % 128
    nope = same & (rl < HEAD_DIM) & (cl < HEAD_DIM)
    rope = same & (rl >= HEAD_DIM) & (rl < HEAD_DIM + ROPE_DIM) & (cl >= HEAD_DIM) & (cl < HEAD_DIM + ROPE_DIM)
    return jnp.where(nope, 1.0 / HEAD_DIM, jnp.where(rope, 1.0 / ROPE_DIM, 0.0)).astype(BF16)


def _with_ones_lane(v):
    lane = lax.broadcasted_iota(jnp.int32, v.shape, 1) % 128
    return jnp.where(lane == HEAD_DIM, 1.0, v)


def _mla_pre_kernel(qlat_ref, kvlat_ref, misc_ref, c_ref, sp_ref, sm_ref,
                    qn_ref, wq_ref, gq_ref, kvn_ref, krn_ref, wkv_ref, gk_ref,
                    q_out, k_out, v_out, ckv_out, kr_out):
    seg = _mla_seg_matrix()
    c, sp, sm = c_ref[...], sp_ref[...], sm_ref[...]
    half = ROPE_DIM // 2

    qraw = _dot(_rms_rows(qlat_ref[...], qn_ref[...]), wq_ref[...])
    qh = qraw * lax.rsqrt(_dot_f32_lhs(qraw * qraw, seg) + RMS_EPS) * gq_ref[...]
    q = _rope_lanes(qh, _tile_lanes(c, 4), _tile_lanes(sp, 4), _tile_lanes(sm, 4), half)
    q_out[...] = q.astype(BF16)

    ckv = _rms_rows(kvlat_ref[...], kvn_ref[...])
    ckv_out[...] = ckv

    misc = misc_ref[...]
    lane = lax.broadcasted_iota(jnp.int32, misc.shape, 1)
    kr = jnp.where(lane < ROPE_DIM, misc, 0.0)
    kr = kr * lax.rsqrt(jnp.sum(kr * kr, axis=-1, keepdims=True) * (1.0 / ROPE_DIM) + RMS_EPS) * krn_ref[...]
    kr = _rope_lanes(pltpu.roll(kr, HEAD_DIM, 1), c, sp, sm, half)
    kr_out[...] = pltpu.roll(kr, 128 - HEAD_DIM, 1)[:, :ROPE_DIM]

    kv = _dot(ckv, wkv_ref[...])
    kraw = kv[:, :512]
    kn = kraw * lax.rsqrt(_dot_f32_lhs(kraw * kraw, seg) + RMS_EPS) * gk_ref[...]
    k_out[...] = (kn + _tile_lanes(kr, 4)).astype(BF16)
    v_out[...] = _with_ones_lane(kv[:, 512:]).astype(BF16)


def _kv_past_kernel(ckv_ref, kr_ref, wkv_ref, gk_ref, k_out, v_out):
    seg = _mla_seg_matrix()
    kv = _dot(ckv_ref[...], wkv_ref[...])
    kraw = kv[:, :512]
    kn = kraw * lax.rsqrt(_dot_f32_lhs(kraw * kraw, seg) + RMS_EPS) * gk_ref[...]
    r = lax.broadcasted_iota(jnp.int32, (ROPE_DIM, 128), 0)
    cidx = lax.broadcasted_iota(jnp.int32, (ROPE_DIM, 128), 1)
    place = jnp.where(cidx == r + HEAD_DIM, 1.0, 0.0).astype(BF16)
    kr = jnp.dot(kr_ref[...].astype(BF16), place, preferred_element_type=F32)
    k_out[...] = (kn + _tile_lanes(kr, 4)).astype(BF16)
    v_out[...] = _with_ones_lane(kv[:, 512:]).astype(BF16)


def _attn_kernel(q_ref, ka_ref, va_ref, kb_ref, vb_ref, on_ref, *rest, aliased, tka, n_past_static):
    y_ref, m_scr, acc_scr = rest[1:] if aliased else rest
    j = pl.program_id(1)
    tq = q_ref.shape[0]
    n_past = j * (tq // tka) if n_past_static is None else n_past_static
    key_c = lax.broadcasted_iota(jnp.int32, (tq, tq), 0) // CHUNK
    qry_c = lax.broadcasted_iota(jnp.int32, (tq, tq), 1) // CHUNK
    visible = key_c <= qry_c
    heads = [slice(128 * h, 128 * h + 128) for h in range(N_HEADS)]
    c = ATT_SCALE * math.log2(math.e)

    s = [jnp.where(visible, _dot_nt(kb_ref[:, hs], q_ref[:, hs]) * c, -jnp.inf) for hs in heads]
    m = [jnp.max(s[h], axis=0, keepdims=True) for h in range(N_HEADS)]
    p = [jnp.exp2(s[h] - m[h]).astype(BF16) for h in range(N_HEADS)]
    pv = [_dot_tn(vb_ref[:, hs], p[h]) for h, hs in enumerate(heads)]
    for h in range(N_HEADS):
        m_scr[h] = m[h]
        acc_scr[h] = pv[h]

    def body(t, carry):
        off = pl.multiple_of(t * tka, tka)
        m_old = [m_scr[h] for h in range(N_HEADS)]
        s = [_dot_nt(ka_ref[pl.ds(off, tka), hs], q_ref[:, hs]) * c for hs in heads]
        m_new = [jnp.maximum(m_old[h], jnp.max(s[h], axis=0, keepdims=True)) for h in range(N_HEADS)]
        p = [jnp.exp2(s[h] - m_new[h]).astype(BF16) for h in range(N_HEADS)]
        pv = [_dot_tn(va_ref[pl.ds(off, tka), hs], p[h]) for h, hs in enumerate(heads)]
        for h in range(N_HEADS):
            acc_scr[h] = jnp.exp2(m_old[h] - m_new[h]) * acc_scr[h] + pv[h]
            m_scr[h] = m_new[h]
        return carry

    lax.fori_loop(0, n_past, body, 0)
    outs = []
    for h in range(N_HEADS):
        a = acc_scr[h]
        outs.append(a[:HEAD_DIM, :] / a[HEAD_DIM:HEAD_DIM + 1, :])
    o = jnp.concatenate(outs, axis=0).T
    y_ref[...] = _head_rms(o, on_ref[...]).astype(BF16)


def _ret_kernel(q_ref, k_ref, v_ref, g_ref, c_ref, sp_ref, sm_ref, s0_ref, on_ref, *rest, aliased):
    y_ref, sout_ref, s_scr = rest[1:] if aliased else rest
    j = pl.program_id(1)
    lc = q_ref.shape[0]

    @pl.when(j == 0)
    def _():
        s_scr[...] = s0_ref[...]

    c, sp, sm = (_tile_lanes(t[...], 2) for t in (c_ref, sp_ref, sm_ref))
    half = HEAD_DIM // 2
    q = _rope_lanes(q_ref[...], c, sp, sm, half)
    k = _rope_lanes(k_ref[...], c, sp, sm, half) * (HEAD_DIM ** -0.5)
    v = v_ref[...].astype(BF16)

    lane_head = _lane_head(GROUP, HEAD_DIM)
    lg_lane = jnp.zeros((1, GROUP), F32)
    for h in range(N_HEADS):
        lg_lane = jnp.where(lane_head == h, RET_LOG_DECAY[h], lg_lane)
    t_col = lax.broadcasted_iota(jnp.int32, (lc, 1), 0).astype(F32)
    cross = jnp.exp((t_col + 1.0) * lg_lane)
    tail = jnp.exp((lc - 1.0 - t_col) * lg_lane)
    whole = jnp.exp(float(lc) * lg_lane)
    ti = lax.broadcasted_iota(jnp.int32, (lc, lc), 0)
    si = lax.broadcasted_iota(jnp.int32, (lc, lc), 1)
    causal = ti >= si
    diff = jnp.where(causal, ti - si, 0).astype(F32)

    s_old = s_scr[...]
    y = _dot(q, s_old) * cross
    kb = k.astype(BF16)
    hs = range(N_HEADS)
    att = [_dot_nt(jnp.where(lane_head == h, q, 0.0), kb) for h in hs]
    att = [(att[h] * jnp.where(causal, jnp.exp(diff * RET_LOG_DECAY[h]), 0.0)).astype(BF16) for h in hs]
    yh = [_dot(att[h], v) for h in hs]
    for h in hs:
        y = y + jnp.where(lane_head == h, yh[h], 0.0)

    r_head = lax.broadcasted_iota(jnp.int32, (GROUP, GROUP), 0) // HEAD_DIM
    c_head = lax.broadcasted_iota(jnp.int32, (GROUP, GROUP), 1) // HEAD_DIM
    s_new = s_old * whole + jnp.where(r_head == c_head, _dot_tn(k * tail, v), 0.0)
    s_scr[...] = s_new
    sout_ref[...] = s_new
    y_ref[...] = (_head_rms(y, on_ref[...]) * (g_ref[...] * _sigmoid(g_ref[...]))).astype(BF16)


def _mlstm_kernel(q_ref, k_ref, v_ref, o_ref, misc_ref, bif_ref, c0_ref, n0_ref, m0_ref, on_ref,
                  *rest, aliased):
    y_ref, cout_ref, nout_ref, mout_ref, c_scr, n_scr, m_scr = rest[1:] if aliased else rest
    j = pl.program_id(1)
    lc = q_ref.shape[0]

    @pl.when(j == 0)
    def _():
        c_scr[...] = c0_ref[...]
        n_scr[...] = n0_ref[...]
        m_scr[...] = m0_ref[...]

    q = q_ref[...]
    k = k_ref[...] * (HEAD_DIM ** -0.5)
    kb = k.astype(BF16)
    v = v_ref[...]
    vb = v.astype(BF16)
    gates = misc_ref[...] + bif_ref[...]
    bh_all = _cumsum_rows(-_softplus(-gates))
    c_old, n_old, m_old = c_scr[...], n_scr[...], m_scr[...]

    lane_head = _lane_head(GROUP, HEAD_DIM)
    lane128 = lax.broadcasted_iota(jnp.int32, (lc, 128), 1)
    ti = lax.broadcasted_iota(jnp.int32, (lc, lc), 0)
    si = lax.broadcasted_iota(jnp.int32, (lc, lc), 1)
    tril = ti >= si
    q_c = _dot_nt(q, c_old)

    hh = jnp.zeros((lc, GROUP), F32)
    ws_full = jnp.zeros((lc, GROUP), F32)
    gl_row = jnp.zeros((1, GROUP), F32)
    m_new_row = m_old
    hs = range(N_HEADS)
    ig = [gates[:, MISC_IF + h:MISC_IF + h + 1] for h in hs]
    bh = [bh_all[:, MISC_IF + N_HEADS + h:MISC_IF + N_HEADS + h + 1] for h in hs]
    m_prev = [m_old[:, h:h + 1] for h in hs]

    def outer_sum(col, row_src):
        b3 = [t.astype(F32) for t in _split3(col)]
        c3 = [t.astype(F32) for t in _split3(row_src)]
        lhs = jnp.where(lane128 == 0, b3[0], jnp.where(lane128 == 1, b3[1], jnp.where(
            lane128 == 2, b3[2], jnp.where(lane128 < 6, 1.0, 0.0)))).astype(BF16)
        rhs = jnp.where(lane128 == 3, c3[0], jnp.where(lane128 == 4, c3[1], jnp.where(
            lane128 == 5, c3[2], jnp.where(lane128 < 3, 1.0, 0.0)))).astype(BF16)
        return lax.dot_general(lhs, rhs, (((1,), (1,)), ((), ())), preferred_element_type=F32)

    dm = [jnp.where(tril, outer_sum(bh[h], ig[h] - bh[h]), -jnp.inf) for h in hs]
    qk = [_dot_nt(jnp.where(lane_head == h, q, 0.0), kb) for h in hs]
    m_state = [bh[h] + m_prev[h] for h in hs]
    m_t = [jnp.maximum(m_state[h], jnp.max(dm[h], axis=-1, keepdims=True)) for h in hs]
    sc = [qk[h] * jnp.exp(dm[h] - m_t[h]) for h in hs]
    g = [jnp.exp(m_state[h] - m_t[h]) for h in hs]
    qn = [jnp.sum(jnp.where(lane_head == h, q * n_old, 0.0), axis=-1, keepdims=True) for h in hs]
    den = [jnp.sum(sc[h], axis=-1, keepdims=True) + g[h] * qn[h] for h in hs]
    scv = [_dot(sc[h], vb) for h in hs]
    for h in hs:
        num = scv[h] + g[h] * q_c
        hh = hh + jnp.where(lane_head == h, num / jnp.maximum(jnp.abs(den[h]), jnp.exp(-m_t[h])), 0.0)
        m_new = m_t[h][lc - 1:lc, :]
        bh_last = bh[h][lc - 1:lc, :]
        ws = jnp.exp(bh_last - bh[h] + ig[h] - m_new)
        gl = jnp.exp(bh_last + m_prev[h] - m_new)
        ws_full = ws_full + jnp.where(lane_head == h, ws, 0.0)
        gl_row = gl_row + jnp.where(lane_head == h, gl, 0.0)
        m_new_row = jnp.where(lane128[0:1, :] == h, m_new, m_new_row)

    r_head = lax.broadcasted_iota(jnp.int32, (GROUP, GROUP), 0) // HEAD_DIM
    c_head = lax.broadcasted_iota(jnp.int32, (GROUP, GROUP), 1) // HEAD_DIM
    c_new = c_old * gl_row + jnp.where(r_head == c_head, _dot_tn(v * ws_full, kb), 0.0)
    n_new = gl_row * n_old + jnp.sum(ws_full * k, axis=0, keepdims=True)
    c_scr[...] = c_new
    n_scr[...] = n_new
    m_scr[...] = m_new_row
    cout_ref[...] = c_new
    nout_ref[...] = n_new
    mout_ref[...] = m_new_row
    y_ref[...] = (_head_rms(hh, on_ref[...]) * _sigmoid(o_ref[...])).astype(BF16)


FFN_CHUNKS = ((0, 1024), (1024, 2048), (2048, FFN_HIDDEN))


def _out_kernel(x_ref, ya_ref, yb_ref, yc_ref, yd_ref, wo_ref, nf_ref, wgu_ref, wd_ref, o_ref):
    y = jnp.concatenate([r[...] for r in (ya_ref, yb_ref, yc_ref, yd_ref)], axis=1)
    x1 = x_ref[...] + jnp.dot(y, wo_ref[...], preferred_element_type=F32)
    h = _rms_rows(x1, nf_ref[...]).astype(BF16)
    ffn = None
    for lo, hi in FFN_CHUNKS:
        g = jnp.dot(h, wgu_ref[:, lo:hi], preferred_element_type=F32)
        u = jnp.dot(h, wgu_ref[:, FFN_HIDDEN + lo:FFN_HIDDEN + hi], preferred_element_type=F32)
        a = (g * _sigmoid(g) * u).astype(BF16)
        d = jnp.dot(a, wd_ref[lo:hi, :], preferred_element_type=F32)
        ffn = d if ffn is None else ffn + d
    o_ref[...] = x1 + ffn


def _out_proj_ffn(x, ys, w_out, norm_ffn, w_gu, w_down, layer, tm):
    rows = x.shape[0]
    ytile = pl.BlockSpec((tm, GROUP), lambda i: (i, 0))
    return pl.pallas_call(
        _out_kernel,
        grid=(rows // tm,),
        in_specs=[pl.BlockSpec((tm, D_MODEL), lambda i: (i, 0)), ytile, ytile, ytile, ytile,
                  pl.BlockSpec((None, D_MODEL, D_MODEL), lambda i: (layer, 0, 0)),
                  pl.BlockSpec((None, 1, D_MODEL), lambda i: (layer, 0, 0)),
                  pl.BlockSpec((None, D_MODEL, 2 * FFN_HIDDEN), lambda i: (layer, 0, 0)),
                  pl.BlockSpec((None, FFN_HIDDEN, D_MODEL), lambda i: (layer, 0, 0))],
        out_specs=pl.BlockSpec((tm, D_MODEL), lambda i: (i, 0)),
        out_shape=jax.ShapeDtypeStruct((rows, D_MODEL), F32),
        compiler_params=pltpu.CompilerParams(dimension_semantics=("arbitrary",),
                                             vmem_limit_bytes=VMEM_LIMIT),
        name="out_proj_ffn",
    )(x, *ys, w_out, norm_ffn, w_gu, w_down)


def _seq_params():
    return pltpu.CompilerParams(dimension_semantics=("arbitrary", "arbitrary"),
                                vmem_limit_bytes=VMEM_LIMIT)


def _row_block(row0, seq, tb):
    base, per_seq = row0 // tb, seq // tb
    return lambda col: (lambda b, j: (base + b * per_seq + j, col))


def _lru_call(proj, y_prev, conv0, h0, lw, layer, nseq, seq, row0, tb):
    rb = _row_block(row0, seq, tb)
    wl = lambda shape: pl.BlockSpec((None,) + shape, lambda b, j: (layer,) + (0,) * len(shape))
    per_seq = lambda shape: pl.BlockSpec((None,) + shape, lambda b, j: (b,) + (0,) * len(shape))
    in_specs = [pl.BlockSpec((tb, GROUP), rb(COL_LRU_X // GROUP)),
                pl.BlockSpec((tb, GROUP), rb(COL_LRU_G // GROUP)),
                per_seq((8, GROUP)), per_seq((1, GROUP)),
                wl((CONV_W, GROUP)), wl((1, GROUP)), wl((GROUP, 2 * GROUP)), wl((1, 2 * GROUP)),
                wl((1, GROUP)), wl((1, GROUP))]
    args = [proj, proj, conv0, h0, lw['conv_w'], lw['conv_b'], lw['lru_wg'], lw['lru_bg'],
            lw['lru_lambda'], lw['on_a']]
    aliases = {}
    if y_prev is not None:
        in_specs.append(pl.BlockSpec(memory_space=pl.ANY))
        args.append(y_prev)
        aliases = {len(args) - 1: 0}
    rows = proj.shape[0]
    return pl.pallas_call(
        functools.partial(_lru_kernel, aliased=y_prev is not None),
        grid=(nseq, seq // tb),
        in_specs=in_specs,
        out_specs=[pl.BlockSpec((tb, GROUP), rb(0)), per_seq((1, GROUP)), per_seq((8, GROUP))],
        out_shape=[jax.ShapeDtypeStruct((rows, GROUP), BF16),
                   jax.ShapeDtypeStruct((nseq, 1, GROUP), F32),
                   jax.ShapeDtypeStruct((nseq, 8, GROUP), F32)],
        scratch_shapes=[pltpu.VMEM((tb + 8, GROUP), F32), pltpu.VMEM((1, GROUP), F32)],
        input_output_aliases=aliases,
        compiler_params=_seq_params(),
        name="rglru",
    )(*args)


def _mla_pre_call(proj, tabs, lw, layer, tm):
    rows = proj.shape[0]
    wl = lambda shape: pl.BlockSpec((None,) + shape, lambda i: (layer,) + (0,) * len(shape))
    tab = pl.BlockSpec((tm, 128), lambda i: (i, 0))
    return pl.pallas_call(
        _mla_pre_kernel,
        grid=(rows // tm,),
        in_specs=[pl.BlockSpec((tm, GROUP), lambda i: (i, COL_Q_LAT // GROUP)),
                  pl.BlockSpec((tm, 128), lambda i: (i, COL_KV_LAT // 128)),
                  pl.BlockSpec((tm, 128), lambda i: (i, COL_MISC // 128)),
                  tab, tab, tab,
                  wl((1, GROUP)), wl((GROUP, 512)), wl((1, 512)), wl((1, KV_RANK)), wl((1, 128)),
                  wl((KV_RANK, 1024)), wl((1, 512))],
        out_specs=[pl.BlockSpec((tm, 512), lambda i: (i, 0))] * 3
                  + [pl.BlockSpec((tm, KV_RANK), lambda i: (i, 0)),
                     pl.BlockSpec((tm, ROPE_DIM), lambda i: (i, 0))],
        out_shape=[jax.ShapeDtypeStruct((rows, 512), BF16)] * 3
                  + [jax.ShapeDtypeStruct((rows, KV_RANK), F32),
                     jax.ShapeDtypeStruct((rows, ROPE_DIM), F32)],
        compiler_params=pltpu.CompilerParams(dimension_semantics=("arbitrary",),
                                             vmem_limit_bytes=VMEM_LIMIT),
        name="mla_pre",
    )(proj, proj, proj, *tabs, lw['q_norm'], lw['wq'], lw['gq'], lw['kv_norm'], lw['kr_norm'],
      lw['wkv'], lw['gk'])


def _kv_past_call(ckv, krope, lw, tr):
    depth, rows, _ = ckv.shape
    wl = lambda shape: pl.BlockSpec((None,) + shape, lambda l, i: (l,) + (0,) * len(shape))
    return pl.pallas_call(
        _kv_past_kernel,
        grid=(depth, rows // tr),
        in_specs=[pl.BlockSpec((None, tr, KV_RANK), lambda l, i: (l, i, 0)),
                  pl.BlockSpec((None, tr, ROPE_DIM), lambda l, i: (l, i, 0)),
                  wl((KV_RANK, 1024)), wl((1, 512))],
        out_specs=[pl.BlockSpec((None, tr, 512), lambda l, i: (l, i, 0))] * 2,
        out_shape=[jax.ShapeDtypeStruct((depth, rows, 512), BF16)] * 2,
        compiler_params=_seq_params(),
        name="kv_past",
    )(ckv, krope, lw['wkv'], lw['gk'])


def _attn_call(q, k, v, k_past, v_past, y_prev, lw, layer, nseq, seq, row0, tq, tka):
    rb = _row_block(row0, seq, tq)
    wl = lambda shape: pl.BlockSpec((None,) + shape, lambda b, j: (layer,) + (0,) * len(shape))
    qspec = pl.BlockSpec((tq, 512), rb(0))
    if k_past is None:
        assert row0 == 0
        ka, va = k, v
        past_spec = pl.BlockSpec((seq, 512), lambda b, j: (b, 0))
        n_past_static = None
    else:
        ka, va = k_past, v_past
        past_len = k_past.shape[1] // nseq
        past_spec = pl.BlockSpec((None, past_len, 512), lambda b, j: (layer, b, 0))
        n_past_static = past_len // tka
    in_specs = [qspec, past_spec, past_spec, qspec, qspec, wl((1, GROUP))]
    args = [q, ka, va, k, v, lw['on_b']]
    aliases = {}
    if y_prev is not None:
        in_specs.append(pl.BlockSpec(memory_space=pl.ANY))
        args.append(y_prev)
        aliases = {len(args) - 1: 0}
    return pl.pallas_call(
        functools.partial(_attn_kernel, aliased=y_prev is not None, tka=tka, n_past_static=n_past_static),
        grid=(nseq, seq // tq),
        in_specs=in_specs,
        out_specs=pl.BlockSpec((tq, GROUP), rb(0)),
        out_shape=jax.ShapeDtypeStruct((q.shape[0], GROUP), BF16),
        scratch_shapes=[pltpu.VMEM((N_HEADS, 1, tq), F32), pltpu.VMEM((N_HEADS, 128, tq), F32)],
        input_output_aliases=aliases,
        compiler_params=_seq_params(),
        name="mla_attn",
    )(*args)


def _ret_call(proj, tabs, y_prev, s0, lw, layer, nseq, seq, row0, lc):
    rb = _row_block(row0, seq, lc)
    wl = lambda shape: pl.BlockSpec((None,) + shape, lambda b, j: (layer,) + (0,) * len(shape))
    per_seq = lambda shape: pl.BlockSpec((None,) + shape, lambda b, j: (b,) + (0,) * len(shape))
    blk = lambda col: pl.BlockSpec((lc, GROUP), rb(col // GROUP))
    tab = pl.BlockSpec((lc, 128), rb(0))
    in_specs = [blk(COL_R_Q), blk(COL_R_K), blk(COL_R_V), blk(COL_R_G), tab, tab, tab,
                per_seq((GROUP, GROUP)), wl((1, GROUP))]
    args = [proj, proj, proj, proj, *tabs, s0, lw['on_c']]
    aliases = {}
    if y_prev is not None:
        in_specs.append(pl.BlockSpec(memory_space=pl.ANY))
        args.append(y_prev)
        aliases = {len(args) - 1: 0}
    return pl.pallas_call(
        functools.partial(_ret_kernel, aliased=y_prev is not None),
        grid=(nseq, seq // lc),
        in_specs=in_specs,
        out_specs=[pl.BlockSpec((lc, GROUP), rb(0)), per_seq((GROUP, GROUP))],
        out_shape=[jax.ShapeDtypeStruct((proj.shape[0], GROUP), BF16),
                   jax.ShapeDtypeStruct((nseq, GROUP, GROUP), F32)],
        scratch_shapes=[pltpu.VMEM((GROUP, GROUP), F32)],
        input_output_aliases=aliases,
        compiler_params=_seq_params(),
        name="retention",
    )(*args)


def _mlstm_call(proj, y_prev, c0, n0, m0, lw, layer, nseq, seq, row0, lc):
    rb = _row_block(row0, seq, lc)
    wl = lambda shape: pl.BlockSpec((None,) + shape, lambda b, j: (layer,) + (0,) * len(shape))
    per_seq = lambda shape: pl.BlockSpec((None,) + shape, lambda b, j: (b,) + (0,) * len(shape))
    blk = lambda col: pl.BlockSpec((lc, GROUP), rb(col // GROUP))
    in_specs = [blk(COL_M_Q), blk(COL_M_K), blk(COL_M_V), blk(COL_M_O),
                pl.BlockSpec((lc, 128), rb(COL_MISC // 128)), wl((1, 128)),
                per_seq((GROUP, GROUP)), per_seq((1, GROUP)), per_seq((1, 128)), wl((1, GROUP))]
    args = [proj, proj, proj, proj, proj, lw['b_if'], c0, n0, m0, lw['on_d']]
    aliases = {}
    if y_prev is not None:
        in_specs.append(pl.BlockSpec(memory_space=pl.ANY))
        args.append(y_prev)
        aliases = {len(args) - 1: 0}
    return pl.pallas_call(
        functools.partial(_mlstm_kernel, aliased=y_prev is not None),
        grid=(nseq, seq // lc),
        in_specs=in_specs,
        out_specs=[pl.BlockSpec((lc, GROUP), rb(0)), per_seq((GROUP, GROUP)), per_seq((1, GROUP)),
                   per_seq((1, 128))],
        out_shape=[jax.ShapeDtypeStruct((proj.shape[0], GROUP), BF16),
                   jax.ShapeDtypeStruct((nseq, GROUP, GROUP), F32),
                   jax.ShapeDtypeStruct((nseq, 1, GROUP), F32),
                   jax.ShapeDtypeStruct((nseq, 1, 128), F32)],
        scratch_shapes=[pltpu.VMEM((GROUP, GROUP), F32), pltpu.VMEM((1, GROUP), F32),
                        pltpu.VMEM((1, 128), F32)],
        input_output_aliases=aliases,
        compiler_params=_seq_params(),
        name="mlstm",
    )(*args)


def _block_diag(s):
    b, h, d, e = s.shape
    eye = jnp.eye(h, dtype=s.dtype)
    return (s[:, :, :, None, :] * eye[None, :, None, :, None]).reshape(b, h * d, h * e)


def _block_diag_extract(s, h):
    b, n, _ = s.shape
    d = n // h
    s5 = s.reshape(b, h, d, h, d)
    return jnp.stack([s5[:, i, :, i, :] for i in range(h)], axis=1)


def _rope_tables(pos, half, lanes, lo):
    inv = ROPE_THETA ** (-jnp.arange(half, dtype=F32) / half)
    ang = pos[:, None] * inv[None, :]
    cos, sin = jnp.cos(ang), jnp.sin(ang)
    n = pos.shape[0]
    c = jnp.ones((n, lanes), F32).at[:, lo:lo + 2 * half].set(jnp.concatenate([cos, cos], axis=1))
    sp = jnp.zeros((n, lanes), F32).at[:, lo + half:lo + 2 * half].set(sin)
    sm = jnp.zeros((n, lanes), F32).at[:, lo:lo + half].set(-sin)
    return c, sp, sm


def _prep_weights(norm_mix, w_in, lru_conv_w, lru_conv_b, lru_wa, lru_ba, lru_wx, lru_bx, lru_lambda,
                  mla_q_norm, mla_wq_b, mla_qn_norm, mla_qr_norm, mla_kv_norm, mla_kr_norm, mla_wkv_b,
                  mla_kn_norm, mlstm_b_if, out_norm, w_out, norm_ffn, w_gu, w_down):
    depth = w_in.shape[0]
    row = lambda a: a.reshape(depth, 1, -1)
    n_if = 2 * N_HEADS
    split = COL_MISC + ROPE_DIM
    w_in_pad = jnp.concatenate(
        [w_in[:, :, :split], w_in[:, :, -n_if:],
         jnp.zeros((depth, D_MODEL, COL_R_Q - split - n_if), w_in.dtype), w_in[:, :, split:-n_if]],
        axis=2).astype(BF16)

    def bd(w):
        eye = jnp.eye(N_HEADS, dtype=w.dtype)
        return (w[:, :, :, None, :] * eye[None, :, None, :, None]).reshape(depth, GROUP, GROUP)

    lru_wg = jnp.concatenate([bd(lru_wa), bd(lru_wx)], axis=2).astype(BF16)
    lru_bg = jnp.concatenate([lru_ba, lru_bx], axis=1)

    wq = mla_wq_b.reshape(depth, GROUP, N_HEADS, HEAD_DIM + ROPE_DIM)
    wq = jnp.pad(wq, ((0, 0), (0, 0), (0, 0), (0, 128 - HEAD_DIM - ROPE_DIM))).reshape(depth, GROUP, 512)
    gq = jnp.concatenate([mla_qn_norm, mla_qr_norm, jnp.zeros((depth, 32), F32)], axis=1)
    gq = jnp.tile(gq, (1, N_HEADS))
    wkv = mla_wkv_b.reshape(depth, KV_RANK, N_HEADS, 2 * HEAD_DIM)
    pad_head = lambda w: jnp.pad(w, ((0, 0), (0, 0), (0, 0), (0, 128 - HEAD_DIM))).reshape(depth, KV_RANK, 512)
    wkv = jnp.concatenate([pad_head(wkv[..., :HEAD_DIM]), pad_head(wkv[..., HEAD_DIM:])], axis=2)
    gk = jnp.tile(jnp.concatenate([mla_kn_norm, jnp.zeros((depth, 128 - HEAD_DIM), F32)], axis=1), (1, N_HEADS))
    kr_norm = jnp.pad(mla_kr_norm, ((0, 0), (0, 128 - ROPE_DIM)))
    b_if = jnp.pad(mlstm_b_if, ((0, 0), (MISC_IF, 128 - MISC_IF - n_if)))

    return dict(
        norm_mix=row(norm_mix), w_in=w_in_pad,
        conv_w=lru_conv_w, conv_b=row(lru_conv_b), lru_wg=lru_wg, lru_bg=row(lru_bg), lru_lambda=row(lru_lambda),
        q_norm=row(mla_q_norm), wq=wq.astype(BF16), gq=row(gq), kv_norm=row(mla_kv_norm), kr_norm=row(kr_norm),
        wkv=wkv.astype(BF16), gk=row(gk), b_if=row(b_if),
        on_a=row(out_norm[:, :GROUP]), on_b=row(out_norm[:, GROUP:2 * GROUP]),
        on_c=row(out_norm[:, 2 * GROUP:3 * GROUP]), on_d=row(out_norm[:, 3 * GROUP:]),
        w_out=w_out.astype(BF16), norm_ffn=row(norm_ffn), w_gu=w_gu.astype(BF16), w_down=w_down.astype(BF16))


def _tile_rows(rows, cap):
    t = cap
    while rows % t:
        t //= 2
    return t


def kernel(x_prompt, x_sample, cache_mla_ckv, cache_mla_krope, state_lru_h, state_lru_conv, state_ret, state_mlstm_C, state_mlstm_n, state_mlstm_m, norm_mix, w_in, lru_conv_w, lru_conv_b, lru_wa, lru_ba, lru_wx, lru_bx, lru_lambda, mla_q_norm, mla_wq_b, mla_qn_norm, mla_qr_norm, mla_kv_norm, mla_kr_norm, mla_wkv_b, mla_kn_norm, mlstm_b_if, out_norm, w_out, norm_ffn, w_gu, w_down):
    bp, tp, _ = x_prompt.shape
    bs, ts, _ = x_sample.shape
    depth, _, past, _ = cache_mla_ckv.shape
    rows_p, rows_s = bp * tp, bs * ts
    rows = rows_p + rows_s
    lw = _prep_weights(norm_mix, w_in, lru_conv_w, lru_conv_b, lru_wa, lru_ba, lru_wx, lru_bx, lru_lambda,
                       mla_q_norm, mla_wq_b, mla_qn_norm, mla_qr_norm, mla_kv_norm, mla_kr_norm, mla_wkv_b,
                       mla_kn_norm, mlstm_b_if, out_norm, w_out, norm_ffn, w_gu, w_down)

    pos = jnp.concatenate([jnp.tile(jnp.arange(tp, dtype=F32), bp),
                           jnp.tile(jnp.arange(past, past + ts, dtype=F32), bs)])
    tabs_mla = _rope_tables(pos, ROPE_DIM // 2, 128, HEAD_DIM)
    c64, sp64, sm64 = _rope_tables(pos, HEAD_DIM // 2, HEAD_DIM, 0)
    tabs_ret = tuple(jnp.concatenate([t, t], axis=1) for t in (c64, sp64, sm64))

    tm = _tile_rows(rows, 512)
    tb_p, tb_s = min(tp, 256), min(ts, 256)
    lc_p, lc_s = min(tp, 128), min(ts, 128)
    tq_p, tq_s = min(tp, 256), min(ts, 256)
    tka_s = _tile_rows(past, 512)

    k_past, v_past = _kv_past_call(cache_mla_ckv.reshape(depth, bs * past, KV_RANK),
                                   cache_mla_krope.reshape(depth, bs * past, ROPE_DIM), lw,
                                   _tile_rows(bs * past, 1024))

    zeros = lambda *shape: jnp.zeros(shape, F32)
    pad_m = lambda m: jnp.pad(m, ((0, 0), (0, 128 - N_HEADS)))[:, None, :]
    x = jnp.concatenate([x_prompt.reshape(rows_p, D_MODEL), x_sample.reshape(rows_s, D_MODEL)], axis=0)
    p_acc = [[] for _ in range(8)]
    s_acc = [[] for _ in range(8)]
    for l in range(depth):
        proj = _in_proj(x, lw['norm_mix'], lw['w_in'], l, tm)

        conv_s = jnp.pad(state_lru_conv[l], ((0, 0), (8 - (CONV_W - 1), 0), (0, 0)))
        ya, h_p, conv_p = _lru_call(proj, None, zeros(bp, 8, GROUP), zeros(bp, 1, GROUP), lw, l, bp, tp, 0, tb_p)
        ya, h_s, conv_o = _lru_call(proj, ya, conv_s, state_lru_h[l][:, None, :], lw, l, bs, ts, rows_p, tb_s)

        q, k, v, ckv, krope = _mla_pre_call(proj, tabs_mla, lw, l, tm)
        yb = _attn_call(q, k, v, None, None, None, lw, l, bp, tp, 0, tq_p, tq_p)
        yb = _attn_call(q, k, v, k_past, v_past, yb, lw, l, bs, ts, rows_p, tq_s, tka_s)

        yc, s_p = _ret_call(proj, tabs_ret, None, zeros(bp, GROUP, GROUP), lw, l, bp, tp, 0, lc_p)
        yc, s_s = _ret_call(proj, tabs_ret, yc, _block_diag(state_ret[l]), lw, l, bs, ts, rows_p, lc_s)

        yd, c_p, n_p, m_p = _mlstm_call(proj, None, zeros(bp, GROUP, GROUP), zeros(bp, 1, GROUP),
                                        zeros(bp, 1, 128), lw, l, bp, tp, 0, lc_p)
        yd, c_s, n_s, m_s = _mlstm_call(proj, yd, _block_diag(state_mlstm_C[l]), state_mlstm_n[l].reshape(bs, 1, GROUP),
                                        pad_m(state_mlstm_m[l]), lw, l, bs, ts, rows_p, lc_s)

        x = _out_proj_ffn(x, (ya, yb, yc, yd), lw['w_out'], lw['norm_ffn'], lw['w_gu'], lw['w_down'], l, tm)

        outs_p = (ckv[:rows_p].reshape(bp, tp, KV_RANK), krope[:rows_p].reshape(bp, tp, ROPE_DIM),
                  h_p[:, 0], conv_p[:, 8 - (CONV_W - 1):], _block_diag_extract(s_p, N_HEADS),
                  _block_diag_extract(c_p, N_HEADS), n_p.reshape(bp, N_HEADS, HEAD_DIM), m_p[:, 0, :N_HEADS])
        outs_s = (ckv[rows_p:].reshape(bs, ts, KV_RANK), krope[rows_p:].reshape(bs, ts, ROPE_DIM),
                  h_s[:, 0], conv_o[:, 8 - (CONV_W - 1):], _block_diag_extract(s_s, N_HEADS),
                  _block_diag_extract(c_s, N_HEADS), n_s.reshape(bs, N_HEADS, HEAD_DIM), m_s[:, 0, :N_HEADS])
        for acc, o in zip(p_acc, outs_p):
            acc.append(o)
        for acc, o in zip(s_acc, outs_s):
            acc.append(o)

    yp = x[:rows_p].reshape(bp, tp, D_MODEL)
    ys = x[rows_p:].reshape(bs, ts, D_MODEL)
    return (yp, ys) + tuple(jnp.stack(a) for a in p_acc) + tuple(jnp.stack(a) for a in s_acc)
```

```python
import functools
import math

import jax
import jax.numpy as jnp
import numpy as np
from jax import lax
from jax.experimental import pallas as pl
from jax.experimental.pallas import tpu as pltpu

F32 = jnp.float32
BF16 = jnp.bfloat16

D_MODEL = 1024
CHUNK = 64
HEAD_DIM = 64
GROUP = 256
N_HEADS = 4
RMS_EPS = 1e-6
ROPE_THETA = 10000.0
CONV_W = 4
LRU_C = 8.0
KV_RANK = 128
ROPE_DIM = 32
FFN_HIDDEN = 2816
IN_PAD = 3072

COL_LRU_X, COL_LRU_G, COL_Q_LAT = 0, 256, 512
COL_KV_LAT, COL_MISC = 768, 896
COL_R_Q, COL_R_K, COL_R_V, COL_R_G = 1024, 1280, 1536, 1792
COL_M_Q, COL_M_K, COL_M_V, COL_M_O = 2048, 2304, 2560, 2816
MISC_IG = 32
MISC_FG = MISC_IG + N_HEADS
ATT_SCALE = (HEAD_DIM + ROPE_DIM) ** -0.5
RET_LOG_DECAY = tuple(math.log(1.0 - 2.0 ** (-5.0 - h)) for h in range(N_HEADS))

VMEM_LIMIT = 56 * 1024 * 1024


def _dot(a, b):
    return jnp.dot(a.astype(BF16), b.astype(BF16), preferred_element_type=F32)


def _dot_nt(a, b):
    return lax.dot_general(a.astype(BF16), b.astype(BF16), (((1,), (1,)), ((), ())),
                           preferred_element_type=F32)


def _dot_tn(a, b):
    return lax.dot_general(a.astype(BF16), b.astype(BF16), (((0,), (0,)), ((), ())),
                           preferred_element_type=F32)


def _split3(x):
    hi = x.astype(BF16)
    r1 = x - hi.astype(F32)
    mid = r1.astype(BF16)
    lo = (r1 - mid.astype(F32)).astype(BF16)
    return hi, mid, lo


def _dot_f32_lhs(x, m):
    hi, mid, _ = _split3(x)
    return jnp.dot(hi, m, preferred_element_type=F32) + jnp.dot(mid, m, preferred_element_type=F32)


def _rms_rows(x, g):
    return x * lax.rsqrt(jnp.mean(x * x, axis=-1, keepdims=True) + RMS_EPS) * g


def _lane_head(width, head_width):
    return lax.broadcasted_iota(jnp.int32, (1, width), 1) // head_width


def _same_head(n):
    r = lax.broadcasted_iota(jnp.int32, (n, n), 0) // HEAD_DIM
    c = lax.broadcasted_iota(jnp.int32, (n, n), 1) // HEAD_DIM
    return r == c


def _head_rms(y, g):
    mean_mat = jnp.where(_same_head(GROUP), 1.0 / HEAD_DIM, 0.0).astype(BF16)
    return y * lax.rsqrt(_dot_f32_lhs(y * y, mean_mat) + RMS_EPS) * g


def _shift_rows(x, d, fill):
    rows = lax.broadcasted_iota(jnp.int32, x.shape, 0)
    return jnp.where(rows >= d, pltpu.roll(x, d, 0), fill)


def _cumsum_rows(x):
    d = 1
    while d < x.shape[0]:
        x = x + _shift_rows(x, d, 0.0)
        d *= 2
    return x


def _softplus(z):
    return jnp.maximum(z, 0.0) + jnp.log1p(jnp.exp(-jnp.abs(z)))


def _sigmoid(z):
    return 1.0 / (1.0 + jnp.exp(-z))


def _gelu_tanh(z):
    return 0.5 * z * (1.0 + jnp.tanh(math.sqrt(2.0 / math.pi) * (z + 0.044715 * (z * z * z))))


def _rope_lanes(x, c, sp, sm, half):
    w = x.shape[1]
    return x * c + pltpu.roll(x, half, 1) * sp + pltpu.roll(x, w - half, 1) * sm


def _tile_lanes(t, n):
    return jnp.concatenate([t] * n, axis=1) if n > 1 else t


def _in_proj_kernel(xp_ref, xs_ref, g_ref, w_ref, o_ref, *, n_p):
    x = jnp.where(pl.program_id(0) < n_p, xp_ref[...], xs_ref[...])
    o_ref[...] = jnp.dot(_rms_rows(x, g_ref[...]).astype(BF16), w_ref[...], preferred_element_type=F32)


def _two_source(n_p, tm, width):
    return (pl.BlockSpec((tm, width), lambda i: (jnp.minimum(i, n_p - 1), 0)),
            pl.BlockSpec((tm, width), lambda i: (jnp.maximum(i - n_p, 0), 0)))


def _in_proj(x_p, x_s, lw, layer, tm):
    n_p, n_s = x_p.shape[0] // tm, x_s.shape[0] // tm
    return pl.pallas_call(
        functools.partial(_in_proj_kernel, n_p=n_p),
        grid=(n_p + n_s,),
        in_specs=[*_two_source(n_p, tm, D_MODEL),
                  pl.BlockSpec((None, 1, D_MODEL), lambda i: (layer, 0, 0)),
                  pl.BlockSpec((None, D_MODEL, IN_PAD), lambda i: (layer, 0, 0))],
        out_specs=pl.BlockSpec((tm, IN_PAD), lambda i: (i, 0)),
        out_shape=jax.ShapeDtypeStruct((x_p.shape[0] + x_s.shape[0], IN_PAD), F32),
        compiler_params=pltpu.CompilerParams(dimension_semantics=("arbitrary",),
                                             vmem_limit_bytes=VMEM_LIMIT),
        name="in_proj",
    )(x_p, x_s, lw['norm_mix'], lw['w_in'])


def _lru_kernel(x_ref, g_ref, conv0_ref, h0_ref, cw_ref, cb_ref, wg_ref, bg_ref, lam_ref, on_ref,
                y_ref, hout_ref, convout_ref, xp_scr, h_scr):
    j = pl.program_id(1)
    tb = x_ref.shape[0]

    @pl.when(j == 0)
    def _():
        xp_scr[0:8, :] = conv0_ref[...]
        h_scr[...] = h0_ref[...]

    x = x_ref[...]
    xp_scr[8:8 + tb, :] = x
    cw = cw_ref[...]
    xc = (cb_ref[...] + cw[3:4] * x + cw[2:3] * xp_scr[7:7 + tb, :]
          + cw[1:2] * xp_scr[6:6 + tb, :] + cw[0:1] * xp_scr[5:5 + tb, :])
    tail = xp_scr[tb:tb + 8, :]
    xp_scr[0:8, :] = tail
    convout_ref[...] = tail

    gates = _dot(xc, wg_ref[...]) + bg_ref[...]
    r = _sigmoid(gates[:, :GROUP])
    i = _sigmoid(gates[:, GROUP:])
    log_a = -LRU_C * r * _softplus(-lam_ref[...])
    a = jnp.exp(log_a)
    u = jnp.sqrt(-jnp.tanh(log_a) * (a * a + 1.0)) * i * xc

    d = 1
    while d < tb:
        u = a * _shift_rows(u, d, 0.0) + u
        a = a * _shift_rows(a, d, 1.0)
        d *= 2
    h = u + a * h_scr[...]
    h_last = h[tb - 1:tb, :]
    h_scr[...] = h_last
    hout_ref[...] = h_last
    y_ref[...] = (_head_rms(h, on_ref[...]) * _gelu_tanh(g_ref[...])).astype(BF16)


def _mla_seg_matrix():
    r = lax.broadcasted_iota(jnp.int32, (128, 128), 0)
    c = lax.broadcasted_iota(jnp.int32, (128, 128), 1)
    nope = (r < HEAD_DIM) & (c < HEAD_DIM)
    rope = (r >= HEAD_DIM) & (r < HEAD_DIM + ROPE_DIM) & (c >= HEAD_DIM) & (c < HEAD_DIM + ROPE_DIM)
    return jnp.where(nope, 1.0 / HEAD_DIM, jnp.where(rope, 1.0 / ROPE_DIM, 0.0)).astype(BF16)


def _mla_seg_rms(x, g, seg):
    ms = jnp.concatenate([_dot_f32_lhs(jnp.square(x[:, 128 * h:128 * h + 128]), seg) for h in range(N_HEADS)],
                         axis=1)
    return x * lax.rsqrt(ms + RMS_EPS) * g


def _with_ones_lane(v):
    lane = lax.broadcasted_iota(jnp.int32, v.shape, 1) % 128
    return jnp.where(lane == HEAD_DIM, 1.0, v)


def _mla_pre_kernel(qlat_ref, kvlat_ref, misc_ref, c_ref, sp_ref, sm_ref,
                    qn_ref, wq_ref, gq_ref, kvn_ref, krn_ref, wkv_ref, gk_ref,
                    q_out, k_out, v_out, ckv_p, ckv_s, kr_p, kr_s, *, n_p):
    seg = _mla_seg_matrix()
    c, sp, sm = c_ref[...], sp_ref[...], sm_ref[...]
    half = ROPE_DIM // 2

    qraw = _dot(_rms_rows(qlat_ref[...], qn_ref[...]), wq_ref[...])
    qh = _mla_seg_rms(qraw, gq_ref[...], seg)
    q = _rope_lanes(qh, _tile_lanes(c, 4), _tile_lanes(sp, 4), _tile_lanes(sm, 4), half)
    q_out[...] = q.astype(BF16)

    ckv = _rms_rows(kvlat_ref[...], kvn_ref[...])
    misc = misc_ref[...]
    lane = lax.broadcasted_iota(jnp.int32, misc.shape, 1)
    kr = jnp.where(lane < ROPE_DIM, misc, 0.0)
    kr = kr * lax.rsqrt(jnp.sum(kr * kr, axis=-1, keepdims=True) * (1.0 / ROPE_DIM) + RMS_EPS) * krn_ref[...]
    kr = _rope_lanes(pltpu.roll(kr, HEAD_DIM, 1), c, sp, sm, half)
    kr_new = pltpu.roll(kr, 128 - HEAD_DIM, 1)[:, :ROPE_DIM]
    is_prompt = pl.program_id(0) < n_p

    @pl.when(is_prompt)
    def _():
        ckv_p[...] = ckv
        kr_p[...] = kr_new

    @pl.when(jnp.logical_not(is_prompt))
    def _():
        ckv_s[...] = ckv
        kr_s[...] = kr_new

    kv = _dot(ckv, wkv_ref[...])
    kn = _mla_seg_rms(kv[:, :512], gk_ref[...], seg)
    k_out[...] = (kn + _tile_lanes(kr, 4)).astype(BF16)
    v_out[...] = _with_ones_lane(kv[:, 512:]).astype(BF16)


def _kv_past_kernel(ckv_ref, kr_ref, wkv_ref, gk_ref, k_out, v_out):
    kv = _dot(ckv_ref[...], wkv_ref[...])
    kn = _mla_seg_rms(kv[:, :512], gk_ref[...], _mla_seg_matrix())
    r = lax.broadcasted_iota(jnp.int32, (ROPE_DIM, 128), 0)
    cidx = lax.broadcasted_iota(jnp.int32, (ROPE_DIM, 128), 1)
    place = jnp.where(cidx == r + HEAD_DIM, 1.0, 0.0).astype(BF16)
    kr = jnp.dot(kr_ref[...].astype(BF16), place, preferred_element_type=F32)
    k_out[...] = (kn + _tile_lanes(kr, 4)).astype(BF16)
    v_out[...] = _with_ones_lane(kv[:, 512:]).astype(BF16)


def _attn_kernel(q_ref, ka_ref, va_ref, kb_ref, vb_ref, on_ref, y_ref, m_scr, acc_scr, *, tka, n_past_static):
    j = pl.program_id(1)
    tq = q_ref.shape[0]
    n_past = j * (tq // tka) if n_past_static is None else n_past_static
    key_c = lax.broadcasted_iota(jnp.int32, (tq, tq), 0) // CHUNK
    qry_c = lax.broadcasted_iota(jnp.int32, (tq, tq), 1) // CHUNK
    visible = key_c <= qry_c
    heads = [slice(128 * h, 128 * h + 128) for h in range(N_HEADS)]
    c = ATT_SCALE * math.log2(math.e)

    s = [jnp.where(visible, _dot_nt(kb_ref[:, hs], q_ref[:, hs]) * c, -jnp.inf) for hs in heads]
    m = [jnp.max(s[h], axis=0, keepdims=True) for h in range(N_HEADS)]
    p = [jnp.exp2(s[h] - m[h]).astype(BF16) for h in range(N_HEADS)]
    pv = [_dot_tn(vb_ref[:, hs], p[h]) for h, hs in enumerate(heads)]
    for h in range(N_HEADS):
        m_scr[h] = m[h]
        acc_scr[h] = pv[h]

    def body(t, carry):
        off = pl.multiple_of(t * tka, tka)
        m_old = [m_scr[h] for h in range(N_HEADS)]
        s = [_dot_nt(ka_ref[pl.ds(off, tka), hs], q_ref[:, hs]) * c for hs in heads]
        m_new = [jnp.maximum(m_old[h], jnp.max(s[h], axis=0, keepdims=True)) for h in range(N_HEADS)]
        p = [jnp.exp2(s[h] - m_new[h]).astype(BF16) for h in range(N_HEADS)]
        pv = [_dot_tn(va_ref[pl.ds(off, tka), hs], p[h]) for h, hs in enumerate(heads)]
        for h in range(N_HEADS):
            acc_scr[h] = jnp.exp2(m_old[h] - m_new[h]) * acc_scr[h] + pv[h]
            m_scr[h] = m_new[h]
        return carry

    lax.fori_loop(0, n_past, body, 0)
    outs = []
    for h in range(N_HEADS):
        a = acc_scr[h]
        outs.append(a[:HEAD_DIM, :] / a[HEAD_DIM:HEAD_DIM + 1, :])
    o = jnp.concatenate(outs, axis=0).T
    y_ref[...] = _head_rms(o, on_ref[...]).astype(BF16)


def _ret_kernel(q_ref, k_ref, v_ref, g_ref, c_ref, sp_ref, sm_ref, s0_ref, on_ref,
                y_ref, sout_ref, s_scr, dmat_scr, cross_scr, tail_scr):
    b, j = pl.program_id(0), pl.program_id(1)
    lc = q_ref.shape[0]
    lane_head = _lane_head(GROUP, HEAD_DIM)
    lg_lane = jnp.zeros((1, GROUP), F32)
    for h in range(N_HEADS):
        lg_lane = jnp.where(lane_head == h, RET_LOG_DECAY[h], lg_lane)

    @pl.when((b == 0) & (j == 0))
    def _():
        t_col = lax.broadcasted_iota(jnp.int32, (lc, 1), 0).astype(F32)
        cross_scr[...] = jnp.exp((t_col + 1.0) * lg_lane)
        tail_scr[...] = jnp.exp((lc - 1.0 - t_col) * lg_lane)
        ti = lax.broadcasted_iota(jnp.int32, (lc, lc), 0)
        si = lax.broadcasted_iota(jnp.int32, (lc, lc), 1)
        causal = ti >= si
        diff = jnp.where(causal, ti - si, 0).astype(F32)
        for h in range(N_HEADS):
            dmat_scr[h] = jnp.where(causal, jnp.exp(diff * RET_LOG_DECAY[h]), 0.0)

    @pl.when(j == 0)
    def _():
        s_scr[...] = s0_ref[...]

    c, sp, sm = (_tile_lanes(t[...], 2) for t in (c_ref, sp_ref, sm_ref))
    half = HEAD_DIM // 2
    q = _rope_lanes(q_ref[...], c, sp, sm, half)
    k = _rope_lanes(k_ref[...], c, sp, sm, half) * (HEAD_DIM ** -0.5)
    v = v_ref[...].astype(BF16)
    kb = k.astype(BF16)

    s_old = s_scr[...]
    y = _dot(q, s_old) * cross_scr[...]
    hs = range(N_HEADS)
    att = [_dot_nt(jnp.where(lane_head == h, q, 0.0), kb) for h in hs]
    att = [(att[h] * dmat_scr[h]).astype(BF16) for h in hs]
    yh = [_dot(att[h], v) for h in hs]
    for h in hs:
        y = y + jnp.where(lane_head == h, yh[h], 0.0)

    s_new = s_old * jnp.exp(float(lc) * lg_lane) + jnp.where(_same_head(GROUP), _dot_tn(k * tail_scr[...], v), 0.0)
    s_scr[...] = s_new
    sout_ref[...] = s_new
    y_ref[...] = (_head_rms(y, on_ref[...]) * (g_ref[...] * _sigmoid(g_ref[...]))).astype(BF16)


def _mlstm_kernel(q_ref, k_ref, v_ref, o_ref, misc_ref, bif_ref, c0_ref, n0_ref, m0_ref, on_ref,
                  y_ref, cout_ref, nout_ref, mout_ref, c_scr, n_scr, m_scr):
    j = pl.program_id(1)
    lc = q_ref.shape[0]

    @pl.when(j == 0)
    def _():
        c_scr[...] = c0_ref[...]
        n_scr[...] = n0_ref[...]
        m_scr[...] = m0_ref[...]

    q = q_ref[...]
    kb = (k_ref[...] * (HEAD_DIM ** -0.5)).astype(BF16)
    v = v_ref[...]
    vb = v.astype(BF16)
    gates = misc_ref[...] + bif_ref[...]
    bh_all = _cumsum_rows(-_softplus(-gates))
    src_all = gates - pltpu.roll(bh_all, 128 - N_HEADS, 1)
    gates_t, bh_t = gates.T, bh_all.T
    c_old, n_old, m_old = c_scr[...], n_scr[...], m_scr[...]

    hs = range(N_HEADS)
    lane_head = _lane_head(GROUP, HEAD_DIM)
    lane128 = lax.broadcasted_iota(jnp.int32, (lc, 128), 1)
    si = lax.broadcasted_iota(jnp.int32, (lc, lc), 0)
    ti = lax.broadcasted_iota(jnp.int32, (lc, lc), 1)
    causal = si <= ti
    src3 = [t.astype(F32) for t in _split3(src_all)]
    bh3 = [t.astype(F32) for t in _split3(bh_all)]

    def decay_logits(h):
        a, b = lane128 == MISC_IG + h, lane128 == MISC_FG + h
        lhs = jnp.concatenate([jnp.where(a, t, jnp.where(b, 1.0, 0.0)) for t in src3], axis=1)
        rhs = jnp.concatenate([jnp.where(a, 1.0, jnp.where(b, t, 0.0)) for t in bh3], axis=1)
        return _dot_nt(lhs, rhs)

    ig = [gates_t[MISC_IG + h:MISC_IG + h + 1, :] for h in hs]
    bh = [bh_t[MISC_FG + h:MISC_FG + h + 1, :] for h in hs]
    m_prev = [m_old[:, h:h + 1] for h in hs]
    dm = [jnp.where(causal, decay_logits(h), -jnp.inf) for h in hs]
    kq = [_dot_nt(kb, jnp.where(lane_head == h, q, 0.0)) for h in hs]
    m_state = [bh[h] + m_prev[h] for h in hs]
    m_t = [jnp.maximum(m_state[h], jnp.max(dm[h], axis=0, keepdims=True)) for h in hs]
    sc = [kq[h] * jnp.exp(dm[h] - m_t[h]) for h in hs]
    g = [jnp.exp(m_state[h] - m_t[h]) for h in hs]
    row8 = lax.broadcasted_iota(jnp.int32, (8, GROUP), 0)
    qn = _dot_nt(jnp.where(row8 == lane_head, n_old, 0.0), q)
    den = [jnp.sum(sc[h], axis=0, keepdims=True) + g[h] * qn[h:h + 1, :] for h in hs]
    q_c = _dot_nt(c_old, q)
    num = [_dot_tn(vb, sc[h]) for h in hs]
    parts = []
    for h in hs:
        rows = slice(HEAD_DIM * h, HEAD_DIM * (h + 1))
        parts.append((num[h][rows, :] + g[h] * q_c[rows, :])
                     / jnp.maximum(jnp.abs(den[h]), jnp.exp(-m_t[h])))
    hh = jnp.concatenate(parts, axis=0).T
    y_ref[...] = (_head_rms(hh, on_ref[...]) * _sigmoid(o_ref[...])).astype(BF16)

    m_new = [m_t[h][:, lc - 1:lc] for h in hs]
    bh_last = [bh[h][:, lc - 1:lc] for h in hs]
    ws = [jnp.exp(bh_last[h] - bh[h] + ig[h] - m_new[h]) for h in hs]
    gl_row = jnp.zeros((1, GROUP), F32)
    m_new_row = m_old
    ws8 = jnp.zeros((8, lc), F32)
    row8s = lax.broadcasted_iota(jnp.int32, (8, lc), 0)
    for h in hs:
        gl_row = jnp.where(lane_head == h, jnp.exp(bh_last[h] + m_prev[h] - m_new[h]), gl_row)
        m_new_row = jnp.where(lane128[0:1, :] == h, m_new[h], m_new_row)
        ws8 = jnp.where(row8s == h, ws[h], ws8)
    n_mat = _dot(ws8, kb)
    n_upd = jnp.zeros((1, GROUP), F32)
    for h in hs:
        n_upd = jnp.where(lane_head == h, n_mat[h:h + 1, :], n_upd)
    w_rows = jnp.concatenate([jnp.broadcast_to(ws[h], (HEAD_DIM, lc)) for h in hs], axis=0)
    c_new = c_old * gl_row + jnp.where(_same_head(GROUP), _dot(v.T * w_rows, kb), 0.0)
    n_new = gl_row * n_old + n_upd
    c_scr[...] = c_new
    n_scr[...] = n_new
    m_scr[...] = m_new_row
    cout_ref[...] = c_new
    nout_ref[...] = n_new
    mout_ref[...] = m_new_row


FFN_CHUNKS = ((0, 1024), (1024, 2048), (2048, FFN_HIDDEN))


def _out_kernel(xp_ref, xs_ref, *refs, n_p):
    yp_refs, ys_refs = refs[0:4], refs[4:8]
    wo_ref, nf_ref, wgu_ref, wd_ref, op_ref, os_ref = refs[8:]
    is_prompt = pl.program_id(0) < n_p
    y = jnp.concatenate([jnp.where(is_prompt, a[...], b[...]) for a, b in zip(yp_refs, ys_refs)], axis=1)
    x1 = jnp.where(is_prompt, xp_ref[...], xs_ref[...]) + jnp.dot(y, wo_ref[...], preferred_element_type=F32)
    h = _rms_rows(x1, nf_ref[...]).astype(BF16)
    ffn = None
    for lo, hi in FFN_CHUNKS:
        g = jnp.dot(h, wgu_ref[:, lo:hi], preferred_element_type=F32)
        u = jnp.dot(h, wgu_ref[:, FFN_HIDDEN + lo:FFN_HIDDEN + hi], preferred_element_type=F32)
        a = (g * _sigmoid(g) * u).astype(BF16)
        d = jnp.dot(a, wd_ref[lo:hi, :], preferred_element_type=F32)
        ffn = d if ffn is None else ffn + d
    out = x1 + ffn

    @pl.when(is_prompt)
    def _():
        op_ref[...] = out

    @pl.when(jnp.logical_not(is_prompt))
    def _():
        os_ref[...] = out


def _out_proj_ffn(x_p, x_s, ys_p, ys_s, lw, layer, tm):
    n_p, n_s = x_p.shape[0] // tm, x_s.shape[0] // tm
    yp_spec, ys_spec = _two_source(n_p, tm, GROUP)
    xp_spec, xs_spec = _two_source(n_p, tm, D_MODEL)
    return pl.pallas_call(
        functools.partial(_out_kernel, n_p=n_p),
        grid=(n_p + n_s,),
        in_specs=[xp_spec, xs_spec, *[yp_spec] * 4, *[ys_spec] * 4,
                  pl.BlockSpec((None, D_MODEL, D_MODEL), lambda i: (layer, 0, 0)),
                  pl.BlockSpec((None, 1, D_MODEL), lambda i: (layer, 0, 0)),
                  pl.BlockSpec((None, D_MODEL, 2 * FFN_HIDDEN), lambda i: (layer, 0, 0)),
                  pl.BlockSpec((None, FFN_HIDDEN, D_MODEL), lambda i: (layer, 0, 0))],
        out_specs=[xp_spec, xs_spec],
        out_shape=[jax.ShapeDtypeStruct(x_p.shape, F32), jax.ShapeDtypeStruct(x_s.shape, F32)],
        compiler_params=pltpu.CompilerParams(dimension_semantics=("arbitrary",),
                                             vmem_limit_bytes=VMEM_LIMIT),
        name="out_proj_ffn",
    )(x_p, x_s, *ys_p, *ys_s, lw['w_out'], lw['norm_ffn'], lw['w_gu'], lw['w_down'])


def _seq_params():
    return pltpu.CompilerParams(dimension_semantics=("arbitrary", "arbitrary"),
                                vmem_limit_bytes=VMEM_LIMIT)


def _row_block(row0, seq, tb):
    base, per_seq = row0 // tb, seq // tb
    return lambda col: (lambda b, j: (base + b * per_seq + j, col))


def _layer_spec(layer, shape):
    return pl.BlockSpec((None,) + shape, lambda b, j: (layer,) + (0,) * len(shape))


def _state_spec(state_layer, shape):
    return pl.BlockSpec((None, None) + shape, lambda b, j: (state_layer, b) + (0,) * len(shape))


def _lru_call(proj, conv0, h0, state_layer, lw, layer, nseq, seq, row0, tb):
    rb, ob = _row_block(row0, seq, tb), _row_block(0, seq, tb)
    wl = functools.partial(_layer_spec, layer)
    st = functools.partial(_state_spec, state_layer)
    out_state = lambda shape: pl.BlockSpec((None,) + shape, lambda b, j: (b,) + (0,) * len(shape))
    return pl.pallas_call(
        _lru_kernel,
        grid=(nseq, seq // tb),
        in_specs=[pl.BlockSpec((tb, GROUP), rb(COL_LRU_X // GROUP)),
                  pl.BlockSpec((tb, GROUP), rb(COL_LRU_G // GROUP)),
                  st((8, GROUP)), st((1, GROUP)),
                  wl((CONV_W, GROUP)), wl((1, GROUP)), wl((GROUP, 2 * GROUP)), wl((1, 2 * GROUP)),
                  wl((1, GROUP)), wl((1, GROUP))],
        out_specs=[pl.BlockSpec((tb, GROUP), ob(0)), out_state((1, GROUP)), out_state((8, GROUP))],
        out_shape=[jax.ShapeDtypeStruct((nseq * seq, GROUP), BF16),
                   jax.ShapeDtypeStruct((nseq, 1, GROUP), F32),
                   jax.ShapeDtypeStruct((nseq, 8, GROUP), F32)],
        scratch_shapes=[pltpu.VMEM((tb + 8, GROUP), F32), pltpu.VMEM((1, GROUP), F32)],
        compiler_params=_seq_params(),
        name="rglru",
    )(proj, proj, conv0, h0, lw['conv_w'], lw['conv_b'], lw['lru_wg'], lw['lru_bg'], lw['lru_lambda'], lw['on_a'])


def _mla_pre_call(proj, tabs, lw, layer, tm, rows_p, seq_p):
    rows = proj.shape[0]
    n_p = rows_p // tm
    pos_blocks = seq_p // tm
    wl = lambda shape: pl.BlockSpec((None,) + shape, lambda i: (layer,) + (0,) * len(shape))
    tab = pl.BlockSpec((tm, 128), lambda i: (jnp.where(i < n_p, i % pos_blocks, pos_blocks + i - n_p), 0))
    ckv_p, ckv_s = _two_source(n_p, tm, KV_RANK)
    kr_p, kr_s = _two_source(n_p, tm, ROPE_DIM)
    return pl.pallas_call(
        functools.partial(_mla_pre_kernel, n_p=n_p),
        grid=(rows // tm,),
        in_specs=[pl.BlockSpec((tm, GROUP), lambda i: (i, COL_Q_LAT // GROUP)),
                  pl.BlockSpec((tm, 128), lambda i: (i, COL_KV_LAT // 128)),
                  pl.BlockSpec((tm, 128), lambda i: (i, COL_MISC // 128)),
                  tab, tab, tab,
                  wl((1, GROUP)), wl((GROUP, 512)), wl((1, 512)), wl((1, KV_RANK)), wl((1, 128)),
                  wl((KV_RANK, 1024)), wl((1, 512))],
        out_specs=[pl.BlockSpec((tm, 512), lambda i: (i, 0))] * 3 + [ckv_p, ckv_s, kr_p, kr_s],
        out_shape=[jax.ShapeDtypeStruct((rows, 512), BF16)] * 3
                  + [jax.ShapeDtypeStruct((rows_p, KV_RANK), F32), jax.ShapeDtypeStruct((rows - rows_p, KV_RANK), F32),
                     jax.ShapeDtypeStruct((rows_p, ROPE_DIM), F32), jax.ShapeDtypeStruct((rows - rows_p, ROPE_DIM), F32)],
        compiler_params=pltpu.CompilerParams(dimension_semantics=("arbitrary",),
                                             vmem_limit_bytes=VMEM_LIMIT),
        name="mla_pre",
    )(proj, proj, proj, *tabs, lw['q_norm'], lw['wq'], lw['gq'], lw['kv_norm'], lw['kr_norm'],
      lw['wkv'], lw['gk'])


def _kv_past_call(ckv, krope, lw, tr):
    depth, rows, _ = ckv.shape
    wl = lambda shape: pl.BlockSpec((None,) + shape, lambda l, i: (l,) + (0,) * len(shape))
    return pl.pallas_call(
        _kv_past_kernel,
        grid=(depth, rows // tr),
        in_specs=[pl.BlockSpec((None, tr, KV_RANK), lambda l, i: (l, i, 0)),
                  pl.BlockSpec((None, tr, ROPE_DIM), lambda l, i: (l, i, 0)),
                  wl((KV_RANK, 1024)), wl((1, 512))],
        out_specs=[pl.BlockSpec((None, tr, 512), lambda l, i: (l, i, 0))] * 2,
        out_shape=[jax.ShapeDtypeStruct((depth, rows, 512), BF16)] * 2,
        compiler_params=_seq_params(),
        name="kv_past",
    )(ckv, krope, lw['wkv'], lw['gk'])


def _attn_call(q, k, v, k_past, v_past, lw, layer, nseq, seq, row0, tq, tka):
    rb, ob = _row_block(row0, seq, tq), _row_block(0, seq, tq)
    qspec = pl.BlockSpec((tq, 512), rb(0))
    if k_past is None:
        assert row0 == 0
        ka, va = k, v
        past_spec = pl.BlockSpec((seq, 512), lambda b, j: (b, 0))
        n_past_static = None
    else:
        ka, va = k_past, v_past
        past_len = k_past.shape[1] // nseq
        past_spec = pl.BlockSpec((None, past_len, 512), lambda b, j: (layer, b, 0))
        n_past_static = past_len // tka
    return pl.pallas_call(
        functools.partial(_attn_kernel, tka=tka, n_past_static=n_past_static),
        grid=(nseq, seq // tq),
        in_specs=[qspec, past_spec, past_spec, qspec, qspec, _layer_spec(layer, (1, GROUP))],
        out_specs=pl.BlockSpec((tq, GROUP), ob(0)),
        out_shape=jax.ShapeDtypeStruct((nseq * seq, GROUP), BF16),
        scratch_shapes=[pltpu.VMEM((N_HEADS, 1, tq), F32), pltpu.VMEM((N_HEADS, 128, tq), F32)],
        compiler_params=_seq_params(),
        name="mla_attn",
    )(q, ka, va, k, v, lw['on_b'])


def _ret_call(proj, tabs, tab_map, s0, state_layer, lw, layer, nseq, seq, row0, lc):
    rb, ob = _row_block(row0, seq, lc), _row_block(0, seq, lc)
    blk = lambda col: pl.BlockSpec((lc, GROUP), rb(col // GROUP))
    tab = pl.BlockSpec((lc, 128), tab_map)
    return pl.pallas_call(
        _ret_kernel,
        grid=(nseq, seq // lc),
        in_specs=[blk(COL_R_Q), blk(COL_R_K), blk(COL_R_V), blk(COL_R_G), tab, tab, tab,
                  _state_spec(state_layer, (GROUP, GROUP)), _layer_spec(layer, (1, GROUP))],
        out_specs=[pl.BlockSpec((lc, GROUP), ob(0)),
                   pl.BlockSpec((None, GROUP, GROUP), lambda b, j: (b, 0, 0))],
        out_shape=[jax.ShapeDtypeStruct((nseq * seq, GROUP), BF16),
                   jax.ShapeDtypeStruct((nseq, GROUP, GROUP), F32)],
        scratch_shapes=[pltpu.VMEM((GROUP, GROUP), F32), pltpu.VMEM((N_HEADS, lc, lc), F32),
                        pltpu.VMEM((lc, GROUP), F32), pltpu.VMEM((lc, GROUP), F32)],
        compiler_params=_seq_params(),
        name="retention",
    )(proj, proj, proj, proj, *tabs, s0, lw['on_c'])


def _mlstm_call(proj, c0, n0, m0, state_layer, lw, layer, nseq, seq, row0, lc):
    rb, ob = _row_block(row0, seq, lc), _row_block(0, seq, lc)
    blk = lambda col: pl.BlockSpec((lc, GROUP), rb(col // GROUP))
    st = functools.partial(_state_spec, state_layer)
    out_state = lambda shape: pl.BlockSpec((None,) + shape, lambda b, j: (b,) + (0,) * len(shape))
    return pl.pallas_call(
        _mlstm_kernel,
        grid=(nseq, seq // lc),
        in_specs=[blk(COL_M_Q), blk(COL_M_K), blk(COL_M_V), blk(COL_M_O),
                  pl.BlockSpec((lc, 128), rb(COL_MISC // 128)), _layer_spec(layer, (1, 128)),
                  st((GROUP, GROUP)), st((1, GROUP)), st((1, 128)), _layer_spec(layer, (1, GROUP))],
        out_specs=[pl.BlockSpec((lc, GROUP), ob(0)), out_state((GROUP, GROUP)), out_state((1, GROUP)),
                   out_state((1, 128))],
        out_shape=[jax.ShapeDtypeStruct((nseq * seq, GROUP), BF16),
                   jax.ShapeDtypeStruct((nseq, GROUP, GROUP), F32),
                   jax.ShapeDtypeStruct((nseq, 1, GROUP), F32),
                   jax.ShapeDtypeStruct((nseq, 1, 128), F32)],
        scratch_shapes=[pltpu.VMEM((GROUP, GROUP), F32), pltpu.VMEM((1, GROUP), F32),
                        pltpu.VMEM((1, 128), F32)],
        compiler_params=_seq_params(),
        name="mlstm",
    )(proj, proj, proj, proj, proj, lw['b_if'], c0, n0, m0, lw['on_d'])


def _block_diag(s):
    h, d, e = s.shape[-3:]
    eye = jnp.eye(h, dtype=s.dtype)
    return (s[..., :, :, None, :] * eye[:, None, :, None]).reshape(s.shape[:-3] + (h * d, h * e))


def _block_diag_extract(s, h):
    d = s.shape[-1] // h
    s5 = s.reshape(s.shape[:-2] + (h, d, h, d))
    return jnp.stack([s5[..., i, :, i, :] for i in range(h)], axis=-3)


def _rope_tables(pos, half, lanes, lo):
    inv = ROPE_THETA ** (-jnp.arange(half, dtype=F32) / half)
    ang = pos[:, None] * inv[None, :]
    cos, sin = jnp.cos(ang), jnp.sin(ang)
    n = pos.shape[0]
    c = jnp.ones((n, lanes), F32).at[:, lo:lo + 2 * half].set(jnp.concatenate([cos, cos], axis=1))
    sp = jnp.zeros((n, lanes), F32).at[:, lo + half:lo + 2 * half].set(sin)
    sm = jnp.zeros((n, lanes), F32).at[:, lo:lo + half].set(-sin)
    return c, sp, sm


def _prep_weights(norm_mix, w_in, lru_conv_w, lru_conv_b, lru_wa, lru_ba, lru_wx, lru_bx, lru_lambda,
                  mla_q_norm, mla_wq_b, mla_qn_norm, mla_qr_norm, mla_kv_norm, mla_kr_norm, mla_wkv_b,
                  mla_kn_norm, mlstm_b_if, out_norm, w_out, norm_ffn, w_gu, w_down):
    depth = w_in.shape[0]
    row = lambda a: a.reshape(depth, 1, -1)
    n_if = 2 * N_HEADS
    split = COL_MISC + ROPE_DIM
    w_in = w_in.astype(BF16)
    w_in_pad = jnp.concatenate(
        [w_in[:, :, :split], w_in[:, :, -n_if:],
         jnp.zeros((depth, D_MODEL, COL_R_Q - split - n_if), BF16), w_in[:, :, split:-n_if]], axis=2)

    lru_wg = jnp.concatenate([_block_diag(lru_wa), _block_diag(lru_wx)], axis=2).astype(BF16)
    lru_bg = jnp.concatenate([lru_ba, lru_bx], axis=1)

    wq = mla_wq_b.reshape(depth, GROUP, N_HEADS, HEAD_DIM + ROPE_DIM)
    wq = jnp.pad(wq, ((0, 0), (0, 0), (0, 0), (0, 128 - HEAD_DIM - ROPE_DIM))).reshape(depth, GROUP, 512)
    gq = jnp.concatenate([mla_qn_norm, mla_qr_norm, jnp.zeros((depth, 32), F32)], axis=1)
    gq = jnp.tile(gq, (1, N_HEADS))
    wkv = mla_wkv_b.reshape(depth, KV_RANK, N_HEADS, 2 * HEAD_DIM)
    pad_head = lambda w: jnp.pad(w, ((0, 0), (0, 0), (0, 0), (0, 128 - HEAD_DIM))).reshape(depth, KV_RANK, 512)
    wkv = jnp.concatenate([pad_head(wkv[..., :HEAD_DIM]), pad_head(wkv[..., HEAD_DIM:])], axis=2)
    gk = jnp.tile(jnp.concatenate([mla_kn_norm, jnp.zeros((depth, 128 - HEAD_DIM), F32)], axis=1), (1, N_HEADS))
    kr_norm = jnp.pad(mla_kr_norm, ((0, 0), (0, 128 - ROPE_DIM)))
    b_if = jnp.pad(mlstm_b_if, ((0, 0), (MISC_IG, 128 - MISC_IG - n_if)))

    return dict(
        norm_mix=row(norm_mix), w_in=w_in_pad,
        conv_w=lru_conv_w, conv_b=row(lru_conv_b), lru_wg=lru_wg, lru_bg=row(lru_bg), lru_lambda=row(lru_lambda),
        q_norm=row(mla_q_norm), wq=wq.astype(BF16), gq=row(gq), kv_norm=row(mla_kv_norm), kr_norm=row(kr_norm),
        wkv=wkv.astype(BF16), gk=row(gk), b_if=row(b_if),
        on_a=row(out_norm[:, :GROUP]), on_b=row(out_norm[:, GROUP:2 * GROUP]),
        on_c=row(out_norm[:, 2 * GROUP:3 * GROUP]), on_d=row(out_norm[:, 3 * GROUP:]),
        w_out=w_out.astype(BF16), norm_ffn=row(norm_ffn), w_gu=w_gu.astype(BF16), w_down=w_down.astype(BF16))


def _tile_rows(rows, cap):
    t = cap
    while rows % t:
        t //= 2
    return t


def kernel(x_prompt, x_sample, cache_mla_ckv, cache_mla_krope, state_lru_h, state_lru_conv, state_ret, state_mlstm_C, state_mlstm_n, state_mlstm_m, norm_mix, w_in, lru_conv_w, lru_conv_b, lru_wa, lru_ba, lru_wx, lru_bx, lru_lambda, mla_q_norm, mla_wq_b, mla_qn_norm, mla_qr_norm, mla_kv_norm, mla_kr_norm, mla_wkv_b, mla_kn_norm, mlstm_b_if, out_norm, w_out, norm_ffn, w_gu, w_down):
    bp, tp, _ = x_prompt.shape
    bs, ts, _ = x_sample.shape
    depth, _, past, _ = cache_mla_ckv.shape
    rows_p, rows_s = bp * tp, bs * ts
    tm = math.gcd(_tile_rows(tp, 512), rows_s)
    tb_p, tb_s = min(tp, 256), min(ts, 256)
    lc_p, lc_s = min(tp, 128), min(ts, 128)
    tq_p, tq_s = min(tp, 256), min(ts, 256)
    tka_s = _tile_rows(past, 512)
    assert past % CHUNK == 0 and ts % CHUNK == 0 and tp % tm == 0 and rows_s % tm == 0

    lw = _prep_weights(norm_mix, w_in, lru_conv_w, lru_conv_b, lru_wa, lru_ba, lru_wx, lru_bx, lru_lambda,
                       mla_q_norm, mla_wq_b, mla_qn_norm, mla_qr_norm, mla_kv_norm, mla_kr_norm, mla_wkv_b,
                       mla_kn_norm, mlstm_b_if, out_norm, w_out, norm_ffn, w_gu, w_down)

    pos = jnp.concatenate([jnp.arange(tp, dtype=F32), jnp.tile(jnp.arange(past, past + ts, dtype=F32), bs)])
    tabs_mla = _rope_tables(pos, ROPE_DIM // 2, 128, HEAD_DIM)
    tabs_ret = tuple(jnp.concatenate([t, t], axis=1) for t in _rope_tables(pos, HEAD_DIM // 2, HEAD_DIM, 0))

    k_past, v_past = _kv_past_call(cache_mla_ckv.reshape(depth, bs * past, KV_RANK),
                                   cache_mla_krope.reshape(depth, bs * past, ROPE_DIM), lw,
                                   _tile_rows(bs * past, 1024))

    zeros = lambda *shape: jnp.zeros((1, bp) + shape, F32)
    st_p = dict(conv=zeros(8, GROUP), h=zeros(1, GROUP), s=zeros(GROUP, GROUP), c=zeros(GROUP, GROUP),
                n=zeros(1, GROUP), m=zeros(1, 128))
    st_s = dict(conv=jnp.pad(state_lru_conv, ((0, 0), (0, 0), (8 - (CONV_W - 1), 0), (0, 0))),
                h=state_lru_h[:, :, None, :], s=_block_diag(state_ret), c=_block_diag(state_mlstm_C),
                n=state_mlstm_n.reshape(depth, bs, 1, GROUP),
                m=jnp.pad(state_mlstm_m, ((0, 0), (0, 0), (0, 128 - N_HEADS)))[:, :, None, :])

    x_p = x_prompt.reshape(rows_p, D_MODEL)
    x_s = x_sample.reshape(rows_s, D_MODEL)
    acc = {name: [] for name in ('ckv_p', 'ckv_s', 'kr_p', 'kr_s', 'h_p', 'h_s', 'conv_p', 'conv_s',
                                 's_p', 's_s', 'c_p', 'c_s', 'n_p', 'n_s', 'm_p', 'm_s')}
    for l in range(depth):
        proj = _in_proj(x_p, x_s, lw, l, tm)
        ya_p, h_p, conv_p = _lru_call(proj, st_p['conv'], st_p['h'], 0, lw, l, bp, tp, 0, tb_p)
        ya_s, h_s, conv_s = _lru_call(proj, st_s['conv'], st_s['h'], l, lw, l, bs, ts, rows_p, tb_s)
        q, k, v, ckv_p, ckv_s, kr_p, kr_s = _mla_pre_call(proj, tabs_mla, lw, l, tm, rows_p, tp)
        yb_p = _attn_call(q, k, v, None, None, lw, l, bp, tp, 0, tq_p, tq_p)
        yb_s = _attn_call(q, k, v, k_past, v_past, lw, l, bs, ts, rows_p, tq_s, tka_s)
        yc_p, s_p = _ret_call(proj, tabs_ret, lambda b, j: (j, 0), st_p['s'], 0, lw, l, bp, tp, 0, lc_p)
        yc_s, s_s = _ret_call(proj, tabs_ret, _row_block(tp, ts, lc_s)(0), st_s['s'], l, lw, l, bs, ts, rows_p, lc_s)
        yd_p, c_p, n_p, m_p = _mlstm_call(proj, st_p['c'], st_p['n'], st_p['m'], 0, lw, l, bp, tp, 0, lc_p)
        yd_s, c_s, n_s, m_s = _mlstm_call(proj, st_s['c'], st_s['n'], st_s['m'], l, lw, l, bs, ts, rows_p, lc_s)
        x_p, x_s = _out_proj_ffn(x_p, x_s, (ya_p, yb_p, yc_p, yd_p), (ya_s, yb_s, yc_s, yd_s), lw, l, tm)
        for name, val in (('ckv_p', ckv_p), ('ckv_s', ckv_s), ('kr_p', kr_p), ('kr_s', kr_s), ('h_p', h_p),
                          ('h_s', h_s), ('conv_p', conv_p), ('conv_s', conv_s), ('s_p', s_p), ('s_s', s_s),
                          ('c_p', c_p), ('c_s', c_s), ('n_p', n_p), ('n_s', n_s), ('m_p', m_p), ('m_s', m_s)):
            acc[name].append(val)

    st = {name: jnp.stack(vals) for name, vals in acc.items()}

    def outputs(tag, b, t):
        return (st['ckv_' + tag].reshape(depth, b, t, KV_RANK), st['kr_' + tag].reshape(depth, b, t, ROPE_DIM),
                st['h_' + tag][:, :, 0], st['conv_' + tag][:, :, 8 - (CONV_W - 1):],
                _block_diag_extract(st['s_' + tag], N_HEADS), _block_diag_extract(st['c_' + tag], N_HEADS),
                st['n_' + tag].reshape(depth, b, N_HEADS, HEAD_DIM), st['m_' + tag][:, :, 0, :N_HEADS])

    return ((x_p.reshape(bp, tp, D_MODEL), x_s.reshape(bs, ts, D_MODEL))
            + outputs('p', bp, tp) + outputs('s', bs, ts))
```

```python
import functools
import math

import jax
import jax.numpy as jnp
import numpy as np
from jax import lax
from jax.experimental import pallas as pl
from jax.experimental.pallas import tpu as pltpu

F32 = jnp.float32
BF16 = jnp.bfloat16

D_MODEL = 1024
CHUNK = 64
HEAD_DIM = 64
GROUP = 256
N_HEADS = 4
RMS_EPS = 1e-6
ROPE_THETA = 10000.0
CONV_W = 4
LRU_C = 8.0
KV_RANK = 128
ROPE_DIM = 32
FFN_HIDDEN = 2816
IN_PAD = 3072

COL_LRU_X, COL_LRU_G, COL_Q_LAT = 0, 256, 512
COL_KV_LAT, COL_MISC = 768, 896
COL_R_Q, COL_R_K, COL_R_V, COL_R_G = 1024, 1280, 1536, 1792
COL_M_Q, COL_M_K, COL_M_V, COL_M_O = 2048, 2304, 2560, 2816
MISC_IG = 32
MISC_FG = MISC_IG + N_HEADS
ATT_SCALE = (HEAD_DIM + ROPE_DIM) ** -0.5
RET_LOG_DECAY = tuple(math.log(1.0 - 2.0 ** (-5.0 - h)) for h in range(N_HEADS))

VMEM_LIMIT = 56 * 1024 * 1024


def _dot(a, b):
    return jnp.dot(a.astype(BF16), b.astype(BF16), preferred_element_type=F32)


def _dot_nt(a, b):
    return lax.dot_general(a.astype(BF16), b.astype(BF16), (((1,), (1,)), ((), ())),
                           preferred_element_type=F32)


def _dot_tn(a, b):
    return lax.dot_general(a.astype(BF16), b.astype(BF16), (((0,), (0,)), ((), ())),
                           preferred_element_type=F32)


def _split3(x):
    hi = x.astype(BF16)
    r1 = x - hi.astype(F32)
    mid = r1.astype(BF16)
    lo = (r1 - mid.astype(F32)).astype(BF16)
    return hi, mid, lo


def _dot_f32_lhs(x, m, terms=2):
    parts = [jnp.dot(t, m, preferred_element_type=F32) for t in _split3(x)[:terms]]
    return functools.reduce(lambda a, b: a + b, parts)


def _pack_heads(s):
    r = lax.broadcasted_iota(jnp.int32, (HEAD_DIM, GROUP), 0)
    c = lax.broadcasted_iota(jnp.int32, (HEAD_DIM, GROUP), 1)
    tile = jnp.where(c % HEAD_DIM == r, 1.0, 0.0).astype(BF16)
    return jnp.where(_same_head(GROUP), _dot_f32_lhs(s, tile, terms=3), 0.0)


def _unpack_heads(s):
    r = lax.broadcasted_iota(jnp.int32, (GROUP, HEAD_DIM), 0)
    c = lax.broadcasted_iota(jnp.int32, (GROUP, HEAD_DIM), 1)
    fold = jnp.where(r % HEAD_DIM == c, 1.0, 0.0).astype(BF16)
    return _dot_f32_lhs(s, fold, terms=3)


def _rms_rows(x, g):
    return x * lax.rsqrt(jnp.mean(x * x, axis=-1, keepdims=True) + RMS_EPS) * g


def _lane_head(width, head_width):
    return lax.broadcasted_iota(jnp.int32, (1, width), 1) // head_width


def _same_head(n):
    r = lax.broadcasted_iota(jnp.int32, (n, n), 0) // HEAD_DIM
    c = lax.broadcasted_iota(jnp.int32, (n, n), 1) // HEAD_DIM
    return r == c


def _head_rms(y, g):
    mean_mat = jnp.where(_same_head(GROUP), 1.0 / HEAD_DIM, 0.0).astype(BF16)
    return y * lax.rsqrt(_dot_f32_lhs(y * y, mean_mat) + RMS_EPS) * g


def _shift_rows(x, d, fill):
    rows = lax.broadcasted_iota(jnp.int32, x.shape, 0)
    return jnp.where(rows >= d, pltpu.roll(x, d, 0), fill)


def _cumsum_rows(x):
    d = 1
    while d < x.shape[0]:
        x = x + _shift_rows(x, d, 0.0)
        d *= 2
    return x


def _softplus(z):
    return jnp.maximum(z, 0.0) + jnp.log1p(jnp.exp(-jnp.abs(z)))


def _sigmoid(z):
    return 1.0 / (1.0 + jnp.exp(-z))


def _gelu_tanh(z):
    return 0.5 * z * (1.0 + jnp.tanh(math.sqrt(2.0 / math.pi) * (z + 0.044715 * (z * z * z))))


def _rope_lanes(x, c, sp, sm, half):
    w = x.shape[1]
    return x * c + pltpu.roll(x, half, 1) * sp + pltpu.roll(x, w - half, 1) * sm


def _tile_lanes(t, n):
    return jnp.concatenate([t] * n, axis=1) if n > 1 else t


def _w_in_prep_kernel(w_ref, o_ref):
    n_in = w_ref.shape[1]
    n_if = 2 * N_HEADS
    split = COL_MISC + ROPE_DIM
    o_ref[:, :split] = w_ref[:, :split].astype(BF16)
    gates = jnp.concatenate([w_ref[:, n_in - n_if:], jnp.zeros((w_ref.shape[0], COL_R_Q - split - n_if), F32)],
                            axis=1)
    o_ref[:, split:COL_R_Q] = gates.astype(BF16)
    o_ref[:, COL_R_Q:] = w_ref[:, split:n_in - n_if].astype(BF16)


def _w_in_prep(w_in, rows=256):
    depth, d_model, n_in = w_in.shape
    return pl.pallas_call(
        _w_in_prep_kernel,
        grid=(depth, d_model // rows),
        in_specs=[pl.BlockSpec((None, rows, n_in), lambda l, i: (l, i, 0))],
        out_specs=pl.BlockSpec((None, rows, IN_PAD), lambda l, i: (l, i, 0)),
        out_shape=jax.ShapeDtypeStruct((depth, d_model, IN_PAD), BF16),
        compiler_params=_seq_params(),
        name="w_in_prep",
    )(w_in)


def _in_proj_kernel(xp_ref, xs_ref, g_ref, w_ref, o_ref, *, n_p):
    x = jnp.where(pl.program_id(0) < n_p, xp_ref[...], xs_ref[...])
    o_ref[...] = jnp.dot(_rms_rows(x, g_ref[...]).astype(BF16), w_ref[...], preferred_element_type=F32)


def _two_source(n_p, tm, width):
    return (pl.BlockSpec((tm, width), lambda i: (jnp.minimum(i, n_p - 1), 0)),
            pl.BlockSpec((tm, width), lambda i: (jnp.maximum(i - n_p, 0), 0)))


def _in_proj(x_p, x_s, lw, layer, tm):
    n_p, n_s = x_p.shape[0] // tm, x_s.shape[0] // tm
    return pl.pallas_call(
        functools.partial(_in_proj_kernel, n_p=n_p),
        grid=(n_p + n_s,),
        in_specs=[*_two_source(n_p, tm, D_MODEL),
                  pl.BlockSpec((None, 1, D_MODEL), lambda i: (layer, 0, 0)),
                  pl.BlockSpec((None, D_MODEL, IN_PAD), lambda i: (layer, 0, 0))],
        out_specs=pl.BlockSpec((tm, IN_PAD), lambda i: (i, 0)),
        out_shape=jax.ShapeDtypeStruct((x_p.shape[0] + x_s.shape[0], IN_PAD), F32),
        compiler_params=pltpu.CompilerParams(dimension_semantics=("arbitrary",),
                                             vmem_limit_bytes=VMEM_LIMIT),
        name="in_proj",
    )(x_p, x_s, lw['norm_mix'], lw['w_in'])


def _lru_kernel(x_ref, g_ref, conv0_ref, h0_ref, cw_ref, cb_ref, wg_ref, bg_ref, lam_ref, on_ref,
                y_ref, hout_ref, convout_ref, xp_scr, h_scr):
    j = pl.program_id(1)
    tb = x_ref.shape[0]

    @pl.when(j == 0)
    def _():
        xp_scr[0:8, :] = conv0_ref[...]
        h_scr[...] = h0_ref[...]

    x = x_ref[...]
    xp_scr[8:8 + tb, :] = x
    cw = cw_ref[...]
    xc = (cb_ref[...] + cw[3:4] * x + cw[2:3] * xp_scr[7:7 + tb, :]
          + cw[1:2] * xp_scr[6:6 + tb, :] + cw[0:1] * xp_scr[5:5 + tb, :])
    tail = xp_scr[tb:tb + 8, :]
    xp_scr[0:8, :] = tail
    convout_ref[...] = tail

    gates = _dot(xc, wg_ref[...]) + bg_ref[...]
    r = _sigmoid(gates[:, :GROUP])
    i = _sigmoid(gates[:, GROUP:])
    log_a = -LRU_C * r * _softplus(-lam_ref[...])
    a = jnp.exp(log_a)
    u = jnp.sqrt(-jnp.tanh(log_a) * (a * a + 1.0)) * i * xc

    d = 1
    while d < tb:
        u = a * _shift_rows(u, d, 0.0) + u
        a = a * _shift_rows(a, d, 1.0)
        d *= 2
    h = u + a * h_scr[...]
    h_last = h[tb - 1:tb, :]
    h_scr[...] = h_last
    hout_ref[...] = h_last
    y_ref[...] = (_head_rms(h, on_ref[...]) * _gelu_tanh(g_ref[...])).astype(BF16)


def _mla_seg_matrix():
    r = lax.broadcasted_iota(jnp.int32, (256, 256), 0)
    c = lax.broadcasted_iota(jnp.int32, (256, 256), 1)
    same = (r // 128) == (c // 128)
    rl, cl = r % 128, c % 128
    nope = same & (rl < HEAD_DIM) & (cl < HEAD_DIM)
    rope = same & (rl >= HEAD_DIM) & (rl < HEAD_DIM + ROPE_DIM) & (cl >= HEAD_DIM) & (cl < HEAD_DIM + ROPE_DIM)
    return jnp.where(nope, 1.0 / HEAD_DIM, jnp.where(rope, 1.0 / ROPE_DIM, 0.0)).astype(BF16)


def _mla_seg_rms(x, g, seg):
    ms = jnp.concatenate([_dot_f32_lhs(jnp.square(x[:, 256 * p:256 * p + 256]), seg) for p in range(N_HEADS // 2)],
                         axis=1)
    return x * lax.rsqrt(ms + RMS_EPS) * g


def _with_ones_lane(v):
    lane = lax.broadcasted_iota(jnp.int32, v.shape, 1) % 128
    return jnp.where(lane == HEAD_DIM, 1.0, v)


def _mla_pre_kernel(qlat_ref, kvlat_ref, misc_ref, c_ref, sp_ref, sm_ref,
                    qn_ref, wq_ref, gq_ref, kvn_ref, krn_ref, wkv_ref, gk_ref,
                    q_out, k_out, v_out, ckv_p, ckv_s, kr_p, kr_s, *, n_p):
    seg = _mla_seg_matrix()
    c, sp, sm = c_ref[...], sp_ref[...], sm_ref[...]
    half = ROPE_DIM // 2

    qraw = _dot(_rms_rows(qlat_ref[...], qn_ref[...]), wq_ref[...])
    qh = _mla_seg_rms(qraw, gq_ref[...], seg)
    q = _rope_lanes(qh, _tile_lanes(c, 4), _tile_lanes(sp, 4), _tile_lanes(sm, 4), half)
    q_out[...] = q.astype(BF16)

    ckv = _rms_rows(kvlat_ref[...], kvn_ref[...])
    misc = misc_ref[...]
    lane = lax.broadcasted_iota(jnp.int32, misc.shape, 1)
    kr = jnp.where(lane < ROPE_DIM, misc, 0.0)
    kr = kr * lax.rsqrt(jnp.sum(kr * kr, axis=-1, keepdims=True) * (1.0 / ROPE_DIM) + RMS_EPS) * krn_ref[...]
    kr = _rope_lanes(pltpu.roll(kr, HEAD_DIM, 1), c, sp, sm, half)
    kr_new = pltpu.roll(kr, 128 - HEAD_DIM, 1)[:, :ROPE_DIM]
    is_prompt = pl.program_id(0) < n_p

    @pl.when(is_prompt)
    def _():
        ckv_p[...] = ckv
        kr_p[...] = kr_new

    @pl.when(jnp.logical_not(is_prompt))
    def _():
        ckv_s[...] = ckv
        kr_s[...] = kr_new

    kv = _dot(ckv, wkv_ref[...])
    kn = _mla_seg_rms(kv[:, :512], gk_ref[...], seg)
    k_out[...] = (kn + _tile_lanes(kr, 4)).astype(BF16)
    v_out[...] = _with_ones_lane(kv[:, 512:]).astype(BF16)


def _kv_past_kernel(ckv_ref, kr_ref, wkv_ref, gk_ref, k_out, v_out):
    kv = _dot(ckv_ref[...], wkv_ref[...])
    kn = _mla_seg_rms(kv[:, :512], gk_ref[...], _mla_seg_matrix())
    r = lax.broadcasted_iota(jnp.int32, (ROPE_DIM, 128), 0)
    cidx = lax.broadcasted_iota(jnp.int32, (ROPE_DIM, 128), 1)
    place = jnp.where(cidx == r + HEAD_DIM, 1.0, 0.0).astype(BF16)
    kr = jnp.dot(kr_ref[...].astype(BF16), place, preferred_element_type=F32)
    k_out[...] = (kn + _tile_lanes(kr, 4)).astype(BF16)
    v_out[...] = _with_ones_lane(kv[:, 512:]).astype(BF16)


def _attn_kernel(q_ref, ka_ref, va_ref, kb_ref, vb_ref, on_ref, y_ref, m_scr, acc_scr, *, tka, n_past_static):
    j = pl.program_id(1)
    tq = q_ref.shape[0]
    n_past = j * (tq // tka) if n_past_static is None else n_past_static
    key_c = lax.broadcasted_iota(jnp.int32, (tq, tq), 0) // CHUNK
    qry_c = lax.broadcasted_iota(jnp.int32, (tq, tq), 1) // CHUNK
    visible = key_c <= qry_c
    heads = [slice(128 * h, 128 * h + 128) for h in range(N_HEADS)]
    c = ATT_SCALE * math.log2(math.e)

    s = [jnp.where(visible, _dot_nt(kb_ref[:, hs], q_ref[:, hs]) * c, -jnp.inf) for hs in heads]
    m = [jnp.max(s[h], axis=0, keepdims=True) for h in range(N_HEADS)]
    p = [jnp.exp2(s[h] - m[h]).astype(BF16) for h in range(N_HEADS)]
    pv = [_dot_tn(vb_ref[:, hs], p[h]) for h, hs in enumerate(heads)]
    for h in range(N_HEADS):
        m_scr[h] = m[h]
        acc_scr[h] = pv[h]

    def body(t, carry):
        off = pl.multiple_of(t * tka, tka)
        m_old = [m_scr[h] for h in range(N_HEADS)]
        s = [_dot_nt(ka_ref[pl.ds(off, tka), hs], q_ref[:, hs]) * c for hs in heads]
        m_new = [jnp.maximum(m_old[h], jnp.max(s[h], axis=0, keepdims=True)) for h in range(N_HEADS)]
        p = [jnp.exp2(s[h] - m_new[h]).astype(BF16) for h in range(N_HEADS)]
        pv = [_dot_tn(va_ref[pl.ds(off, tka), hs], p[h]) for h, hs in enumerate(heads)]
        for h in range(N_HEADS):
            acc_scr[h] = jnp.exp2(m_old[h] - m_new[h]) * acc_scr[h] + pv[h]
            m_scr[h] = m_new[h]
        return carry

    lax.fori_loop(0, n_past, body, 0)
    outs = []
    for h in range(N_HEADS):
        a = acc_scr[h]
        outs.append(a[:HEAD_DIM, :] / a[HEAD_DIM:HEAD_DIM + 1, :])
    o = jnp.concatenate(outs, axis=0).T
    y_ref[...] = _head_rms(o, on_ref[...]).astype(BF16)


def _ret_kernel(q_ref, k_ref, v_ref, g_ref, c_ref, sp_ref, sm_ref, s0_ref, on_ref,
                y_ref, sout_ref, s_scr, dmat_scr, cross_scr, tail_scr):
    b, j = pl.program_id(0), pl.program_id(1)
    lc = q_ref.shape[0]
    lane_head = _lane_head(GROUP, HEAD_DIM)
    lg_lane = jnp.zeros((1, GROUP), F32)
    for h in range(N_HEADS):
        lg_lane = jnp.where(lane_head == h, RET_LOG_DECAY[h], lg_lane)

    @pl.when((b == 0) & (j == 0))
    def _():
        t_col = lax.broadcasted_iota(jnp.int32, (lc, 1), 0).astype(F32)
        cross_scr[...] = jnp.exp((t_col + 1.0) * lg_lane)
        tail_scr[...] = jnp.exp((lc - 1.0 - t_col) * lg_lane)
        ti = lax.broadcasted_iota(jnp.int32, (lc, lc), 0)
        si = lax.broadcasted_iota(jnp.int32, (lc, lc), 1)
        causal = ti >= si
        diff = jnp.where(causal, ti - si, 0).astype(F32)
        for h in range(N_HEADS):
            dmat_scr[h] = jnp.where(causal, jnp.exp(diff * RET_LOG_DECAY[h]), 0.0)

    @pl.when(j == 0)
    def _():
        s_scr[...] = _pack_heads(s0_ref[...])

    c, sp, sm = (_tile_lanes(t[...], 2) for t in (c_ref, sp_ref, sm_ref))
    half = HEAD_DIM // 2
    q = _rope_lanes(q_ref[...], c, sp, sm, half)
    k = _rope_lanes(k_ref[...], c, sp, sm, half) * (HEAD_DIM ** -0.5)
    v = v_ref[...].astype(BF16)
    kb = k.astype(BF16)

    s_old = s_scr[...]
    y = _dot(q, s_old) * cross_scr[...]
    hs = range(N_HEADS)
    att = [_dot_nt(jnp.where(lane_head == h, q, 0.0), kb) for h in hs]
    att = [(att[h] * dmat_scr[h]).astype(BF16) for h in hs]
    yh = [_dot(att[h], v) for h in hs]
    for h in hs:
        y = y + jnp.where(lane_head == h, yh[h], 0.0)

    s_new = s_old * jnp.exp(float(lc) * lg_lane) + jnp.where(_same_head(GROUP), _dot_tn(k * tail_scr[...], v), 0.0)
    s_scr[...] = s_new

    @pl.when(j == pl.num_programs(1) - 1)
    def _():
        sout_ref[...] = _unpack_heads(s_new)

    y_ref[...] = (_head_rms(y, on_ref[...]) * (g_ref[...] * _sigmoid(g_ref[...]))).astype(BF16)


def _mlstm_kernel(q_ref, k_ref, v_ref, o_ref, misc_ref, bif_ref, c0_ref, n0_ref, m0_ref, on_ref,
                  y_ref, cout_ref, nout_ref, mout_ref, c_scr, n_scr, m_scr):
    j = pl.program_id(1)
    lc = q_ref.shape[0]

    @pl.when(j == 0)
    def _():
        c_scr[...] = _pack_heads(c0_ref[...])
        n_scr[...] = n0_ref[...]
        m_scr[...] = m0_ref[...]

    q = q_ref[...]
    kb = (k_ref[...] * (HEAD_DIM ** -0.5)).astype(BF16)
    v = v_ref[...]
    vb = v.astype(BF16)
    gates = misc_ref[...] + bif_ref[...]
    bh_all = _cumsum_rows(-_softplus(-gates))
    src_all = gates - pltpu.roll(bh_all, 128 - N_HEADS, 1)
    gates_t, bh_t = gates.T, bh_all.T
    c_old, n_old, m_old = c_scr[...], n_scr[...], m_scr[...]

    hs = range(N_HEADS)
    lane_head = _lane_head(GROUP, HEAD_DIM)
    lane128 = lax.broadcasted_iota(jnp.int32, (lc, 128), 1)
    si = lax.broadcasted_iota(jnp.int32, (lc, lc), 0)
    ti = lax.broadcasted_iota(jnp.int32, (lc, lc), 1)
    causal = si <= ti
    src3 = [t.astype(F32) for t in _split3(src_all)]
    bh3 = [t.astype(F32) for t in _split3(bh_all)]

    def decay_logits(h):
        a, b = lane128 == MISC_IG + h, lane128 == MISC_FG + h
        lhs = jnp.concatenate([jnp.where(a, t, jnp.where(b, 1.0, 0.0)) for t in src3], axis=1)
        rhs = jnp.concatenate([jnp.where(a, 1.0, jnp.where(b, t, 0.0)) for t in bh3], axis=1)
        return _dot_nt(lhs, rhs)

    ig = [gates_t[MISC_IG + h:MISC_IG + h + 1, :] for h in hs]
    bh = [bh_t[MISC_FG + h:MISC_FG + h + 1, :] for h in hs]
    m_prev = [m_old[:, h:h + 1] for h in hs]
    dm = [jnp.where(causal, decay_logits(h), -jnp.inf) for h in hs]
    kq = [_dot_nt(kb, jnp.where(lane_head == h, q, 0.0)) for h in hs]
    m_state = [bh[h] + m_prev[h] for h in hs]
    m_t = [jnp.maximum(m_state[h], jnp.max(dm[h], axis=0, keepdims=True)) for h in hs]
    sc = [kq[h] * jnp.exp(dm[h] - m_t[h]) for h in hs]
    g = [jnp.exp(m_state[h] - m_t[h]) for h in hs]
    row8 = lax.broadcasted_iota(jnp.int32, (8, GROUP), 0)
    qn = _dot_nt(jnp.where(row8 == lane_head, n_old, 0.0), q)
    den = [jnp.sum(sc[h], axis=0, keepdims=True) + g[h] * qn[h:h + 1, :] for h in hs]
    q_c = _dot_nt(c_old, q)
    num = [_dot_tn(vb, sc[h]) for h in hs]
    parts = []
    for h in hs:
        rows = slice(HEAD_DIM * h, HEAD_DIM * (h + 1))
        parts.append((num[h][rows, :] + g[h] * q_c[rows, :])
                     / jnp.maximum(jnp.abs(den[h]), jnp.exp(-m_t[h])))
    hh = jnp.concatenate(parts, axis=0).T
    y_ref[...] = (_head_rms(hh, on_ref[...]) * _sigmoid(o_ref[...])).astype(BF16)

    m_new = [m_t[h][:, lc - 1:lc] for h in hs]
    bh_last = [bh[h][:, lc - 1:lc] for h in hs]
    ws = [jnp.exp(bh_last[h] - bh[h] + ig[h] - m_new[h]) for h in hs]
    gl_row = jnp.zeros((1, GROUP), F32)
    m_new_row = m_old
    ws8 = jnp.zeros((8, lc), F32)
    row8s = lax.broadcasted_iota(jnp.int32, (8, lc), 0)
    for h in hs:
        gl_row = jnp.where(lane_head == h, jnp.exp(bh_last[h] + m_prev[h] - m_new[h]), gl_row)
        m_new_row = jnp.where(lane128[0:1, :] == h, m_new[h], m_new_row)
        ws8 = jnp.where(row8s == h, ws[h], ws8)
    n_mat = _dot(ws8, kb)
    n_upd = jnp.zeros((1, GROUP), F32)
    for h in hs:
        n_upd = jnp.where(lane_head == h, n_mat[h:h + 1, :], n_upd)
    w_rows = jnp.concatenate([jnp.broadcast_to(ws[h], (HEAD_DIM, lc)) for h in hs], axis=0)
    c_new = c_old * gl_row + jnp.where(_same_head(GROUP), _dot(v.T * w_rows, kb), 0.0)
    n_new = gl_row * n_old + n_upd
    c_scr[...] = c_new
    n_scr[...] = n_new
    m_scr[...] = m_new_row

    @pl.when(j == pl.num_programs(1) - 1)
    def _():
        cout_ref[...] = _unpack_heads(c_new)

    nout_ref[...] = n_new
    mout_ref[...] = m_new_row


FFN_CHUNKS = ((0, 1024), (1024, 2048), (2048, FFN_HIDDEN))


def _out_kernel(xp_ref, xs_ref, *refs, n_p):
    yp_refs, ys_refs = refs[0:4], refs[4:8]
    wo_ref, nf_ref, wgu_ref, wd_ref, op_ref, os_ref = refs[8:]
    is_prompt = pl.program_id(0) < n_p
    y = jnp.concatenate([jnp.where(is_prompt, a[...], b[...]) for a, b in zip(yp_refs, ys_refs)], axis=1)
    x1 = jnp.where(is_prompt, xp_ref[...], xs_ref[...]) + jnp.dot(y, wo_ref[...], preferred_element_type=F32)
    h = _rms_rows(x1, nf_ref[...]).astype(BF16)
    ffn = None
    for lo, hi in FFN_CHUNKS:
        g = jnp.dot(h, wgu_ref[:, lo:hi], preferred_element_type=F32)
        u = jnp.dot(h, wgu_ref[:, FFN_HIDDEN + lo:FFN_HIDDEN + hi], preferred_element_type=F32)
        a = (g * _sigmoid(g) * u).astype(BF16)
        d = jnp.dot(a, wd_ref[lo:hi, :], preferred_element_type=F32)
        ffn = d if ffn is None else ffn + d
    out = x1 + ffn

    @pl.when(is_prompt)
    def _():
        op_ref[...] = out

    @pl.when(jnp.logical_not(is_prompt))
    def _():
        os_ref[...] = out


def _out_proj_ffn(x_p, x_s, ys_p, ys_s, lw, layer, tm):
    n_p, n_s = x_p.shape[0] // tm, x_s.shape[0] // tm
    yp_spec, ys_spec = _two_source(n_p, tm, GROUP)
    xp_spec, xs_spec = _two_source(n_p, tm, D_MODEL)
    return pl.pallas_call(
        functools.partial(_out_kernel, n_p=n_p),
        grid=(n_p + n_s,),
        in_specs=[xp_spec, xs_spec, *[yp_spec] * 4, *[ys_spec] * 4,
                  pl.BlockSpec((None, D_MODEL, D_MODEL), lambda i: (layer, 0, 0)),
                  pl.BlockSpec((None, 1, D_MODEL), lambda i: (layer, 0, 0)),
                  pl.BlockSpec((None, D_MODEL, 2 * FFN_HIDDEN), lambda i: (layer, 0, 0)),
                  pl.BlockSpec((None, FFN_HIDDEN, D_MODEL), lambda i: (layer, 0, 0))],
        out_specs=[xp_spec, xs_spec],
        out_shape=[jax.ShapeDtypeStruct(x_p.shape, F32), jax.ShapeDtypeStruct(x_s.shape, F32)],
        compiler_params=pltpu.CompilerParams(dimension_semantics=("arbitrary",),
                                             vmem_limit_bytes=VMEM_LIMIT),
        name="out_proj_ffn",
    )(x_p, x_s, *ys_p, *ys_s, lw['w_out'], lw['norm_ffn'], lw['w_gu'], lw['w_down'])


def _seq_params():
    return pltpu.CompilerParams(dimension_semantics=("arbitrary", "arbitrary"),
                                vmem_limit_bytes=VMEM_LIMIT)


def _row_block(row0, seq, tb):
    base, per_seq = row0 // tb, seq // tb
    return lambda col: (lambda b, j: (base + b * per_seq + j, col))


def _layer_spec(layer, shape):
    return pl.BlockSpec((None,) + shape, lambda b, j: (layer,) + (0,) * len(shape))


def _state_spec(state_layer, shape):
    return pl.BlockSpec((None, None) + shape, lambda b, j: (state_layer, b) + (0,) * len(shape))


def _lru_call(proj, conv0, h0, state_layer, lw, layer, nseq, seq, row0, tb):
    rb, ob = _row_block(row0, seq, tb), _row_block(0, seq, tb)
    wl = functools.partial(_layer_spec, layer)
    st = functools.partial(_state_spec, state_layer)
    out_state = lambda shape: pl.BlockSpec((None,) + shape, lambda b, j: (b,) + (0,) * len(shape))
    return pl.pallas_call(
        _lru_kernel,
        grid=(nseq, seq // tb),
        in_specs=[pl.BlockSpec((tb, GROUP), rb(COL_LRU_X // GROUP)),
                  pl.BlockSpec((tb, GROUP), rb(COL_LRU_G // GROUP)),
                  st((8, GROUP)), st((1, GROUP)),
                  wl((CONV_W, GROUP)), wl((1, GROUP)), wl((GROUP, 2 * GROUP)), wl((1, 2 * GROUP)),
                  wl((1, GROUP)), wl((1, GROUP))],
        out_specs=[pl.BlockSpec((tb, GROUP), ob(0)), out_state((1, GROUP)), out_state((8, GROUP))],
        out_shape=[jax.ShapeDtypeStruct((nseq * seq, GROUP), BF16),
                   jax.ShapeDtypeStruct((nseq, 1, GROUP), F32),
                   jax.ShapeDtypeStruct((nseq, 8, GROUP), F32)],
        scratch_shapes=[pltpu.VMEM((tb + 8, GROUP), F32), pltpu.VMEM((1, GROUP), F32)],
        compiler_params=_seq_params(),
        name="rglru",
    )(proj, proj, conv0, h0, lw['conv_w'], lw['conv_b'], lw['lru_wg'], lw['lru_bg'], lw['lru_lambda'], lw['on_a'])


def _mla_pre_call(proj, tabs, lw, layer, tm, rows_p, seq_p):
    rows = proj.shape[0]
    n_p = rows_p // tm
    pos_blocks = seq_p // tm
    wl = lambda shape: pl.BlockSpec((None,) + shape, lambda i: (layer,) + (0,) * len(shape))
    tab = pl.BlockSpec((tm, 128), lambda i: (jnp.where(i < n_p, i % pos_blocks, pos_blocks + i - n_p), 0))
    ckv_p, ckv_s = _two_source(n_p, tm, KV_RANK)
    kr_p, kr_s = _two_source(n_p, tm, ROPE_DIM)
    return pl.pallas_call(
        functools.partial(_mla_pre_kernel, n_p=n_p),
        grid=(rows // tm,),
        in_specs=[pl.BlockSpec((tm, GROUP), lambda i: (i, COL_Q_LAT // GROUP)),
                  pl.BlockSpec((tm, 128), lambda i: (i, COL_KV_LAT // 128)),
                  pl.BlockSpec((tm, 128), lambda i: (i, COL_MISC // 128)),
                  tab, tab, tab,
                  wl((1, GROUP)), wl((GROUP, 512)), wl((1, 512)), wl((1, KV_RANK)), wl((1, 128)),
                  wl((KV_RANK, 1024)), wl((1, 512))],
        out_specs=[pl.BlockSpec((tm, 512), lambda i: (i, 0))] * 3 + [ckv_p, ckv_s, kr_p, kr_s],
        out_shape=[jax.ShapeDtypeStruct((rows, 512), BF16)] * 3
                  + [jax.ShapeDtypeStruct((rows_p, KV_RANK), F32), jax.ShapeDtypeStruct((rows - rows_p, KV_RANK), F32),
                     jax.ShapeDtypeStruct((rows_p, ROPE_DIM), F32), jax.ShapeDtypeStruct((rows - rows_p, ROPE_DIM), F32)],
        compiler_params=pltpu.CompilerParams(dimension_semantics=("arbitrary",),
                                             vmem_limit_bytes=VMEM_LIMIT),
        name="mla_pre",
    )(proj, proj, proj, *tabs, lw['q_norm'], lw['wq'], lw['gq'], lw['kv_norm'], lw['kr_norm'],
      lw['wkv'], lw['gk'])


def _kv_past_call(ckv, krope, lw, tr):
    depth, rows, _ = ckv.shape
    wl = lambda shape: pl.BlockSpec((None,) + shape, lambda l, i: (l,) + (0,) * len(shape))
    return pl.pallas_call(
        _kv_past_kernel,
        grid=(depth, rows // tr),
        in_specs=[pl.BlockSpec((None, tr, KV_RANK), lambda l, i: (l, i, 0)),
                  pl.BlockSpec((None, tr, ROPE_DIM), lambda l, i: (l, i, 0)),
                  wl((KV_RANK, 1024)), wl((1, 512))],
        out_specs=[pl.BlockSpec((None, tr, 512), lambda l, i: (l, i, 0))] * 2,
        out_shape=[jax.ShapeDtypeStruct((depth, rows, 512), BF16)] * 2,
        compiler_params=_seq_params(),
        name="kv_past",
    )(ckv, krope, lw['wkv'], lw['gk'])


def _attn_call(q, k, v, k_past, v_past, lw, layer, nseq, seq, row0, tq, tka):
    rb, ob = _row_block(row0, seq, tq), _row_block(0, seq, tq)
    qspec = pl.BlockSpec((tq, 512), rb(0))
    if k_past is None:
        assert row0 == 0
        ka, va = k, v
        past_spec = pl.BlockSpec((seq, 512), lambda b, j: (b, 0))
        n_past_static = None
    else:
        ka, va = k_past, v_past
        past_len = k_past.shape[1] // nseq
        past_spec = pl.BlockSpec((None, past_len, 512), lambda b, j: (layer, b, 0))
        n_past_static = past_len // tka
    return pl.pallas_call(
        functools.partial(_attn_kernel, tka=tka, n_past_static=n_past_static),
        grid=(nseq, seq // tq),
        in_specs=[qspec, past_spec, past_spec, qspec, qspec, _layer_spec(layer, (1, GROUP))],
        out_specs=pl.BlockSpec((tq, GROUP), ob(0)),
        out_shape=jax.ShapeDtypeStruct((nseq * seq, GROUP), BF16),
        scratch_shapes=[pltpu.VMEM((N_HEADS, 1, tq), F32), pltpu.VMEM((N_HEADS, 128, tq), F32)],
        compiler_params=_seq_params(),
        name="mla_attn",
    )(q, ka, va, k, v, lw['on_b'])


def _ret_call(proj, tabs, tab_map, s0, state_layer, lw, layer, nseq, seq, row0, lc):
    rb, ob = _row_block(row0, seq, lc), _row_block(0, seq, lc)
    blk = lambda col: pl.BlockSpec((lc, GROUP), rb(col // GROUP))
    tab = pl.BlockSpec((lc, 128), tab_map)
    return pl.pallas_call(
        _ret_kernel,
        grid=(nseq, seq // lc),
        in_specs=[blk(COL_R_Q), blk(COL_R_K), blk(COL_R_V), blk(COL_R_G), tab, tab, tab,
                  _state_spec(state_layer, (GROUP, HEAD_DIM)), _layer_spec(layer, (1, GROUP))],
        out_specs=[pl.BlockSpec((lc, GROUP), ob(0)),
                   pl.BlockSpec((None, GROUP, HEAD_DIM), lambda b, j: (b, 0, 0))],
        out_shape=[jax.ShapeDtypeStruct((nseq * seq, GROUP), BF16),
                   jax.ShapeDtypeStruct((nseq, GROUP, HEAD_DIM), F32)],
        scratch_shapes=[pltpu.VMEM((GROUP, GROUP), F32), pltpu.VMEM((N_HEADS, lc, lc), F32),
                        pltpu.VMEM((lc, GROUP), F32), pltpu.VMEM((lc, GROUP), F32)],
        compiler_params=_seq_params(),
        name="retention",
    )(proj, proj, proj, proj, *tabs, s0, lw['on_c'])


def _mlstm_call(proj, c0, n0, m0, state_layer, lw, layer, nseq, seq, row0, lc):
    rb, ob = _row_block(row0, seq, lc), _row_block(0, seq, lc)
    blk = lambda col: pl.BlockSpec((lc, GROUP), rb(col // GROUP))
    st = functools.partial(_state_spec, state_layer)
    out_state = lambda shape: pl.BlockSpec((None,) + shape, lambda b, j: (b,) + (0,) * len(shape))
    return pl.pallas_call(
        _mlstm_kernel,
        grid=(nseq, seq // lc),
        in_specs=[blk(COL_M_Q), blk(COL_M_K), blk(COL_M_V), blk(COL_M_O),
                  pl.BlockSpec((lc, 128), rb(COL_MISC // 128)), _layer_spec(layer, (1, 128)),
                  st((GROUP, HEAD_DIM)), st((1, GROUP)), st((1, 128)), _layer_spec(layer, (1, GROUP))],
        out_specs=[pl.BlockSpec((lc, GROUP), ob(0)), out_state((GROUP, HEAD_DIM)), out_state((1, GROUP)),
                   out_state((1, 128))],
        out_shape=[jax.ShapeDtypeStruct((nseq * seq, GROUP), BF16),
                   jax.ShapeDtypeStruct((nseq, GROUP, HEAD_DIM), F32),
                   jax.ShapeDtypeStruct((nseq, 1, GROUP), F32),
                   jax.ShapeDtypeStruct((nseq, 1, 128), F32)],
        scratch_shapes=[pltpu.VMEM((GROUP, GROUP), F32), pltpu.VMEM((1, GROUP), F32),
                        pltpu.VMEM((1, 128), F32)],
        compiler_params=_seq_params(),
        name="mlstm",
    )(proj, proj, proj, proj, proj, lw['b_if'], c0, n0, m0, lw['on_d'])


def _block_diag(s):
    h, d, e = s.shape[-3:]
    eye = jnp.eye(h, dtype=s.dtype)
    return (s[..., :, :, None, :] * eye[:, None, :, None]).reshape(s.shape[:-3] + (h * d, h * e))


def _rope_tables(pos, half, lanes, lo):
    inv = ROPE_THETA ** (-jnp.arange(half, dtype=F32) / half)
    ang = pos[:, None] * inv[None, :]
    cos, sin = jnp.cos(ang), jnp.sin(ang)
    n = pos.shape[0]
    c = jnp.ones((n, lanes), F32).at[:, lo:lo + 2 * half].set(jnp.concatenate([cos, cos], axis=1))
    sp = jnp.zeros((n, lanes), F32).at[:, lo + half:lo + 2 * half].set(sin)
    sm = jnp.zeros((n, lanes), F32).at[:, lo:lo + half].set(-sin)
    return c, sp, sm


def _prep_weights(norm_mix, w_in, lru_conv_w, lru_conv_b, lru_wa, lru_ba, lru_wx, lru_bx, lru_lambda,
                  mla_q_norm, mla_wq_b, mla_qn_norm, mla_qr_norm, mla_kv_norm, mla_kr_norm, mla_wkv_b,
                  mla_kn_norm, mlstm_b_if, out_norm, w_out, norm_ffn, w_gu, w_down):
    depth = w_in.shape[0]
    row = lambda a: a.reshape(depth, 1, -1)
    n_if = 2 * N_HEADS
    w_in_pad = _w_in_prep(w_in)

    lru_wg = jnp.concatenate([_block_diag(lru_wa), _block_diag(lru_wx)], axis=2).astype(BF16)
    lru_bg = jnp.concatenate([lru_ba, lru_bx], axis=1)

    wq = mla_wq_b.reshape(depth, GROUP, N_HEADS, HEAD_DIM + ROPE_DIM)
    wq = jnp.pad(wq, ((0, 0), (0, 0), (0, 0), (0, 128 - HEAD_DIM - ROPE_DIM))).reshape(depth, GROUP, 512)
    gq = jnp.concatenate([mla_qn_norm, mla_qr_norm, jnp.zeros((depth, 32), F32)], axis=1)
    gq = jnp.tile(gq, (1, N_HEADS))
    wkv = mla_wkv_b.reshape(depth, KV_RANK, N_HEADS, 2 * HEAD_DIM)
    pad_head = lambda w: jnp.pad(w, ((0, 0), (0, 0), (0, 0), (0, 128 - HEAD_DIM))).reshape(depth, KV_RANK, 512)
    wkv = jnp.concatenate([pad_head(wkv[..., :HEAD_DIM]), pad_head(wkv[..., HEAD_DIM:])], axis=2)
    gk = jnp.tile(jnp.concatenate([mla_kn_norm, jnp.zeros((depth, 128 - HEAD_DIM), F32)], axis=1), (1, N_HEADS))
    kr_norm = jnp.pad(mla_kr_norm, ((0, 0), (0, 128 - ROPE_DIM)))
    b_if = jnp.pad(mlstm_b_if, ((0, 0), (MISC_IG, 128 - MISC_IG - n_if)))

    return dict(
        norm_mix=row(norm_mix), w_in=w_in_pad,
        conv_w=lru_conv_w, conv_b=row(lru_conv_b), lru_wg=lru_wg, lru_bg=row(lru_bg), lru_lambda=row(lru_lambda),
        q_norm=row(mla_q_norm), wq=wq.astype(BF16), gq=row(gq), kv_norm=row(mla_kv_norm), kr_norm=row(kr_norm),
        wkv=wkv.astype(BF16), gk=row(gk), b_if=row(b_if),
        on_a=row(out_norm[:, :GROUP]), on_b=row(out_norm[:, GROUP:2 * GROUP]),
        on_c=row(out_norm[:, 2 * GROUP:3 * GROUP]), on_d=row(out_norm[:, 3 * GROUP:]),
        w_out=w_out.astype(BF16), norm_ffn=row(norm_ffn), w_gu=w_gu.astype(BF16), w_down=w_down.astype(BF16))


def _tile_rows(rows, cap):
    t = cap
    while rows % t:
        t //= 2
    return t


def kernel(x_prompt, x_sample, cache_mla_ckv, cache_mla_krope, state_lru_h, state_lru_conv, state_ret, state_mlstm_C, state_mlstm_n, state_mlstm_m, norm_mix, w_in, lru_conv_w, lru_conv_b, lru_wa, lru_ba, lru_wx, lru_bx, lru_lambda, mla_q_norm, mla_wq_b, mla_qn_norm, mla_qr_norm, mla_kv_norm, mla_kr_norm, mla_wkv_b, mla_kn_norm, mlstm_b_if, out_norm, w_out, norm_ffn, w_gu, w_down):
    bp, tp, _ = x_prompt.shape
    bs, ts, _ = x_sample.shape
    depth, _, past, _ = cache_mla_ckv.shape
    rows_p, rows_s = bp * tp, bs * ts
    tm = math.gcd(_tile_rows(tp, 512), rows_s)
    tb_p, tb_s = min(tp, 512), min(ts, 512)
    lc_p, lc_s = min(tp, 256), min(ts, 256)
    tq_p, tq_s = min(tp, 512), min(ts, 512)
    tka_s = _tile_rows(past, 512)
    assert past % CHUNK == 0 and ts % CHUNK == 0 and tp % tm == 0 and rows_s % tm == 0

    lw = _prep_weights(norm_mix, w_in, lru_conv_w, lru_conv_b, lru_wa, lru_ba, lru_wx, lru_bx, lru_lambda,
                       mla_q_norm, mla_wq_b, mla_qn_norm, mla_qr_norm, mla_kv_norm, mla_kr_norm, mla_wkv_b,
                       mla_kn_norm, mlstm_b_if, out_norm, w_out, norm_ffn, w_gu, w_down)

    pos = jnp.concatenate([jnp.arange(tp, dtype=F32), jnp.tile(jnp.arange(past, past + ts, dtype=F32), bs)])
    tabs_mla = _rope_tables(pos, ROPE_DIM // 2, 128, HEAD_DIM)
    tabs_ret = tuple(jnp.concatenate([t, t], axis=1) for t in _rope_tables(pos, HEAD_DIM // 2, HEAD_DIM, 0))

    k_past, v_past = _kv_past_call(cache_mla_ckv.reshape(depth, bs * past, KV_RANK),
                                   cache_mla_krope.reshape(depth, bs * past, ROPE_DIM), lw,
                                   _tile_rows(bs * past, 1024))

    zeros = lambda *shape: jnp.zeros((1, bp) + shape, F32)
    st_p = dict(conv=zeros(8, GROUP), h=zeros(1, GROUP), s=zeros(GROUP, HEAD_DIM), c=zeros(GROUP, HEAD_DIM),
                n=zeros(1, GROUP), m=zeros(1, 128))
    st_s = dict(conv=jnp.pad(state_lru_conv, ((0, 0), (0, 0), (8 - (CONV_W - 1), 0), (0, 0))),
                h=state_lru_h[:, :, None, :], s=state_ret.reshape(depth, bs, GROUP, HEAD_DIM),
                c=state_mlstm_C.reshape(depth, bs, GROUP, HEAD_DIM),
                n=state_mlstm_n.reshape(depth, bs, 1, GROUP),
                m=jnp.pad(state_mlstm_m, ((0, 0), (0, 0), (0, 128 - N_HEADS)))[:, :, None, :])

    x_p = x_prompt.reshape(rows_p, D_MODEL)
    x_s = x_sample.reshape(rows_s, D_MODEL)
    acc = {name: [] for name in ('ckv_p', 'ckv_s', 'kr_p', 'kr_s', 'h_p', 'h_s', 'conv_p', 'conv_s',
                                 's_p', 's_s', 'c_p', 'c_s', 'n_p', 'n_s', 'm_p', 'm_s')}
    for l in range(depth):
        proj = _in_proj(x_p, x_s, lw, l, tm)
        ya_p, h_p, conv_p = _lru_call(proj, st_p['conv'], st_p['h'], 0, lw, l, bp, tp, 0, tb_p)
        ya_s, h_s, conv_s = _lru_call(proj, st_s['conv'], st_s['h'], l, lw, l, bs, ts, rows_p, tb_s)
        q, k, v, ckv_p, ckv_s, kr_p, kr_s = _mla_pre_call(proj, tabs_mla, lw, l, tm, rows_p, tp)
        yb_p = _attn_call(q, k, v, None, None, lw, l, bp, tp, 0, tq_p, tq_p)
        yb_s = _attn_call(q, k, v, k_past, v_past, lw, l, bs, ts, rows_p, tq_s, tka_s)
        yc_p, s_p = _ret_call(proj, tabs_ret, lambda b, j: (j, 0), st_p['s'], 0, lw, l, bp, tp, 0, lc_p)
        yc_s, s_s = _ret_call(proj, tabs_ret, _row_block(tp, ts, lc_s)(0), st_s['s'], l, lw, l, bs, ts, rows_p, lc_s)
        yd_p, c_p, n_p, m_p = _mlstm_call(proj, st_p['c'], st_p['n'], st_p['m'], 0, lw, l, bp, tp, 0, lc_p)
        yd_s, c_s, n_s, m_s = _mlstm_call(proj, st_s['c'], st_s['n'], st_s['m'], l, lw, l, bs, ts, rows_p, lc_s)
        x_p, x_s = _out_proj_ffn(x_p, x_s, (ya_p, yb_p, yc_p, yd_p), (ya_s, yb_s, yc_s, yd_s), lw, l, tm)
        for name, val in (('ckv_p', ckv_p), ('ckv_s', ckv_s), ('kr_p', kr_p), ('kr_s', kr_s), ('h_p', h_p),
                          ('h_s', h_s), ('conv_p', conv_p), ('conv_s', conv_s), ('s_p', s_p), ('s_s', s_s),
                          ('c_p', c_p), ('c_s', c_s), ('n_p', n_p), ('n_s', n_s), ('m_p', m_p), ('m_s', m_s)):
            acc[name].append(val)

    st = {name: jnp.stack(vals) for name, vals in acc.items()}

    def outputs(tag, b, t):
        return (st['ckv_' + tag].reshape(depth, b, t, KV_RANK), st['kr_' + tag].reshape(depth, b, t, ROPE_DIM),
                st['h_' + tag][:, :, 0], st['conv_' + tag][:, :, 8 - (CONV_W - 1):],
                st['s_' + tag].reshape(depth, b, N_HEADS, HEAD_DIM, HEAD_DIM),
                st['c_' + tag].reshape(depth, b, N_HEADS, HEAD_DIM, HEAD_DIM),
                st['n_' + tag].reshape(depth, b, N_HEADS, HEAD_DIM), st['m_' + tag][:, :, 0, :N_HEADS])

    return ((x_p.reshape(bp, tp, D_MODEL), x_s.reshape(bs, ts, D_MODEL))
            + outputs('p', bp, tp) + outputs('s', bs, ts))
```

```python
import functools
import math

import jax
import jax.numpy as jnp
import numpy as np
from jax import lax
from jax.experimental import pallas as pl
from jax.experimental.pallas import tpu as pltpu

F32 = jnp.float32
BF16 = jnp.bfloat16

D_MODEL = 1024
CHUNK = 64
HEAD_DIM = 64
GROUP = 256
N_HEADS = 4
RMS_EPS = 1e-6
ROPE_THETA = 10000.0
CONV_W = 4
LRU_C = 8.0
KV_RANK = 128
ROPE_DIM = 32
FFN_HIDDEN = 2816
IN_PAD = 3072

COL_LRU_X, COL_LRU_G, COL_Q_LAT = 0, 256, 512
COL_KV_LAT, COL_MISC = 768, 896
COL_R_Q, COL_R_K, COL_R_V, COL_R_G = 1024, 1280, 1536, 1792
COL_M_Q, COL_M_K, COL_M_V, COL_M_O = 2048, 2304, 2560, 2816
MISC_IG = 32
MISC_FG = MISC_IG + N_HEADS
ATT_SCALE = (HEAD_DIM + ROPE_DIM) ** -0.5
RET_LOG_DECAY = tuple(math.log(1.0 - 2.0 ** (-5.0 - h)) for h in range(N_HEADS))

VMEM_LIMIT = 56 * 1024 * 1024


def _dot(a, b):
    return jnp.dot(a.astype(BF16), b.astype(BF16), preferred_element_type=F32)


def _dot_nt(a, b):
    return lax.dot_general(a.astype(BF16), b.astype(BF16), (((1,), (1,)), ((), ())),
                           preferred_element_type=F32)


def _dot_tn(a, b):
    return lax.dot_general(a.astype(BF16), b.astype(BF16), (((0,), (0,)), ((), ())),
                           preferred_element_type=F32)


def _split3(x):
    hi = x.astype(BF16)
    r1 = x - hi.astype(F32)
    mid = r1.astype(BF16)
    lo = (r1 - mid.astype(F32)).astype(BF16)
    return hi, mid, lo


def _dot_f32_lhs(x, m, terms=2):
    parts = [jnp.dot(t, m, preferred_element_type=F32) for t in _split3(x)[:terms]]
    return functools.reduce(lambda a, b: a + b, parts)


def _pack_heads(s):
    r = lax.broadcasted_iota(jnp.int32, (HEAD_DIM, GROUP), 0)
    c = lax.broadcasted_iota(jnp.int32, (HEAD_DIM, GROUP), 1)
    tile = jnp.where(c % HEAD_DIM == r, 1.0, 0.0).astype(BF16)
    return jnp.where(_same_head(GROUP), _dot_f32_lhs(s, tile, terms=3), 0.0)


def _unpack_heads(s):
    r = lax.broadcasted_iota(jnp.int32, (GROUP, HEAD_DIM), 0)
    c = lax.broadcasted_iota(jnp.int32, (GROUP, HEAD_DIM), 1)
    fold = jnp.where(r % HEAD_DIM == c, 1.0, 0.0).astype(BF16)
    return _dot_f32_lhs(s, fold, terms=3)


def _rms_rows(x, g):
    return x * lax.rsqrt(jnp.mean(x * x, axis=-1, keepdims=True) + RMS_EPS) * g


def _lane_head(width, head_width):
    return lax.broadcasted_iota(jnp.int32, (1, width), 1) // head_width


def _same_head(n):
    r = lax.broadcasted_iota(jnp.int32, (n, n), 0) // HEAD_DIM
    c = lax.broadcasted_iota(jnp.int32, (n, n), 1) // HEAD_DIM
    return r == c


def _head_rms(y, g):
    mean_mat = jnp.where(_same_head(GROUP), 1.0 / HEAD_DIM, 0.0).astype(BF16)
    return y * lax.rsqrt(_dot_f32_lhs(y * y, mean_mat) + RMS_EPS) * g


def _shift_rows(x, d, fill):
    rows = lax.broadcasted_iota(jnp.int32, x.shape, 0)
    return jnp.where(rows >= d, pltpu.roll(x, d, 0), fill)


def _cumsum_rows(x):
    d = 1
    while d < x.shape[0]:
        x = x + _shift_rows(x, d, 0.0)
        d *= 2
    return x


def _softplus(z):
    return jnp.maximum(z, 0.0) + jnp.log1p(jnp.exp(-jnp.abs(z)))


def _sigmoid(z):
    return 1.0 / (1.0 + jnp.exp(-z))


def _gelu_tanh(z):
    return 0.5 * z * (1.0 + jnp.tanh(math.sqrt(2.0 / math.pi) * (z + 0.044715 * (z * z * z))))


def _rope_lanes(x, c, sp, sm, half):
    w = x.shape[1]
    return x * c + pltpu.roll(x, half, 1) * sp + pltpu.roll(x, w - half, 1) * sm


def _tile_lanes(t, n):
    return jnp.concatenate([t] * n, axis=1) if n > 1 else t


def _w_in_prep_kernel(wt_ref, o_ref):
    n_in, cols = wt_ref.shape
    n_if = 2 * N_HEADS
    split = COL_MISC + ROPE_DIM
    o_ref[:, :COL_MISC] = wt_ref[0:COL_MISC, :].T.astype(BF16)
    misc = jnp.concatenate([wt_ref[COL_MISC:split, :], wt_ref[n_in - n_if:n_in, :],
                            jnp.zeros((COL_R_Q - split - n_if, cols), F32)], axis=0)
    o_ref[:, COL_MISC:COL_R_Q] = misc.T.astype(BF16)
    o_ref[:, COL_R_Q:] = wt_ref[split:n_in - n_if, :].T.astype(BF16)


def _w_in_prep(w_in_t, cols=256):
    depth, n_in, d_model = w_in_t.shape
    return pl.pallas_call(
        _w_in_prep_kernel,
        grid=(depth, d_model // cols),
        in_specs=[pl.BlockSpec((None, n_in, cols), lambda l, i: (l, 0, i))],
        out_specs=pl.BlockSpec((None, cols, IN_PAD), lambda l, i: (l, i, 0)),
        out_shape=jax.ShapeDtypeStruct((depth, d_model, IN_PAD), BF16),
        compiler_params=_seq_params(),
        name="w_in_prep",
    )(w_in_t)


def _in_proj_kernel(xp_ref, xs_ref, g_ref, w_ref, o_ref, *, n_p):
    x = jnp.where(pl.program_id(0) < n_p, xp_ref[...], xs_ref[...])
    o_ref[...] = jnp.dot(_rms_rows(x, g_ref[...]).astype(BF16), w_ref[...], preferred_element_type=F32)


def _two_source(n_p, tm, width):
    return (pl.BlockSpec((tm, width), lambda i: (jnp.minimum(i, n_p - 1), 0)),
            pl.BlockSpec((tm, width), lambda i: (jnp.maximum(i - n_p, 0), 0)))


def _in_proj(x_p, x_s, lw, layer, tm):
    n_p, n_s = x_p.shape[0] // tm, x_s.shape[0] // tm
    return pl.pallas_call(
        functools.partial(_in_proj_kernel, n_p=n_p),
        grid=(n_p + n_s,),
        in_specs=[*_two_source(n_p, tm, D_MODEL),
                  pl.BlockSpec((None, 1, D_MODEL), lambda i: (layer, 0, 0)),
                  pl.BlockSpec((None, D_MODEL, IN_PAD), lambda i: (layer, 0, 0))],
        out_specs=pl.BlockSpec((tm, IN_PAD), lambda i: (i, 0)),
        out_shape=jax.ShapeDtypeStruct((x_p.shape[0] + x_s.shape[0], IN_PAD), F32),
        compiler_params=pltpu.CompilerParams(dimension_semantics=("arbitrary",),
                                             vmem_limit_bytes=VMEM_LIMIT),
        name="in_proj",
    )(x_p, x_s, lw['norm_mix'], lw['w_in'])


def _lru_kernel(x_ref, g_ref, conv0_ref, h0_ref, cw_ref, cb_ref, wg_ref, bg_ref, lam_ref, on_ref,
                y_ref, hout_ref, convout_ref, xp_scr, h_scr):
    j = pl.program_id(1)
    tb = x_ref.shape[0]

    @pl.when(j == 0)
    def _():
        xp_scr[0:8, :] = conv0_ref[...]
        h_scr[...] = h0_ref[...]

    x = x_ref[...]
    xp_scr[8:8 + tb, :] = x
    cw = cw_ref[...]
    xc = (cb_ref[...] + cw[3:4] * x + cw[2:3] * xp_scr[7:7 + tb, :]
          + cw[1:2] * xp_scr[6:6 + tb, :] + cw[0:1] * xp_scr[5:5 + tb, :])
    tail = xp_scr[tb:tb + 8, :]
    xp_scr[0:8, :] = tail
    convout_ref[...] = tail

    gates = _dot(xc, wg_ref[...]) + bg_ref[...]
    r = _sigmoid(gates[:, :GROUP])
    i = _sigmoid(gates[:, GROUP:])
    log_a = -LRU_C * r * _softplus(-lam_ref[...])
    a = jnp.exp(log_a)
    u = jnp.sqrt(-jnp.tanh(log_a) * (a * a + 1.0)) * i * xc

    d = 1
    while d < tb:
        u = a * _shift_rows(u, d, 0.0) + u
        a = a * _shift_rows(a, d, 1.0)
        d *= 2
    h = u + a * h_scr[...]
    h_last = h[tb - 1:tb, :]
    h_scr[...] = h_last
    hout_ref[...] = h_last
    y_ref[...] = (_head_rms(h, on_ref[...]) * _gelu_tanh(g_ref[...])).astype(BF16)


def _mla_seg_rms(x, g, rope):
    if rope:
        r = lax.broadcasted_iota(jnp.int32, (256, 256), 0)
        c = lax.broadcasted_iota(jnp.int32, (256, 256), 1)
        same = (r // 128) == (c // 128)
        rl, cl = r % 128, c % 128
        in_nope = same & (rl < HEAD_DIM) & (cl < HEAD_DIM)
        in_rope = same & (rl >= HEAD_DIM) & (cl >= HEAD_DIM)
        seg = jnp.where(in_nope, 1.0 / HEAD_DIM, jnp.where(in_rope, 1.0 / ROPE_DIM, 0.0)).astype(BF16)
        ms = jnp.concatenate([_dot_f32_lhs(jnp.square(x[:, 256 * p:256 * p + 256]), seg)
                              for p in range(N_HEADS // 2)], axis=1)
        return x * lax.rsqrt(ms + RMS_EPS) * g
    outs = []
    for h in range(N_HEADS):
        xh = x[:, 128 * h:128 * h + 128]
        outs.append(xh * lax.rsqrt(jnp.sum(xh * xh, axis=-1, keepdims=True) * (1.0 / HEAD_DIM) + RMS_EPS))
    return jnp.concatenate(outs, axis=1) * g


def _with_ones_lane(v):
    lane = lax.broadcasted_iota(jnp.int32, v.shape, 1) % 128
    return jnp.where(lane == HEAD_DIM, 1.0, v)


def _mla_pre_kernel(qlat_ref, kvlat_ref, misc_ref, c_ref, sp_ref, sm_ref,
                    qn_ref, wq_ref, gq_ref, kvn_ref, krn_ref, wkv_ref, gk_ref,
                    q_out, k_out, v_out, ckv_p, ckv_s, kr_p, kr_s, *, n_p):
    c, sp, sm = c_ref[...], sp_ref[...], sm_ref[...]
    half = ROPE_DIM // 2

    qraw = _dot(_rms_rows(qlat_ref[...], qn_ref[...]), wq_ref[...])
    qh = _mla_seg_rms(qraw, gq_ref[...], rope=True)
    q = _rope_lanes(qh, _tile_lanes(c, 4), _tile_lanes(sp, 4), _tile_lanes(sm, 4), half)
    q_out[...] = q.astype(BF16)

    ckv = _rms_rows(kvlat_ref[...], kvn_ref[...])
    misc = misc_ref[...]
    lane = lax.broadcasted_iota(jnp.int32, misc.shape, 1)
    kr = jnp.where(lane < ROPE_DIM, misc, 0.0)
    kr = kr * lax.rsqrt(jnp.sum(kr * kr, axis=-1, keepdims=True) * (1.0 / ROPE_DIM) + RMS_EPS) * krn_ref[...]
    kr = _rope_lanes(pltpu.roll(kr, HEAD_DIM, 1), c, sp, sm, half)
    kr_new = pltpu.roll(kr, 128 - HEAD_DIM, 1)
    is_prompt = pl.program_id(0) < n_p

    @pl.when(is_prompt)
    def _():
        ckv_p[...] = ckv
        kr_p[...] = kr_new.T[:ROPE_DIM, :]

    @pl.when(jnp.logical_not(is_prompt))
    def _():
        ckv_s[...] = ckv
        kr_s[...] = kr_new[:, :ROPE_DIM]

    kv = _dot(ckv, wkv_ref[...])
    kn = _mla_seg_rms(kv[:, :512], gk_ref[...], rope=False)
    k_out[...] = (kn + _tile_lanes(kr, 4)).astype(BF16)
    v_out[...] = _with_ones_lane(kv[:, 512:]).astype(BF16)


def _kv_past_kernel(ckv_ref, kr_ref, wkv_ref, gk_ref, k_out, v_out):
    kv = _dot(ckv_ref[...], wkv_ref[...])
    kn = _mla_seg_rms(kv[:, :512], gk_ref[...], rope=False)
    r = lax.broadcasted_iota(jnp.int32, (ROPE_DIM, 128), 0)
    cidx = lax.broadcasted_iota(jnp.int32, (ROPE_DIM, 128), 1)
    place = jnp.where(cidx == r + HEAD_DIM, 1.0, 0.0).astype(BF16)
    kr = _dot_tn(kr_ref[...], place)
    k_out[...] = (kn + _tile_lanes(kr, 4)).astype(BF16)
    v_out[...] = _with_ones_lane(kv[:, 512:]).astype(BF16)


def _attn_kernel(q_ref, ka_ref, va_ref, kb_ref, vb_ref, on_ref, y_ref, m_scr, acc_scr, *, tka, n_past_static):
    j = pl.program_id(1)
    tq = q_ref.shape[0]
    n_past = j * (tq // tka) if n_past_static is None else n_past_static
    key_c = lax.broadcasted_iota(jnp.int32, (tq, tq), 0) // CHUNK
    qry_c = lax.broadcasted_iota(jnp.int32, (tq, tq), 1) // CHUNK
    visible = key_c <= qry_c
    heads = [slice(128 * h, 128 * h + 128) for h in range(N_HEADS)]
    c = ATT_SCALE * math.log2(math.e)

    s = [jnp.where(visible, _dot_nt(kb_ref[:, hs], q_ref[:, hs]) * c, -jnp.inf) for hs in heads]
    m = [jnp.max(s[h], axis=0, keepdims=True) for h in range(N_HEADS)]
    p = [jnp.exp2(s[h] - m[h]).astype(BF16) for h in range(N_HEADS)]
    pv = [_dot_tn(vb_ref[:, hs], p[h]) for h, hs in enumerate(heads)]
    for h in range(N_HEADS):
        m_scr[h] = m[h]
        acc_scr[h] = pv[h]

    def body(t, carry):
        off = pl.multiple_of(t * tka, tka)
        m_old = [m_scr[h] for h in range(N_HEADS)]
        s = [_dot_nt(ka_ref[pl.ds(off, tka), hs], q_ref[:, hs]) * c for hs in heads]
        m_new = [jnp.maximum(m_old[h], jnp.max(s[h], axis=0, keepdims=True)) for h in range(N_HEADS)]
        p = [jnp.exp2(s[h] - m_new[h]).astype(BF16) for h in range(N_HEADS)]
        pv = [_dot_tn(va_ref[pl.ds(off, tka), hs], p[h]) for h, hs in enumerate(heads)]
        for h in range(N_HEADS):
            acc_scr[h] = jnp.exp2(m_old[h] - m_new[h]) * acc_scr[h] + pv[h]
            m_scr[h] = m_new[h]
        return carry

    lax.fori_loop(0, n_past, body, 0)
    outs = []
    for h in range(N_HEADS):
        a = acc_scr[h]
        outs.append(a[:HEAD_DIM, :] / a[HEAD_DIM:HEAD_DIM + 1, :])
    o = jnp.concatenate(outs, axis=0).T
    y_ref[...] = _head_rms(o, on_ref[...]).astype(BF16)


def _ret_kernel(q_ref, k_ref, v_ref, g_ref, c_ref, sp_ref, sm_ref, s0_ref, on_ref,
                y_ref, sout_ref, s_scr, dmat_scr, cross_scr, tail_scr):
    b, j = pl.program_id(0), pl.program_id(1)
    lc = q_ref.shape[0]
    lane_head = _lane_head(GROUP, HEAD_DIM)
    lg_lane = jnp.zeros((1, GROUP), F32)
    for h in range(N_HEADS):
        lg_lane = jnp.where(lane_head == h, RET_LOG_DECAY[h], lg_lane)

    @pl.when((b == 0) & (j == 0))
    def _():
        t_col = lax.broadcasted_iota(jnp.int32, (lc, 1), 0).astype(F32)
        cross_scr[...] = jnp.exp((t_col + 1.0) * lg_lane)
        tail_scr[...] = jnp.exp((lc - 1.0 - t_col) * lg_lane)
        ti = lax.broadcasted_iota(jnp.int32, (lc, lc), 0)
        si = lax.broadcasted_iota(jnp.int32, (lc, lc), 1)
        causal = ti >= si
        diff = jnp.where(causal, ti - si, 0).astype(F32)
        for h in range(N_HEADS):
            dmat_scr[h] = jnp.where(causal, jnp.exp(diff * RET_LOG_DECAY[h]), 0.0)

    @pl.when(j == 0)
    def _():
        s_scr[...] = _pack_heads(s0_ref[...])

    c, sp, sm = (_tile_lanes(t[...], 2) for t in (c_ref, sp_ref, sm_ref))
    half = HEAD_DIM // 2
    q = _rope_lanes(q_ref[...], c, sp, sm, half)
    k = _rope_lanes(k_ref[...], c, sp, sm, half) * (HEAD_DIM ** -0.5)
    v = v_ref[...].astype(BF16)
    kb = k.astype(BF16)

    s_old = s_scr[...]
    y = _dot(q, s_old) * cross_scr[...]
    hs = range(N_HEADS)
    att = [_dot_nt(jnp.where(lane_head == h, q, 0.0), kb) for h in hs]
    att = [(att[h] * dmat_scr[h]).astype(BF16) for h in hs]
    yh = [_dot(att[h], v) for h in hs]
    for h in hs:
        y = y + jnp.where(lane_head == h, yh[h], 0.0)

    s_new = s_old * jnp.exp(float(lc) * lg_lane) + jnp.where(_same_head(GROUP), _dot_tn(k * tail_scr[...], v), 0.0)
    s_scr[...] = s_new

    @pl.when(j == pl.num_programs(1) - 1)
    def _():
        sout_ref[...] = _unpack_heads(s_new)

    y_ref[...] = (_head_rms(y, on_ref[...]) * (g_ref[...] * _sigmoid(g_ref[...]))).astype(BF16)


def _mlstm_kernel(q_ref, k_ref, v_ref, o_ref, misc_ref, bif_ref, c0_ref, n0_ref, m0_ref, on_ref,
                  y_ref, cout_ref, nout_ref, mout_ref, c_scr, n_scr, m_scr):
    j = pl.program_id(1)
    lc = q_ref.shape[0]

    @pl.when(j == 0)
    def _():
        c_scr[...] = _pack_heads(c0_ref[...])
        n_scr[...] = n0_ref[...]
        m_scr[...] = m0_ref[...]

    q = q_ref[...]
    kb = (k_ref[...] * (HEAD_DIM ** -0.5)).astype(BF16)
    v = v_ref[...]
    vb = v.astype(BF16)
    gates = misc_ref[...] + bif_ref[...]
    bh_all = _cumsum_rows(-_softplus(-gates))
    src_all = gates - pltpu.roll(bh_all, 128 - N_HEADS, 1)
    gates_t, bh_t = gates.T, bh_all.T
    c_old, n_old, m_old = c_scr[...], n_scr[...], m_scr[...]

    hs = range(N_HEADS)
    lane_head = _lane_head(GROUP, HEAD_DIM)
    lane128 = lax.broadcasted_iota(jnp.int32, (lc, 128), 1)
    si = lax.broadcasted_iota(jnp.int32, (lc, lc), 0)
    ti = lax.broadcasted_iota(jnp.int32, (lc, lc), 1)
    causal = si <= ti
    src3 = [t.astype(F32) for t in _split3(src_all)]
    bh3 = [t.astype(F32) for t in _split3(bh_all)]

    def decay_logits(h):
        a, b = lane128 == MISC_IG + h, lane128 == MISC_FG + h
        lhs = jnp.concatenate([jnp.where(a, t, jnp.where(b, 1.0, 0.0)) for t in src3], axis=1)
        rhs = jnp.concatenate([jnp.where(a, 1.0, jnp.where(b, t, 0.0)) for t in bh3], axis=1)
        return _dot_nt(lhs, rhs)

    ig = [gates_t[MISC_IG + h:MISC_IG + h + 1, :] for h in hs]
    bh = [bh_t[MISC_FG + h:MISC_FG + h + 1, :] for h in hs]
    m_prev = [m_old[:, h:h + 1] for h in hs]
    dm = [jnp.where(causal, decay_logits(h), -jnp.inf) for h in hs]
    kq = [_dot_nt(kb, jnp.where(lane_head == h, q, 0.0)) for h in hs]
    m_state = [bh[h] + m_prev[h] for h in hs]
    m_t = [jnp.maximum(m_state[h], jnp.max(dm[h], axis=0, keepdims=True)) for h in hs]
    sc = [kq[h] * jnp.exp(dm[h] - m_t[h]) for h in hs]
    g = [jnp.exp(m_state[h] - m_t[h]) for h in hs]
    row8 = lax.broadcasted_iota(jnp.int32, (8, GROUP), 0)
    qn = _dot_nt(jnp.where(row8 == lane_head, n_old, 0.0), q)
    den = [jnp.sum(sc[h], axis=0, keepdims=True) + g[h] * qn[h:h + 1, :] for h in hs]
    q_c = _dot_nt(c_old, q)
    num = [_dot_tn(vb, sc[h]) for h in hs]
    parts = []
    for h in hs:
        rows = slice(HEAD_DIM * h, HEAD_DIM * (h + 1))
        parts.append((num[h][rows, :] + g[h] * q_c[rows, :])
                     / jnp.maximum(jnp.abs(den[h]), jnp.exp(-m_t[h])))
    hh = jnp.concatenate(parts, axis=0).T
    y_ref[...] = (_head_rms(hh, on_ref[...]) * _sigmoid(o_ref[...])).astype(BF16)

    m_new = [m_t[h][:, lc - 1:lc] for h in hs]
    bh_last = [bh[h][:, lc - 1:lc] for h in hs]
    ws = [jnp.exp(bh_last[h] - bh[h] + ig[h] - m_new[h]) for h in hs]
    gl_row = jnp.zeros((1, GROUP), F32)
    m_new_row = m_old
    ws8 = jnp.zeros((8, lc), F32)
    row8s = lax.broadcasted_iota(jnp.int32, (8, lc), 0)
    for h in hs:
        gl_row = jnp.where(lane_head == h, jnp.exp(bh_last[h] + m_prev[h] - m_new[h]), gl_row)
        m_new_row = jnp.where(lane128[0:1, :] == h, m_new[h], m_new_row)
        ws8 = jnp.where(row8s == h, ws[h], ws8)
    n_mat = _dot(ws8, kb)
    n_upd = jnp.zeros((1, GROUP), F32)
    for h in hs:
        n_upd = jnp.where(lane_head == h, n_mat[h:h + 1, :], n_upd)
    w_rows = jnp.concatenate([jnp.broadcast_to(ws[h], (HEAD_DIM, lc)) for h in hs], axis=0)
    c_new = c_old * gl_row + jnp.where(_same_head(GROUP), _dot(v.T * w_rows, kb), 0.0)
    n_new = gl_row * n_old + n_upd
    c_scr[...] = c_new
    n_scr[...] = n_new
    m_scr[...] = m_new_row

    @pl.when(j == pl.num_programs(1) - 1)
    def _():
        cout_ref[...] = _unpack_heads(c_new)

    nout_ref[...] = n_new
    mout_ref[...] = m_new_row


FFN_CHUNKS = ((0, 1024), (1024, 2048), (2048, FFN_HIDDEN))


def _out_kernel(xp_ref, xs_ref, *refs, n_p):
    yp_refs, ys_refs = refs[0:4], refs[4:8]
    wo_ref, nf_ref, wgu_ref, wd_ref, op_ref, os_ref = refs[8:]
    is_prompt = pl.program_id(0) < n_p
    y = jnp.concatenate([jnp.where(is_prompt, a[...], b[...]) for a, b in zip(yp_refs, ys_refs)], axis=1)
    x1 = jnp.where(is_prompt, xp_ref[...], xs_ref[...]) + jnp.dot(y, wo_ref[...], preferred_element_type=F32)
    h = _rms_rows(x1, nf_ref[...]).astype(BF16)
    ffn = None
    for lo, hi in FFN_CHUNKS:
        g = jnp.dot(h, wgu_ref[:, lo:hi], preferred_element_type=F32)
        u = jnp.dot(h, wgu_ref[:, FFN_HIDDEN + lo:FFN_HIDDEN + hi], preferred_element_type=F32)
        a = (g * _sigmoid(g) * u).astype(BF16)
        d = jnp.dot(a, wd_ref[lo:hi, :], preferred_element_type=F32)
        ffn = d if ffn is None else ffn + d
    out = x1 + ffn

    @pl.when(is_prompt)
    def _():
        op_ref[...] = out

    @pl.when(jnp.logical_not(is_prompt))
    def _():
        os_ref[...] = out


def _out_proj_ffn(x_p, x_s, ys_p, ys_s, lw, layer, tm):
    n_p, n_s = x_p.shape[0] // tm, x_s.shape[0] // tm
    yp_spec, ys_spec = _two_source(n_p, tm, GROUP)
    xp_spec, xs_spec = _two_source(n_p, tm, D_MODEL)
    return pl.pallas_call(
        functools.partial(_out_kernel, n_p=n_p),
        grid=(n_p + n_s,),
        in_specs=[xp_spec, xs_spec, *[yp_spec] * 4, *[ys_spec] * 4,
                  pl.BlockSpec((None, D_MODEL, D_MODEL), lambda i: (layer, 0, 0)),
                  pl.BlockSpec((None, 1, D_MODEL), lambda i: (layer, 0, 0)),
                  pl.BlockSpec((None, D_MODEL, 2 * FFN_HIDDEN), lambda i: (layer, 0, 0)),
                  pl.BlockSpec((None, FFN_HIDDEN, D_MODEL), lambda i: (layer, 0, 0))],
        out_specs=[xp_spec, xs_spec],
        out_shape=[jax.ShapeDtypeStruct(x_p.shape, F32), jax.ShapeDtypeStruct(x_s.shape, F32)],
        compiler_params=pltpu.CompilerParams(dimension_semantics=("arbitrary",),
                                             vmem_limit_bytes=VMEM_LIMIT),
        name="out_proj_ffn",
    )(x_p, x_s, *ys_p, *ys_s, lw['w_out'], lw['norm_ffn'], lw['w_gu'], lw['w_down'])


def _seq_params():
    return pltpu.CompilerParams(dimension_semantics=("arbitrary", "arbitrary"),
                                vmem_limit_bytes=VMEM_LIMIT)


def _row_block(row0, seq, tb):
    base, per_seq = row0 // tb, seq // tb
    return lambda col: (lambda b, j: (base + b * per_seq + j, col))


def _layer_spec(layer, shape):
    return pl.BlockSpec((None,) + shape, lambda b, j: (layer,) + (0,) * len(shape))


def _state_spec(state_layer, shape):
    return pl.BlockSpec((None, None) + shape, lambda b, j: (state_layer, b) + (0,) * len(shape))


def _lru_call(proj, conv0, h0, state_layer, lw, layer, nseq, seq, row0, tb):
    rb, ob = _row_block(row0, seq, tb), _row_block(0, seq, tb)
    wl = functools.partial(_layer_spec, layer)
    st = functools.partial(_state_spec, state_layer)
    out_state = lambda shape: pl.BlockSpec((None,) + shape, lambda b, j: (b,) + (0,) * len(shape))
    return pl.pallas_call(
        _lru_kernel,
        grid=(nseq, seq // tb),
        in_specs=[pl.BlockSpec((tb, GROUP), rb(COL_LRU_X // GROUP)),
                  pl.BlockSpec((tb, GROUP), rb(COL_LRU_G // GROUP)),
                  st((8, GROUP)), st((1, GROUP)),
                  wl((CONV_W, GROUP)), wl((1, GROUP)), wl((GROUP, 2 * GROUP)), wl((1, 2 * GROUP)),
                  wl((1, GROUP)), wl((1, GROUP))],
        out_specs=[pl.BlockSpec((tb, GROUP), ob(0)), out_state((1, GROUP)), out_state((8, GROUP))],
        out_shape=[jax.ShapeDtypeStruct((nseq * seq, GROUP), BF16),
                   jax.ShapeDtypeStruct((nseq, 1, GROUP), F32),
                   jax.ShapeDtypeStruct((nseq, 8, GROUP), F32)],
        scratch_shapes=[pltpu.VMEM((tb + 8, GROUP), F32), pltpu.VMEM((1, GROUP), F32)],
        compiler_params=_seq_params(),
        name="rglru",
    )(proj, proj, conv0, h0, lw['conv_w'], lw['conv_b'], lw['lru_wg'], lw['lru_bg'], lw['lru_lambda'], lw['on_a'])


def _mla_pre_call(proj, tabs, lw, layer, tm, nseq_p, seq_p):
    rows, rows_p = proj.shape[0], nseq_p * seq_p
    n_p = rows_p // tm
    pos_blocks = seq_p // tm
    wl = lambda shape: pl.BlockSpec((None,) + shape, lambda i: (layer,) + (0,) * len(shape))
    tab = pl.BlockSpec((tm, 128), lambda i: (jnp.where(i < n_p, i % pos_blocks, pos_blocks + i - n_p), 0))
    ckv_p, ckv_s = _two_source(n_p, tm, KV_RANK)
    _, kr_s = _two_source(n_p, tm, ROPE_DIM)

    def kr_p_map(i):
        t = jnp.minimum(i, n_p - 1)
        return (t // pos_blocks, 0, t % pos_blocks)

    kr_p = pl.BlockSpec((None, ROPE_DIM, tm), kr_p_map)
    return pl.pallas_call(
        functools.partial(_mla_pre_kernel, n_p=n_p),
        grid=(rows // tm,),
        in_specs=[pl.BlockSpec((tm, GROUP), lambda i: (i, COL_Q_LAT // GROUP)),
                  pl.BlockSpec((tm, 128), lambda i: (i, COL_KV_LAT // 128)),
                  pl.BlockSpec((tm, 128), lambda i: (i, COL_MISC // 128)),
                  tab, tab, tab,
                  wl((1, GROUP)), wl((GROUP, 512)), wl((1, 512)), wl((1, KV_RANK)), wl((1, 128)),
                  wl((KV_RANK, 1024)), wl((1, 512))],
        out_specs=[pl.BlockSpec((tm, 512), lambda i: (i, 0))] * 3 + [ckv_p, ckv_s, kr_p, kr_s],
        out_shape=[jax.ShapeDtypeStruct((rows, 512), BF16)] * 3
                  + [jax.ShapeDtypeStruct((rows_p, KV_RANK), F32), jax.ShapeDtypeStruct((rows - rows_p, KV_RANK), F32),
                     jax.ShapeDtypeStruct((nseq_p, ROPE_DIM, seq_p), F32),
                     jax.ShapeDtypeStruct((rows - rows_p, ROPE_DIM), F32)],
        compiler_params=pltpu.CompilerParams(dimension_semantics=("arbitrary",),
                                             vmem_limit_bytes=VMEM_LIMIT),
        name="mla_pre",
    )(proj, proj, proj, *tabs, lw['q_norm'], lw['wq'], lw['gq'], lw['kv_norm'], lw['kr_norm'],
      lw['wkv'], lw['gk'])


def _kv_past_call(ckv, krope_t, lw, tr):
    depth, rows, _ = ckv.shape
    per_seq = krope_t.shape[3] // tr
    wl = lambda shape: pl.BlockSpec((None,) + shape, lambda l, i: (l,) + (0,) * len(shape))
    return pl.pallas_call(
        _kv_past_kernel,
        grid=(depth, rows // tr),
        in_specs=[pl.BlockSpec((None, tr, KV_RANK), lambda l, i: (l, i, 0)),
                  pl.BlockSpec((None, None, ROPE_DIM, tr), lambda l, i: (l, i // per_seq, 0, i % per_seq)),
                  wl((KV_RANK, 1024)), wl((1, 512))],
        out_specs=[pl.BlockSpec((None, tr, 512), lambda l, i: (l, i, 0))] * 2,
        out_shape=[jax.ShapeDtypeStruct((depth, rows, 512), BF16)] * 2,
        compiler_params=_seq_params(),
        name="kv_past",
    )(ckv, krope_t, lw['wkv'], lw['gk'])


def _attn_call(q, k, v, k_past, v_past, lw, layer, nseq, seq, row0, tq, tka):
    rb, ob = _row_block(row0, seq, tq), _row_block(0, seq, tq)
    qspec = pl.BlockSpec((tq, 512), rb(0))
    if k_past is None:
        assert row0 == 0
        ka, va = k, v
        past_spec = pl.BlockSpec((seq, 512), lambda b, j: (b, 0))
        n_past_static = None
    else:
        ka, va = k_past, v_past
        past_len = k_past.shape[1] // nseq
        past_spec = pl.BlockSpec((None, past_len, 512), lambda b, j: (layer, b, 0))
        n_past_static = past_len // tka
    return pl.pallas_call(
        functools.partial(_attn_kernel, tka=tka, n_past_static=n_past_static),
        grid=(nseq, seq // tq),
        in_specs=[qspec, past_spec, past_spec, qspec, qspec, _layer_spec(layer, (1, GROUP))],
        out_specs=pl.BlockSpec((tq, GROUP), ob(0)),
        out_shape=jax.ShapeDtypeStruct((nseq * seq, GROUP), BF16),
        scratch_shapes=[pltpu.VMEM((N_HEADS, 1, tq), F32), pltpu.VMEM((N_HEADS, 128, tq), F32)],
        compiler_params=_seq_params(),
        name="mla_attn",
    )(q, ka, va, k, v, lw['on_b'])


def _ret_call(proj, tabs, tab_map, s0, state_layer, lw, layer, nseq, seq, row0, lc):
    rb, ob = _row_block(row0, seq, lc), _row_block(0, seq, lc)
    blk = lambda col: pl.BlockSpec((lc, GROUP), rb(col // GROUP))
    tab = pl.BlockSpec((lc, 128), tab_map)
    return pl.pallas_call(
        _ret_kernel,
        grid=(nseq, seq // lc),
        in_specs=[blk(COL_R_Q), blk(COL_R_K), blk(COL_R_V), blk(COL_R_G), tab, tab, tab,
                  _state_spec(state_layer, (GROUP, HEAD_DIM)), _layer_spec(layer, (1, GROUP))],
        out_specs=[pl.BlockSpec((lc, GROUP), ob(0)),
                   pl.BlockSpec((None, GROUP, HEAD_DIM), lambda b, j: (b, 0, 0))],
        out_shape=[jax.ShapeDtypeStruct((nseq * seq, GROUP), BF16),
                   jax.ShapeDtypeStruct((nseq, GROUP, HEAD_DIM), F32)],
        scratch_shapes=[pltpu.VMEM((GROUP, GROUP), F32), pltpu.VMEM((N_HEADS, lc, lc), F32),
                        pltpu.VMEM((lc, GROUP), F32), pltpu.VMEM((lc, GROUP), F32)],
        compiler_params=_seq_params(),
        name="retention",
    )(proj, proj, proj, proj, *tabs, s0, lw['on_c'])


def _mlstm_call(proj, c0, n0, m0, state_layer, lw, layer, nseq, seq, row0, lc):
    rb, ob = _row_block(row0, seq, lc), _row_block(0, seq, lc)
    blk = lambda col: pl.BlockSpec((lc, GROUP), rb(col // GROUP))
    st = functools.partial(_state_spec, state_layer)
    out_state = lambda shape: pl.BlockSpec((None,) + shape, lambda b, j: (b,) + (0,) * len(shape))
    return pl.pallas_call(
        _mlstm_kernel,
        grid=(nseq, seq // lc),
        in_specs=[blk(COL_M_Q), blk(COL_M_K), blk(COL_M_V), blk(COL_M_O),
                  pl.BlockSpec((lc, 128), rb(COL_MISC // 128)), _layer_spec(layer, (1, 128)),
                  st((GROUP, HEAD_DIM)), st((1, GROUP)), st((1, 128)), _layer_spec(layer, (1, GROUP))],
        out_specs=[pl.BlockSpec((lc, GROUP), ob(0)), out_state((GROUP, HEAD_DIM)), out_state((1, GROUP)),
                   out_state((1, 128))],
        out_shape=[jax.ShapeDtypeStruct((nseq * seq, GROUP), BF16),
                   jax.ShapeDtypeStruct((nseq, GROUP, HEAD_DIM), F32),
                   jax.ShapeDtypeStruct((nseq, 1, GROUP), F32),
                   jax.ShapeDtypeStruct((nseq, 1, 128), F32)],
        scratch_shapes=[pltpu.VMEM((GROUP, GROUP), F32), pltpu.VMEM((1, GROUP), F32),
                        pltpu.VMEM((1, 128), F32)],
        compiler_params=_seq_params(),
        name="mlstm",
    )(proj, proj, proj, proj, proj, lw['b_if'], c0, n0, m0, lw['on_d'])


def _block_diag(s):
    h, d, e = s.shape[-3:]
    eye = jnp.eye(h, dtype=s.dtype)
    return (s[..., :, :, None, :] * eye[:, None, :, None]).reshape(s.shape[:-3] + (h * d, h * e))


def _rope_tables(pos, half, lanes, lo):
    inv = ROPE_THETA ** (-jnp.arange(half, dtype=F32) / half)
    ang = pos[:, None] * inv[None, :]
    cos, sin = jnp.cos(ang), jnp.sin(ang)
    n = pos.shape[0]
    c = jnp.ones((n, lanes), F32).at[:, lo:lo + 2 * half].set(jnp.concatenate([cos, cos], axis=1))
    sp = jnp.zeros((n, lanes), F32).at[:, lo + half:lo + 2 * half].set(sin)
    sm = jnp.zeros((n, lanes), F32).at[:, lo:lo + half].set(-sin)
    return c, sp, sm


def _prep_weights(norm_mix, w_in, lru_conv_w, lru_conv_b, lru_wa, lru_ba, lru_wx, lru_bx, lru_lambda,
                  mla_q_norm, mla_wq_b, mla_qn_norm, mla_qr_norm, mla_kv_norm, mla_kr_norm, mla_wkv_b,
                  mla_kn_norm, mlstm_b_if, out_norm, w_out, norm_ffn, w_gu, w_down):
    depth = w_in.shape[0]
    row = lambda a: a.reshape(depth, 1, -1)
    n_if = 2 * N_HEADS
    w_in_pad = _w_in_prep(jnp.swapaxes(w_in, 1, 2))

    lru_wg = jnp.concatenate([_block_diag(lru_wa), _block_diag(lru_wx)], axis=2).astype(BF16)
    lru_bg = jnp.concatenate([lru_ba, lru_bx], axis=1)

    wq = mla_wq_b.reshape(depth, GROUP, N_HEADS, HEAD_DIM + ROPE_DIM)
    wq = jnp.pad(wq, ((0, 0), (0, 0), (0, 0), (0, 128 - HEAD_DIM - ROPE_DIM))).reshape(depth, GROUP, 512)
    gq = jnp.concatenate([mla_qn_norm, mla_qr_norm, jnp.zeros((depth, 32), F32)], axis=1)
    gq = jnp.tile(gq, (1, N_HEADS))
    wkv = mla_wkv_b.reshape(depth, KV_RANK, N_HEADS, 2 * HEAD_DIM)
    pad_head = lambda w: jnp.pad(w, ((0, 0), (0, 0), (0, 0), (0, 128 - HEAD_DIM))).reshape(depth, KV_RANK, 512)
    wkv = jnp.concatenate([pad_head(wkv[..., :HEAD_DIM]), pad_head(wkv[..., HEAD_DIM:])], axis=2)
    gk = jnp.tile(jnp.concatenate([mla_kn_norm, jnp.zeros((depth, 128 - HEAD_DIM), F32)], axis=1), (1, N_HEADS))
    kr_norm = jnp.pad(mla_kr_norm, ((0, 0), (0, 128 - ROPE_DIM)))
    b_if = jnp.pad(mlstm_b_if, ((0, 0), (MISC_IG, 128 - MISC_IG - n_if)))

    return dict(
        norm_mix=row(norm_mix), w_in=w_in_pad,
        conv_w=lru_conv_w, conv_b=row(lru_conv_b), lru_wg=lru_wg, lru_bg=row(lru_bg), lru_lambda=row(lru_lambda),
        q_norm=row(mla_q_norm), wq=wq.astype(BF16), gq=row(gq), kv_norm=row(mla_kv_norm), kr_norm=row(kr_norm),
        wkv=wkv.astype(BF16), gk=row(gk), b_if=row(b_if),
        on_a=row(out_norm[:, :GROUP]), on_b=row(out_norm[:, GROUP:2 * GROUP]),
        on_c=row(out_norm[:, 2 * GROUP:3 * GROUP]), on_d=row(out_norm[:, 3 * GROUP:]),
        w_out=w_out.astype(BF16), norm_ffn=row(norm_ffn), w_gu=w_gu.astype(BF16), w_down=w_down.astype(BF16))


def _tile_rows(rows, cap):
    t = cap
    while rows % t:
        t //= 2
    return t


def kernel(x_prompt, x_sample, cache_mla_ckv, cache_mla_krope, state_lru_h, state_lru_conv, state_ret, state_mlstm_C, state_mlstm_n, state_mlstm_m, norm_mix, w_in, lru_conv_w, lru_conv_b, lru_wa, lru_ba, lru_wx, lru_bx, lru_lambda, mla_q_norm, mla_wq_b, mla_qn_norm, mla_qr_norm, mla_kv_norm, mla_kr_norm, mla_wkv_b, mla_kn_norm, mlstm_b_if, out_norm, w_out, norm_ffn, w_gu, w_down):
    bp, tp, _ = x_prompt.shape
    bs, ts, _ = x_sample.shape
    depth, _, past, _ = cache_mla_ckv.shape
    rows_p, rows_s = bp * tp, bs * ts
    tm = math.gcd(_tile_rows(tp, 512), rows_s)
    tb_p, tb_s = min(tp, 512), min(ts, 512)
    lc_p, lc_s = min(tp, 256), min(ts, 256)
    tq_p, tq_s = min(tp, 512), min(ts, 512)
    tka_s = past
    assert past % CHUNK == 0 and ts % CHUNK == 0 and tp % tm == 0 and rows_s % tm == 0

    lw = _prep_weights(norm_mix, w_in, lru_conv_w, lru_conv_b, lru_wa, lru_ba, lru_wx, lru_bx, lru_lambda,
                       mla_q_norm, mla_wq_b, mla_qn_norm, mla_qr_norm, mla_kv_norm, mla_kr_norm, mla_wkv_b,
                       mla_kn_norm, mlstm_b_if, out_norm, w_out, norm_ffn, w_gu, w_down)

    pos = jnp.concatenate([jnp.arange(tp, dtype=F32), jnp.tile(jnp.arange(past, past + ts, dtype=F32), bs)])
    tabs_mla = _rope_tables(pos, ROPE_DIM // 2, 128, HEAD_DIM)
    tabs_ret = tuple(jnp.concatenate([t, t], axis=1) for t in _rope_tables(pos, HEAD_DIM // 2, HEAD_DIM, 0))

    k_past, v_past = _kv_past_call(cache_mla_ckv.reshape(depth, bs * past, KV_RANK),
                                   jnp.swapaxes(cache_mla_krope, 2, 3), lw,
                                   _tile_rows(past, 1024))

    zeros = lambda *shape: jnp.zeros((1, bp) + shape, F32)
    st_p = dict(conv=zeros(8, GROUP), h=zeros(1, GROUP), s=zeros(GROUP, HEAD_DIM), c=zeros(GROUP, HEAD_DIM),
                n=zeros(1, GROUP), m=zeros(1, 128))
    st_s = dict(conv=jnp.pad(state_lru_conv, ((0, 0), (0, 0), (8 - (CONV_W - 1), 0), (0, 0))),
                h=state_lru_h[:, :, None, :], s=state_ret.reshape(depth, bs, GROUP, HEAD_DIM),
                c=state_mlstm_C.reshape(depth, bs, GROUP, HEAD_DIM),
                n=state_mlstm_n.reshape(depth, bs, 1, GROUP),
                m=jnp.pad(state_mlstm_m, ((0, 0), (0, 0), (0, 128 - N_HEADS)))[:, :, None, :])

    x_p = x_prompt.reshape(rows_p, D_MODEL)
    x_s = x_sample.reshape(rows_s, D_MODEL)
    acc = {name: [] for name in ('ckv_p', 'ckv_s', 'kr_p', 'kr_s', 'h_p', 'h_s', 'conv_p', 'conv_s',
                                 's_p', 's_s', 'c_p', 'c_s', 'n_p', 'n_s', 'm_p', 'm_s')}
    for l in range(depth):
        proj = _in_proj(x_p, x_s, lw, l, tm)
        ya_p, h_p, conv_p = _lru_call(proj, st_p['conv'], st_p['h'], 0, lw, l, bp, tp, 0, tb_p)
        ya_s, h_s, conv_s = _lru_call(proj, st_s['conv'], st_s['h'], l, lw, l, bs, ts, rows_p, tb_s)
        q, k, v, ckv_p, ckv_s, kr_p, kr_s = _mla_pre_call(proj, tabs_mla, lw, l, tm, bp, tp)
        yb_p = _attn_call(q, k, v, None, None, lw, l, bp, tp, 0, tq_p, tq_p)
        yb_s = _attn_call(q, k, v, k_past, v_past, lw, l, bs, ts, rows_p, tq_s, tka_s)
        yc_p, s_p = _ret_call(proj, tabs_ret, lambda b, j: (j, 0), st_p['s'], 0, lw, l, bp, tp, 0, lc_p)
        yc_s, s_s = _ret_call(proj, tabs_ret, _row_block(tp, ts, lc_s)(0), st_s['s'], l, lw, l, bs, ts, rows_p, lc_s)
        yd_p, c_p, n_p, m_p = _mlstm_call(proj, st_p['c'], st_p['n'], st_p['m'], 0, lw, l, bp, tp, 0, lc_p)
        yd_s, c_s, n_s, m_s = _mlstm_call(proj, st_s['c'], st_s['n'], st_s['m'], l, lw, l, bs, ts, rows_p, lc_s)
        x_p, x_s = _out_proj_ffn(x_p, x_s, (ya_p, yb_p, yc_p, yd_p), (ya_s, yb_s, yc_s, yd_s), lw, l, tm)
        for name, val in (('ckv_p', ckv_p), ('ckv_s', ckv_s), ('kr_p', kr_p), ('kr_s', kr_s), ('h_p', h_p),
                          ('h_s', h_s), ('conv_p', conv_p), ('conv_s', conv_s), ('s_p', s_p), ('s_s', s_s),
                          ('c_p', c_p), ('c_s', c_s), ('n_p', n_p), ('n_s', n_s), ('m_p', m_p), ('m_s', m_s)):
            acc[name].append(val)

    st = {name: jnp.stack(vals) for name, vals in acc.items()}

    def outputs(tag, b, t):
        krope = jnp.swapaxes(st['kr_p'], 2, 3) if tag == 'p' else st['kr_s'].reshape(depth, b, t, ROPE_DIM)
        return (st['ckv_' + tag].reshape(depth, b, t, KV_RANK), krope,
                st['h_' + tag][:, :, 0], st['conv_' + tag][:, :, 8 - (CONV_W - 1):],
                st['s_' + tag].reshape(depth, b, N_HEADS, HEAD_DIM, HEAD_DIM),
                st['c_' + tag].reshape(depth, b, N_HEADS, HEAD_DIM, HEAD_DIM),
                st['n_' + tag].reshape(depth, b, N_HEADS, HEAD_DIM), st['m_' + tag][:, :, 0, :N_HEADS])

    return ((x_p.reshape(bp, tp, D_MODEL), x_s.reshape(bs, ts, D_MODEL))
            + outputs('p', bp, tp) + outputs('s', bs, ts))
```

```python
import functools
import math

import jax
import jax.numpy as jnp
import numpy as np
from jax import lax
from jax.experimental import pallas as pl
from jax.experimental.pallas import tpu as pltpu

F32 = jnp.float32
BF16 = jnp.bfloat16

D_MODEL = 1024
CHUNK = 64
HEAD_DIM = 64
GROUP = 256
N_HEADS = 4
RMS_EPS = 1e-6
ROPE_THETA = 10000.0
CONV_W = 4
LRU_C = 8.0
KV_RANK = 128
ROPE_DIM = 32
FFN_HIDDEN = 2816
IN_PAD = 3072

COL_LRU_X, COL_LRU_G, COL_Q_LAT = 0, 256, 512
COL_KV_LAT, COL_MISC = 768, 896
COL_R_Q, COL_R_K, COL_R_V, COL_R_G = 1024, 1280, 1536, 1792
COL_M_Q, COL_M_K, COL_M_V, COL_M_O = 2048, 2304, 2560, 2816
MISC_IG = 32
MISC_FG = MISC_IG + N_HEADS
ATT_SCALE = (HEAD_DIM + ROPE_DIM) ** -0.5
RET_LOG_DECAY = tuple(math.log(1.0 - 2.0 ** (-5.0 - h)) for h in range(N_HEADS))

VMEM_LIMIT = 56 * 1024 * 1024


def _dot(a, b):
    return jnp.dot(a.astype(BF16), b.astype(BF16), preferred_element_type=F32)


def _dot_nt(a, b):
    return lax.dot_general(a.astype(BF16), b.astype(BF16), (((1,), (1,)), ((), ())),
                           preferred_element_type=F32)


def _dot_tn(a, b):
    return lax.dot_general(a.astype(BF16), b.astype(BF16), (((0,), (0,)), ((), ())),
                           preferred_element_type=F32)


def _split3(x):
    hi = x.astype(BF16)
    r1 = x - hi.astype(F32)
    mid = r1.astype(BF16)
    lo = (r1 - mid.astype(F32)).astype(BF16)
    return hi, mid, lo


def _dot_f32_lhs(x, m, terms=2):
    parts = [jnp.dot(t, m, preferred_element_type=F32) for t in _split3(x)[:terms]]
    return functools.reduce(lambda a, b: a + b, parts)


def _pack_heads(s):
    r = lax.broadcasted_iota(jnp.int32, (HEAD_DIM, GROUP), 0)
    c = lax.broadcasted_iota(jnp.int32, (HEAD_DIM, GROUP), 1)
    tile = jnp.where(c % HEAD_DIM == r, 1.0, 0.0).astype(BF16)
    return jnp.where(_same_head(GROUP), _dot_f32_lhs(s, tile, terms=3), 0.0)


def _unpack_heads(s):
    r = lax.broadcasted_iota(jnp.int32, (GROUP, HEAD_DIM), 0)
    c = lax.broadcasted_iota(jnp.int32, (GROUP, HEAD_DIM), 1)
    fold = jnp.where(r % HEAD_DIM == c, 1.0, 0.0).astype(BF16)
    return _dot_f32_lhs(s, fold, terms=3)


def _rms_rows(x, g):
    return x * lax.rsqrt(jnp.mean(x * x, axis=-1, keepdims=True) + RMS_EPS) * g


def _lane_head(width, head_width):
    return lax.broadcasted_iota(jnp.int32, (1, width), 1) // head_width


def _same_head(n):
    r = lax.broadcasted_iota(jnp.int32, (n, n), 0) // HEAD_DIM
    c = lax.broadcasted_iota(jnp.int32, (n, n), 1) // HEAD_DIM
    return r == c


def _head_rms(y, g):
    mean_mat = jnp.where(_same_head(GROUP), 1.0 / HEAD_DIM, 0.0).astype(BF16)
    return y * lax.rsqrt(_dot_f32_lhs(y * y, mean_mat) + RMS_EPS) * g


def _shift_rows(x, d, fill):
    rows = lax.broadcasted_iota(jnp.int32, x.shape, 0)
    return jnp.where(rows >= d, pltpu.roll(x, d, 0), fill)


def _cumsum_rows(x):
    d = 1
    while d < x.shape[0]:
        x = x + _shift_rows(x, d, 0.0)
        d *= 2
    return x


def _softplus(z):
    return jnp.maximum(z, 0.0) + jnp.log1p(jnp.exp(-jnp.abs(z)))


def _sigmoid(z):
    return 0.5 * jnp.tanh(0.5 * z) + 0.5


def _gelu_tanh(z):
    return 0.5 * z * (1.0 + jnp.tanh(math.sqrt(2.0 / math.pi) * (z + 0.044715 * (z * z * z))))


def _rope_lanes(x, c, sp, sm, half):
    w = x.shape[1]
    return x * c + pltpu.roll(x, half, 1) * sp + pltpu.roll(x, w - half, 1) * sm


def _tile_lanes(t, n):
    return jnp.concatenate([t] * n, axis=1) if n > 1 else t


def _w_in_prep_kernel(wt_ref, o_ref):
    n_in, cols = wt_ref.shape
    n_if = 2 * N_HEADS
    split = COL_MISC + ROPE_DIM
    o_ref[:, :COL_MISC] = wt_ref[0:COL_MISC, :].T.astype(BF16)
    misc = jnp.concatenate([wt_ref[COL_MISC:split, :], wt_ref[n_in - n_if:n_in, :],
                            jnp.zeros((COL_R_Q - split - n_if, cols), F32)], axis=0)
    o_ref[:, COL_MISC:COL_R_Q] = misc.T.astype(BF16)
    o_ref[:, COL_R_Q:] = wt_ref[split:n_in - n_if, :].T.astype(BF16)


def _w_in_prep(w_in_t, cols=256):
    depth, n_in, d_model = w_in_t.shape
    return pl.pallas_call(
        _w_in_prep_kernel,
        grid=(depth, d_model // cols),
        in_specs=[pl.BlockSpec((None, n_in, cols), lambda l, i: (l, 0, i))],
        out_specs=pl.BlockSpec((None, cols, IN_PAD), lambda l, i: (l, i, 0)),
        out_shape=jax.ShapeDtypeStruct((depth, d_model, IN_PAD), BF16),
        compiler_params=_seq_params(),
        name="w_in_prep",
    )(w_in_t)


def _in_proj_kernel(xp_ref, xs_ref, g_ref, w_ref, o_ref, *, n_p):
    x = jnp.where(pl.program_id(0) < n_p, xp_ref[...], xs_ref[...])
    o_ref[...] = jnp.dot(_rms_rows(x, g_ref[...]).astype(BF16), w_ref[...], preferred_element_type=F32)


def _two_source(n_p, tm, width):
    return (pl.BlockSpec((tm, width), lambda i: (jnp.minimum(i, n_p - 1), 0)),
            pl.BlockSpec((tm, width), lambda i: (jnp.maximum(i - n_p, 0), 0)))


def _in_proj(x_p, x_s, lw, layer, tm):
    n_p, n_s = x_p.shape[0] // tm, x_s.shape[0] // tm
    return pl.pallas_call(
        functools.partial(_in_proj_kernel, n_p=n_p),
        grid=(n_p + n_s,),
        in_specs=[*_two_source(n_p, tm, D_MODEL),
                  pl.BlockSpec((None, 1, D_MODEL), lambda i: (layer, 0, 0)),
                  pl.BlockSpec((None, D_MODEL, IN_PAD), lambda i: (layer, 0, 0))],
        out_specs=pl.BlockSpec((tm, IN_PAD), lambda i: (i, 0)),
        out_shape=jax.ShapeDtypeStruct((x_p.shape[0] + x_s.shape[0], IN_PAD), F32),
        compiler_params=pltpu.CompilerParams(dimension_semantics=("arbitrary",),
                                             vmem_limit_bytes=VMEM_LIMIT),
        name="in_proj",
    )(x_p, x_s, lw['norm_mix'], lw['w_in'])


def _lru_kernel(x_ref, g_ref, conv0_ref, h0_ref, cw_ref, cb_ref, wg_ref, bg_ref, lam_ref, on_ref,
                y_ref, hout_ref, convout_ref, xp_scr, h_scr):
    j = pl.program_id(1)
    tb = x_ref.shape[0]

    @pl.when(j == 0)
    def _():
        xp_scr[...] = conv0_ref[...]
        h_scr[...] = h0_ref[...]

    x = x_ref[...]
    prev = xp_scr[...]
    row8 = lax.broadcasted_iota(jnp.int32, (8, GROUP), 0)

    def delayed(d):
        r = pltpu.roll(x, d, 0)
        head = jnp.where(row8 < d, pltpu.roll(prev, d, 0), r[0:8, :])
        return jnp.concatenate([head, r[8:, :]], axis=0)

    cw = cw_ref[...]
    xc = cb_ref[...] + cw[3:4] * x + cw[2:3] * delayed(1) + cw[1:2] * delayed(2) + cw[0:1] * delayed(3)
    tail = x[tb - 8:tb, :]
    xp_scr[...] = tail
    convout_ref[...] = tail

    gates = _dot(xc, wg_ref[...]) + bg_ref[...]
    r = _sigmoid(gates[:, :GROUP])
    i = _sigmoid(gates[:, GROUP:])
    log_a = -LRU_C * r * _softplus(-lam_ref[...])
    a = jnp.exp(log_a)
    u = jnp.sqrt(-jnp.tanh(log_a) * (a * a + 1.0)) * i * xc

    d = 1
    while d < tb:
        u = a * _shift_rows(u, d, 0.0) + u
        a = a * _shift_rows(a, d, 1.0)
        d *= 2
    h = u + a * h_scr[...]
    h_last = h[tb - 1:tb, :]
    h_scr[...] = h_last
    hout_ref[...] = h_last
    y_ref[...] = (_head_rms(h, on_ref[...]) * _gelu_tanh(g_ref[...])).astype(BF16)


def _mla_seg_rms(x, g, rope):
    if rope:
        r = lax.broadcasted_iota(jnp.int32, (256, 256), 0)
        c = lax.broadcasted_iota(jnp.int32, (256, 256), 1)
        same = (r // 128) == (c // 128)
        rl, cl = r % 128, c % 128
        in_nope = same & (rl < HEAD_DIM) & (cl < HEAD_DIM)
        in_rope = same & (rl >= HEAD_DIM) & (cl >= HEAD_DIM)
        seg = jnp.where(in_nope, 1.0 / HEAD_DIM, jnp.where(in_rope, 1.0 / ROPE_DIM, 0.0)).astype(BF16)
        ms = jnp.concatenate([_dot_f32_lhs(jnp.square(x[:, 256 * p:256 * p + 256]), seg)
                              for p in range(N_HEADS // 2)], axis=1)
        return lax.rsqrt(ms + RMS_EPS) if g is None else x * lax.rsqrt(ms + RMS_EPS) * g
    outs = []
    for h in range(N_HEADS):
        xh = x[:, 128 * h:128 * h + 128]
        outs.append(xh * lax.rsqrt(jnp.sum(xh * xh, axis=-1, keepdims=True) * (1.0 / HEAD_DIM) + RMS_EPS))
    return jnp.concatenate(outs, axis=1) * g


def _with_ones_lane(v):
    lane = lax.broadcasted_iota(jnp.int32, v.shape, 1) % 128
    return jnp.where(lane == HEAD_DIM, 1.0, v)


def _mla_pre_kernel(qlat_ref, kvlat_ref, misc_ref, c_ref, sp_ref, sm_ref,
                    qn_ref, wq_ref, gq_ref, kvn_ref, krn_ref, wkv_ref, gk_ref,
                    q_out, k_out, v_out, ckv_p, ckv_s, kr_p, kr_s, *, n_p):
    c, sp, sm = c_ref[...], sp_ref[...], sm_ref[...]
    half = ROPE_DIM // 2

    qboth = _dot(_rms_rows(qlat_ref[...], qn_ref[...]), wq_ref[...])
    qraw, qswap = qboth[:, :512], qboth[:, 512:]
    gq, gq_swap = gq_ref[:, :512], gq_ref[:, 512:]
    inv = _mla_seg_rms(qraw, None, rope=True)
    q = inv * (qraw * (gq * _tile_lanes(c, 4)) + qswap * (gq_swap * _tile_lanes(sp + sm, 4)))
    q_out[...] = q.astype(BF16)

    ckv = _rms_rows(kvlat_ref[...], kvn_ref[...])
    misc = misc_ref[...]
    lane = lax.broadcasted_iota(jnp.int32, misc.shape, 1)
    kr = jnp.where(lane < ROPE_DIM, misc, 0.0)
    kr = kr * lax.rsqrt(jnp.sum(kr * kr, axis=-1, keepdims=True) * (1.0 / ROPE_DIM) + RMS_EPS) * krn_ref[...]
    kr = _rope_lanes(pltpu.roll(kr, HEAD_DIM, 1), c, sp, sm, half)
    kr_new = pltpu.roll(kr, 128 - HEAD_DIM, 1)
    is_prompt = pl.program_id(0) < n_p

    @pl.when(is_prompt)
    def _():
        ckv_p[...] = ckv
        kr_p[...] = kr_new.T[:ROPE_DIM, :]

    @pl.when(jnp.logical_not(is_prompt))
    def _():
        ckv_s[...] = ckv
        kr_s[...] = kr_new[:, :ROPE_DIM]

    kv = _dot(ckv, wkv_ref[...])
    kn = _mla_seg_rms(kv[:, :512], gk_ref[...], rope=False)
    k_out[...] = (kn + _tile_lanes(kr, 4)).astype(BF16)
    v_out[...] = _with_ones_lane(kv[:, 512:]).astype(BF16)


def _kv_past_kernel(ckv_ref, kr_ref, wkv_ref, gk_ref, k_out, v_out):
    kv = _dot(ckv_ref[...], wkv_ref[...])
    kn = _mla_seg_rms(kv[:, :512], gk_ref[...], rope=False)
    r = lax.broadcasted_iota(jnp.int32, (ROPE_DIM, 128), 0)
    cidx = lax.broadcasted_iota(jnp.int32, (ROPE_DIM, 128), 1)
    place = jnp.where(cidx == r + HEAD_DIM, 1.0, 0.0).astype(BF16)
    kr = _dot_tn(kr_ref[...], place)
    k_out[...] = (kn + _tile_lanes(kr, 4)).astype(BF16)
    v_out[...] = _with_ones_lane(kv[:, 512:]).astype(BF16)


def _attn_kernel(q_ref, ka_ref, va_ref, kb_ref, vb_ref, on_ref, y_ref, m_scr, acc_scr, *, tka, n_past_static):
    j = pl.program_id(1)
    tq = q_ref.shape[0]
    n_past = j * (tq // tka) if n_past_static is None else n_past_static
    key_c = lax.broadcasted_iota(jnp.int32, (tq, tq), 0) // CHUNK
    qry_c = lax.broadcasted_iota(jnp.int32, (tq, tq), 1) // CHUNK
    visible = key_c <= qry_c
    heads = [slice(128 * h, 128 * h + 128) for h in range(N_HEADS)]
    c = ATT_SCALE * math.log2(math.e)

    s = [jnp.where(visible, _dot_nt(kb_ref[:, hs], q_ref[:, hs]) * c, -jnp.inf) for hs in heads]
    m = [jnp.max(s[h], axis=0, keepdims=True) for h in range(N_HEADS)]
    p = [jnp.exp2(s[h] - m[h]).astype(BF16) for h in range(N_HEADS)]
    pv = [_dot_tn(vb_ref[:, hs], p[h]) for h, hs in enumerate(heads)]
    for h in range(N_HEADS):
        m_scr[h] = m[h]
        acc_scr[h] = pv[h]

    def body(t, carry):
        off = pl.multiple_of(t * tka, tka)
        m_old = [m_scr[h] for h in range(N_HEADS)]
        s = [_dot_nt(ka_ref[pl.ds(off, tka), hs], q_ref[:, hs]) * c for hs in heads]
        m_new = [jnp.maximum(m_old[h], jnp.max(s[h], axis=0, keepdims=True)) for h in range(N_HEADS)]
        p = [jnp.exp2(s[h] - m_new[h]).astype(BF16) for h in range(N_HEADS)]
        pv = [_dot_tn(va_ref[pl.ds(off, tka), hs], p[h]) for h, hs in enumerate(heads)]
        for h in range(N_HEADS):
            acc_scr[h] = jnp.exp2(m_old[h] - m_new[h]) * acc_scr[h] + pv[h]
            m_scr[h] = m_new[h]
        return carry

    lax.fori_loop(0, n_past, body, 0)
    outs = []
    for h in range(N_HEADS):
        a = acc_scr[h]
        outs.append(a[:HEAD_DIM, :] / a[HEAD_DIM:HEAD_DIM + 1, :])
    o = jnp.concatenate(outs, axis=0).T
    y_ref[...] = _head_rms(o, on_ref[...]).astype(BF16)


def _ret_kernel(q_ref, k_ref, v_ref, g_ref, c_ref, sp_ref, sm_ref, s0_ref, on_ref,
                y_ref, sout_ref, s_scr, dmat_scr, cross_scr, tail_scr):
    b, j = pl.program_id(0), pl.program_id(1)
    lc = q_ref.shape[0]
    lane_head = _lane_head(GROUP, HEAD_DIM)
    lg_lane = jnp.zeros((1, GROUP), F32)
    for h in range(N_HEADS):
        lg_lane = jnp.where(lane_head == h, RET_LOG_DECAY[h], lg_lane)

    @pl.when((b == 0) & (j == 0))
    def _():
        t_col = lax.broadcasted_iota(jnp.int32, (lc, 1), 0).astype(F32)
        cross_scr[...] = jnp.exp((t_col + 1.0) * lg_lane)
        tail_scr[...] = jnp.exp((lc - 1.0 - t_col) * lg_lane)
        ti = lax.broadcasted_iota(jnp.int32, (lc, lc), 0)
        si = lax.broadcasted_iota(jnp.int32, (lc, lc), 1)
        causal = ti >= si
        diff = jnp.where(causal, ti - si, 0).astype(F32)
        for h in range(N_HEADS):
            dmat_scr[h] = jnp.where(causal, jnp.exp(diff * RET_LOG_DECAY[h]), 0.0)

    @pl.when(j == 0)
    def _():
        s_scr[...] = _pack_heads(s0_ref[...])

    c, sp, sm = (_tile_lanes(t[...], 2) for t in (c_ref, sp_ref, sm_ref))
    half = HEAD_DIM // 2
    q = _rope_lanes(q_ref[...], c, sp, sm, half)
    k = _rope_lanes(k_ref[...], c, sp, sm, half) * (HEAD_DIM ** -0.5)
    v = v_ref[...].astype(BF16)
    kb = k.astype(BF16)

    s_old = s_scr[...]
    y = _dot(q, s_old) * cross_scr[...]
    hs = range(N_HEADS)
    att = [_dot_nt(jnp.where(lane_head == h, q, 0.0), kb) for h in hs]
    att = [(att[h] * dmat_scr[h]).astype(BF16) for h in hs]
    yh = [_dot(att[h], v) for h in hs]
    for h in hs:
        y = y + jnp.where(lane_head == h, yh[h], 0.0)

    s_new = s_old * jnp.exp(float(lc) * lg_lane) + jnp.where(_same_head(GROUP), _dot_tn(k * tail_scr[...], v), 0.0)
    s_scr[...] = s_new

    @pl.when(j == pl.num_programs(1) - 1)
    def _():
        sout_ref[...] = _unpack_heads(s_new)

    y_ref[...] = (_head_rms(y, on_ref[...]) * (g_ref[...] * _sigmoid(g_ref[...]))).astype(BF16)


def _mlstm_kernel(q_ref, k_ref, v_ref, o_ref, misc_ref, bif_ref, c0_ref, n0_ref, m0_ref, on_ref,
                  y_ref, cout_ref, nout_ref, mout_ref, c_scr, n_scr, m_scr):
    j = pl.program_id(1)
    lc = q_ref.shape[0]

    @pl.when(j == 0)
    def _():
        c_scr[...] = _pack_heads(c0_ref[...])
        n_scr[...] = n0_ref[...]
        m_scr[...] = m0_ref[...]

    q = q_ref[...]
    kb = (k_ref[...] * (HEAD_DIM ** -0.5)).astype(BF16)
    v = v_ref[...]
    vb = v.astype(BF16)
    gates = misc_ref[...] + bif_ref[...]
    bh_all = _cumsum_rows(-_softplus(-gates))
    src_all = gates - pltpu.roll(bh_all, 128 - N_HEADS, 1)
    gates_t, bh_t = gates.T, bh_all.T
    c_old, n_old, m_old = c_scr[...], n_scr[...], m_scr[...]

    hs = range(N_HEADS)
    lane_head = _lane_head(GROUP, HEAD_DIM)
    lane128 = lax.broadcasted_iota(jnp.int32, (lc, 128), 1)
    si = lax.broadcasted_iota(jnp.int32, (lc, lc), 0)
    ti = lax.broadcasted_iota(jnp.int32, (lc, lc), 1)
    causal = si <= ti
    src3 = [t.astype(F32) for t in _split3(src_all)]
    bh3 = [t.astype(F32) for t in _split3(bh_all)]

    def decay_logits(h):
        a, b = lane128 == MISC_IG + h, lane128 == MISC_FG + h
        lhs = jnp.concatenate([jnp.where(a, t, jnp.where(b, 1.0, 0.0)) for t in src3], axis=1)
        rhs = jnp.concatenate([jnp.where(a, 1.0, jnp.where(b, t, 0.0)) for t in bh3], axis=1)
        return _dot_nt(lhs, rhs)

    ig = [gates_t[MISC_IG + h:MISC_IG + h + 1, :] for h in hs]
    bh = [bh_t[MISC_FG + h:MISC_FG + h + 1, :] for h in hs]
    m_prev = [m_old[:, h:h + 1] for h in hs]
    dm = [jnp.where(causal, decay_logits(h), -jnp.inf) for h in hs]
    kq = [_dot_nt(kb, jnp.where(lane_head == h, q, 0.0)) for h in hs]
    m_state = [bh[h] + m_prev[h] for h in hs]
    m_t = [jnp.maximum(m_state[h], jnp.max(dm[h], axis=0, keepdims=True)) for h in hs]
    sc = [kq[h] * jnp.exp(dm[h] - m_t[h]) for h in hs]
    g = [jnp.exp(m_state[h] - m_t[h]) for h in hs]
    row8 = lax.broadcasted_iota(jnp.int32, (8, GROUP), 0)
    qn = _dot_nt(jnp.where(row8 == lane_head, n_old, 0.0), q)
    den = [jnp.sum(sc[h], axis=0, keepdims=True) + g[h] * qn[h:h + 1, :] for h in hs]
    q_c = _dot_nt(c_old, q)
    num = [_dot_tn(vb, sc[h]) for h in hs]
    parts = []
    for h in hs:
        rows = slice(HEAD_DIM * h, HEAD_DIM * (h + 1))
        parts.append((num[h][rows, :] + g[h] * q_c[rows, :])
                     / jnp.maximum(jnp.abs(den[h]), jnp.exp(-m_t[h])))
    hh = jnp.concatenate(parts, axis=0).T
    y_ref[...] = (_head_rms(hh, on_ref[...]) * _sigmoid(o_ref[...])).astype(BF16)

    m_new = [m_t[h][:, lc - 1:lc] for h in hs]
    bh_last = [bh[h][:, lc - 1:lc] for h in hs]
    ws = [jnp.exp(bh_last[h] - bh[h] + ig[h] - m_new[h]) for h in hs]
    gl_row = jnp.zeros((1, GROUP), F32)
    m_new_row = m_old
    ws8 = jnp.zeros((8, lc), F32)
    row8s = lax.broadcasted_iota(jnp.int32, (8, lc), 0)
    for h in hs:
        gl_row = jnp.where(lane_head == h, jnp.exp(bh_last[h] + m_prev[h] - m_new[h]), gl_row)
        m_new_row = jnp.where(lane128[0:1, :] == h, m_new[h], m_new_row)
        ws8 = jnp.where(row8s == h, ws[h], ws8)
    n_mat = _dot(ws8, kb)
    n_upd = jnp.zeros((1, GROUP), F32)
    for h in hs:
        n_upd = jnp.where(lane_head == h, n_mat[h:h + 1, :], n_upd)
    w_rows = jnp.concatenate([jnp.broadcast_to(ws[h], (HEAD_DIM, lc)) for h in hs], axis=0)
    c_new = c_old * gl_row + jnp.where(_same_head(GROUP), _dot(v.T * w_rows, kb), 0.0)
    n_new = gl_row * n_old + n_upd
    c_scr[...] = c_new
    n_scr[...] = n_new
    m_scr[...] = m_new_row

    @pl.when(j == pl.num_programs(1) - 1)
    def _():
        cout_ref[...] = _unpack_heads(c_new)

    nout_ref[...] = n_new
    mout_ref[...] = m_new_row


FFN_CHUNKS = ((0, 1024), (1024, 2048), (2048, FFN_HIDDEN))


def _out_kernel(xp_ref, xs_ref, *refs, n_p):
    yp_refs, ys_refs = refs[0:4], refs[4:8]
    wo_ref, nf_ref, wgu_ref, wd_ref, op_ref, os_ref = refs[8:]
    is_prompt = pl.program_id(0) < n_p
    y = jnp.concatenate([jnp.where(is_prompt, a[...], b[...]) for a, b in zip(yp_refs, ys_refs)], axis=1)
    x1 = jnp.where(is_prompt, xp_ref[...], xs_ref[...]) + jnp.dot(y, wo_ref[...], preferred_element_type=F32)
    h = _rms_rows(x1, nf_ref[...]).astype(BF16)
    ffn = None
    for lo, hi in FFN_CHUNKS:
        g = jnp.dot(h, wgu_ref[:, lo:hi], preferred_element_type=F32)
        u = jnp.dot(h, wgu_ref[:, FFN_HIDDEN + lo:FFN_HIDDEN + hi], preferred_element_type=F32)
        a = (g * _sigmoid(g) * u).astype(BF16)
        d = jnp.dot(a, wd_ref[lo:hi, :], preferred_element_type=F32)
        ffn = d if ffn is None else ffn + d
    out = x1 + ffn

    @pl.when(is_prompt)
    def _():
        op_ref[...] = out

    @pl.when(jnp.logical_not(is_prompt))
    def _():
        os_ref[...] = out


def _out_proj_ffn(x_p, x_s, ys_p, ys_s, lw, layer, tm):
    n_p, n_s = x_p.shape[0] // tm, x_s.shape[0] // tm
    yp_spec, ys_spec = _two_source(n_p, tm, GROUP)
    xp_spec, xs_spec = _two_source(n_p, tm, D_MODEL)
    return pl.pallas_call(
        functools.partial(_out_kernel, n_p=n_p),
        grid=(n_p + n_s,),
        in_specs=[xp_spec, xs_spec, *[yp_spec] * 4, *[ys_spec] * 4,
                  pl.BlockSpec((None, D_MODEL, D_MODEL), lambda i: (layer, 0, 0)),
                  pl.BlockSpec((None, 1, D_MODEL), lambda i: (layer, 0, 0)),
                  pl.BlockSpec((None, D_MODEL, 2 * FFN_HIDDEN), lambda i: (layer, 0, 0)),
                  pl.BlockSpec((None, FFN_HIDDEN, D_MODEL), lambda i: (layer, 0, 0))],
        out_specs=[xp_spec, xs_spec],
        out_shape=[jax.ShapeDtypeStruct(x_p.shape, F32), jax.ShapeDtypeStruct(x_s.shape, F32)],
        compiler_params=pltpu.CompilerParams(dimension_semantics=("arbitrary",),
                                             vmem_limit_bytes=VMEM_LIMIT),
        name="out_proj_ffn",
    )(x_p, x_s, *ys_p, *ys_s, lw['w_out'], lw['norm_ffn'], lw['w_gu'], lw['w_down'])


def _seq_params():
    return pltpu.CompilerParams(dimension_semantics=("arbitrary", "arbitrary"),
                                vmem_limit_bytes=VMEM_LIMIT)


def _row_block(row0, seq, tb):
    base, per_seq = row0 // tb, seq // tb
    return lambda col: (lambda b, j: (base + b * per_seq + j, col))


def _layer_spec(layer, shape):
    return pl.BlockSpec((None,) + shape, lambda b, j: (layer,) + (0,) * len(shape))


def _state_spec(state_layer, shape):
    return pl.BlockSpec((None, None) + shape, lambda b, j: (state_layer, b) + (0,) * len(shape))


def _lru_call(proj, conv0, h0, state_layer, lw, layer, nseq, seq, row0, tb):
    rb, ob = _row_block(row0, seq, tb), _row_block(0, seq, tb)
    wl = functools.partial(_layer_spec, layer)
    st = functools.partial(_state_spec, state_layer)
    out_state = lambda shape: pl.BlockSpec((None,) + shape, lambda b, j: (b,) + (0,) * len(shape))
    return pl.pallas_call(
        _lru_kernel,
        grid=(nseq, seq // tb),
        in_specs=[pl.BlockSpec((tb, GROUP), rb(COL_LRU_X // GROUP)),
                  pl.BlockSpec((tb, GROUP), rb(COL_LRU_G // GROUP)),
                  st((8, GROUP)), st((1, GROUP)),
                  wl((CONV_W, GROUP)), wl((1, GROUP)), wl((GROUP, 2 * GROUP)), wl((1, 2 * GROUP)),
                  wl((1, GROUP)), wl((1, GROUP))],
        out_specs=[pl.BlockSpec((tb, GROUP), ob(0)), out_state((1, GROUP)), out_state((8, GROUP))],
        out_shape=[jax.ShapeDtypeStruct((nseq * seq, GROUP), BF16),
                   jax.ShapeDtypeStruct((nseq, 1, GROUP), F32),
                   jax.ShapeDtypeStruct((nseq, 8, GROUP), F32)],
        scratch_shapes=[pltpu.VMEM((8, GROUP), F32), pltpu.VMEM((1, GROUP), F32)],
        compiler_params=_seq_params(),
        name="rglru",
    )(proj, proj, conv0, h0, lw['conv_w'], lw['conv_b'], lw['lru_wg'], lw['lru_bg'], lw['lru_lambda'], lw['on_a'])


def _mla_pre_call(proj, tabs, lw, layer, tm, nseq_p, seq_p):
    rows, rows_p = proj.shape[0], nseq_p * seq_p
    n_p = rows_p // tm
    pos_blocks = seq_p // tm
    wl = lambda shape: pl.BlockSpec((None,) + shape, lambda i: (layer,) + (0,) * len(shape))
    tab = pl.BlockSpec((tm, 128), lambda i: (jnp.where(i < n_p, i % pos_blocks, pos_blocks + i - n_p), 0))
    ckv_p, ckv_s = _two_source(n_p, tm, KV_RANK)
    _, kr_s = _two_source(n_p, tm, ROPE_DIM)

    def kr_p_map(i):
        t = jnp.minimum(i, n_p - 1)
        return (t // pos_blocks, 0, t % pos_blocks)

    kr_p = pl.BlockSpec((None, ROPE_DIM, tm), kr_p_map)
    return pl.pallas_call(
        functools.partial(_mla_pre_kernel, n_p=n_p),
        grid=(rows // tm,),
        in_specs=[pl.BlockSpec((tm, GROUP), lambda i: (i, COL_Q_LAT // GROUP)),
                  pl.BlockSpec((tm, 128), lambda i: (i, COL_KV_LAT // 128)),
                  pl.BlockSpec((tm, 128), lambda i: (i, COL_MISC // 128)),
                  tab, tab, tab,
                  wl((1, GROUP)), wl((GROUP, 1024)), wl((1, 1024)), wl((1, KV_RANK)), wl((1, 128)),
                  wl((KV_RANK, 1024)), wl((1, 512))],
        out_specs=[pl.BlockSpec((tm, 512), lambda i: (i, 0))] * 3 + [ckv_p, ckv_s, kr_p, kr_s],
        out_shape=[jax.ShapeDtypeStruct((rows, 512), BF16)] * 3
                  + [jax.ShapeDtypeStruct((rows_p, KV_RANK), F32), jax.ShapeDtypeStruct((rows - rows_p, KV_RANK), F32),
                     jax.ShapeDtypeStruct((nseq_p, ROPE_DIM, seq_p), F32),
                     jax.ShapeDtypeStruct((rows - rows_p, ROPE_DIM), F32)],
        compiler_params=pltpu.CompilerParams(dimension_semantics=("arbitrary",),
                                             vmem_limit_bytes=VMEM_LIMIT),
        name="mla_pre",
    )(proj, proj, proj, *tabs, lw['q_norm'], lw['wq'], lw['gq'], lw['kv_norm'], lw['kr_norm'],
      lw['wkv'], lw['gk'])


def _kv_past_call(ckv, krope_t, lw, tr):
    depth, rows, _ = ckv.shape
    per_seq = krope_t.shape[3] // tr
    wl = lambda shape: pl.BlockSpec((None,) + shape, lambda l, i: (l,) + (0,) * len(shape))
    return pl.pallas_call(
        _kv_past_kernel,
        grid=(depth, rows // tr),
        in_specs=[pl.BlockSpec((None, tr, KV_RANK), lambda l, i: (l, i, 0)),
                  pl.BlockSpec((None, None, ROPE_DIM, tr), lambda l, i: (l, i // per_seq, 0, i % per_seq)),
                  wl((KV_RANK, 1024)), wl((1, 512))],
        out_specs=[pl.BlockSpec((None, tr, 512), lambda l, i: (l, i, 0))] * 2,
        out_shape=[jax.ShapeDtypeStruct((depth, rows, 512), BF16)] * 2,
        compiler_params=_seq_params(),
        name="kv_past",
    )(ckv, krope_t, lw['wkv'], lw['gk'])


def _attn_call(q, k, v, k_past, v_past, lw, layer, nseq, seq, row0, tq, tka):
    rb, ob = _row_block(row0, seq, tq), _row_block(0, seq, tq)
    qspec = pl.BlockSpec((tq, 512), rb(0))
    if k_past is None:
        assert row0 == 0
        ka, va = k, v
        past_spec = pl.BlockSpec((seq, 512), lambda b, j: (b, 0))
        n_past_static = None
    else:
        ka, va = k_past, v_past
        past_len = k_past.shape[1] // nseq
        past_spec = pl.BlockSpec((None, past_len, 512), lambda b, j: (layer, b, 0))
        n_past_static = past_len // tka
    return pl.pallas_call(
        functools.partial(_attn_kernel, tka=tka, n_past_static=n_past_static),
        grid=(nseq, seq // tq),
        in_specs=[qspec, past_spec, past_spec, qspec, qspec, _layer_spec(layer, (1, GROUP))],
        out_specs=pl.BlockSpec((tq, GROUP), ob(0)),
        out_shape=jax.ShapeDtypeStruct((nseq * seq, GROUP), BF16),
        scratch_shapes=[pltpu.VMEM((N_HEADS, 1, tq), F32), pltpu.VMEM((N_HEADS, 128, tq), F32)],
        compiler_params=_seq_params(),
        name="mla_attn",
    )(q, ka, va, k, v, lw['on_b'])


def _ret_call(proj, tabs, tab_map, s0, state_layer, lw, layer, nseq, seq, row0, lc):
    rb, ob = _row_block(row0, seq, lc), _row_block(0, seq, lc)
    blk = lambda col: pl.BlockSpec((lc, GROUP), rb(col // GROUP))
    tab = pl.BlockSpec((lc, 128), tab_map)
    return pl.pallas_call(
        _ret_kernel,
        grid=(nseq, seq // lc),
        in_specs=[blk(COL_R_Q), blk(COL_R_K), blk(COL_R_V), blk(COL_R_G), tab, tab, tab,
                  _state_spec(state_layer, (GROUP, HEAD_DIM)), _layer_spec(layer, (1, GROUP))],
        out_specs=[pl.BlockSpec((lc, GROUP), ob(0)),
                   pl.BlockSpec((None, GROUP, HEAD_DIM), lambda b, j: (b, 0, 0))],
        out_shape=[jax.ShapeDtypeStruct((nseq * seq, GROUP), BF16),
                   jax.ShapeDtypeStruct((nseq, GROUP, HEAD_DIM), F32)],
        scratch_shapes=[pltpu.VMEM((GROUP, GROUP), F32), pltpu.VMEM((N_HEADS, lc, lc), F32),
                        pltpu.VMEM((lc, GROUP), F32), pltpu.VMEM((lc, GROUP), F32)],
        compiler_params=_seq_params(),
        name="retention",
    )(proj, proj, proj, proj, *tabs, s0, lw['on_c'])


def _mlstm_call(proj, c0, n0, m0, state_layer, lw, layer, nseq, seq, row0, lc):
    rb, ob = _row_block(row0, seq, lc), _row_block(0, seq, lc)
    blk = lambda col: pl.BlockSpec((lc, GROUP), rb(col // GROUP))
    st = functools.partial(_state_spec, state_layer)
    out_state = lambda shape: pl.BlockSpec((None,) + shape, lambda b, j: (b,) + (0,) * len(shape))
    return pl.pallas_call(
        _mlstm_kernel,
        grid=(nseq, seq // lc),
        in_specs=[blk(COL_M_Q), blk(COL_M_K), blk(COL_M_V), blk(COL_M_O),
                  pl.BlockSpec((lc, 128), rb(COL_MISC // 128)), _layer_spec(layer, (1, 128)),
                  st((GROUP, HEAD_DIM)), st((1, GROUP)), st((1, 128)), _layer_spec(layer, (1, GROUP))],
        out_specs=[pl.BlockSpec((lc, GROUP), ob(0)), out_state((GROUP, HEAD_DIM)), out_state((1, GROUP)),
                   out_state((1, 128))],
        out_shape=[jax.ShapeDtypeStruct((nseq * seq, GROUP), BF16),
                   jax.ShapeDtypeStruct((nseq, GROUP, HEAD_DIM), F32),
                   jax.ShapeDtypeStruct((nseq, 1, GROUP), F32),
                   jax.ShapeDtypeStruct((nseq, 1, 128), F32)],
        scratch_shapes=[pltpu.VMEM((GROUP, GROUP), F32), pltpu.VMEM((1, GROUP), F32),
                        pltpu.VMEM((1, 128), F32)],
        compiler_params=_seq_params(),
        name="mlstm",
    )(proj, proj, proj, proj, proj, lw['b_if'], c0, n0, m0, lw['on_d'])


def _block_diag(s):
    h, d, e = s.shape[-3:]
    eye = jnp.eye(h, dtype=s.dtype)
    return (s[..., :, :, None, :] * eye[:, None, :, None]).reshape(s.shape[:-3] + (h * d, h * e))


def _rope_tables(pos, half, lanes, lo):
    inv = ROPE_THETA ** (-jnp.arange(half, dtype=F32) / half)
    ang = pos[:, None] * inv[None, :]
    cos, sin = jnp.cos(ang), jnp.sin(ang)
    n = pos.shape[0]
    c = jnp.ones((n, lanes), F32).at[:, lo:lo + 2 * half].set(jnp.concatenate([cos, cos], axis=1))
    sp = jnp.zeros((n, lanes), F32).at[:, lo + half:lo + 2 * half].set(sin)
    sm = jnp.zeros((n, lanes), F32).at[:, lo:lo + half].set(-sin)
    return c, sp, sm


def _prep_weights(norm_mix, w_in, lru_conv_w, lru_conv_b, lru_wa, lru_ba, lru_wx, lru_bx, lru_lambda,
                  mla_q_norm, mla_wq_b, mla_qn_norm, mla_qr_norm, mla_kv_norm, mla_kr_norm, mla_wkv_b,
                  mla_kn_norm, mlstm_b_if, out_norm, w_out, norm_ffn, w_gu, w_down):
    depth = w_in.shape[0]
    row = lambda a: a.reshape(depth, 1, -1)
    n_if = 2 * N_HEADS
    w_in_pad = _w_in_prep(jnp.swapaxes(w_in, 1, 2))

    lru_wg = jnp.concatenate([_block_diag(lru_wa), _block_diag(lru_wx)], axis=2).astype(BF16)
    lru_bg = jnp.concatenate([lru_ba, lru_bx], axis=1)

    wq = mla_wq_b.reshape(depth, GROUP, N_HEADS, HEAD_DIM + ROPE_DIM)
    wq = jnp.pad(wq, ((0, 0), (0, 0), (0, 0), (0, 128 - HEAD_DIM - ROPE_DIM))).reshape(depth, GROUP, 512)
    gq = jnp.concatenate([mla_qn_norm, mla_qr_norm, jnp.zeros((depth, 32), F32)], axis=1)
    gq = jnp.tile(gq, (1, N_HEADS))
    lane = np.arange(512)
    in_rope = (lane % 128 >= HEAD_DIM) & (lane % 128 < HEAD_DIM + ROPE_DIM)
    first_half = (lane % 128 - HEAD_DIM) < ROPE_DIM // 2
    swap = np.where(in_rope, np.where(first_half, lane + ROPE_DIM // 2, lane - ROPE_DIM // 2), lane)
    wq = jnp.concatenate([wq, wq[:, :, swap]], axis=2)
    gq = jnp.concatenate([gq, gq[:, swap]], axis=1)
    wkv = mla_wkv_b.reshape(depth, KV_RANK, N_HEADS, 2 * HEAD_DIM)
    pad_head = lambda w: jnp.pad(w, ((0, 0), (0, 0), (0, 0), (0, 128 - HEAD_DIM))).reshape(depth, KV_RANK, 512)
    wkv = jnp.concatenate([pad_head(wkv[..., :HEAD_DIM]), pad_head(wkv[..., HEAD_DIM:])], axis=2)
    gk = jnp.tile(jnp.concatenate([mla_kn_norm, jnp.zeros((depth, 128 - HEAD_DIM), F32)], axis=1), (1, N_HEADS))
    kr_norm = jnp.pad(mla_kr_norm, ((0, 0), (0, 128 - ROPE_DIM)))
    b_if = jnp.pad(mlstm_b_if, ((0, 0), (MISC_IG, 128 - MISC_IG - n_if)))

    return dict(
        norm_mix=row(norm_mix), w_in=w_in_pad,
        conv_w=lru_conv_w, conv_b=row(lru_conv_b), lru_wg=lru_wg, lru_bg=row(lru_bg), lru_lambda=row(lru_lambda),
        q_norm=row(mla_q_norm), wq=wq.astype(BF16), gq=row(gq), kv_norm=row(mla_kv_norm), kr_norm=row(kr_norm),
        wkv=wkv.astype(BF16), gk=row(gk), b_if=row(b_if),
        on_a=row(out_norm[:, :GROUP]), on_b=row(out_norm[:, GROUP:2 * GROUP]),
        on_c=row(out_norm[:, 2 * GROUP:3 * GROUP]), on_d=row(out_norm[:, 3 * GROUP:]),
        w_out=w_out.astype(BF16), norm_ffn=row(norm_ffn), w_gu=w_gu.astype(BF16), w_down=w_down.astype(BF16))


def _tile_rows(rows, cap):
    t = cap
    while rows % t:
        t //= 2
    return t


def kernel(x_prompt, x_sample, cache_mla_ckv, cache_mla_krope, state_lru_h, state_lru_conv, state_ret, state_mlstm_C, state_mlstm_n, state_mlstm_m, norm_mix, w_in, lru_conv_w, lru_conv_b, lru_wa, lru_ba, lru_wx, lru_bx, lru_lambda, mla_q_norm, mla_wq_b, mla_qn_norm, mla_qr_norm, mla_kv_norm, mla_kr_norm, mla_wkv_b, mla_kn_norm, mlstm_b_if, out_norm, w_out, norm_ffn, w_gu, w_down):
    bp, tp, _ = x_prompt.shape
    bs, ts, _ = x_sample.shape
    depth, _, past, _ = cache_mla_ckv.shape
    rows_p, rows_s = bp * tp, bs * ts
    tm = math.gcd(_tile_rows(tp, 512), rows_s)
    tb_p, tb_s = min(tp, 512), min(ts, 512)
    lc_p, lc_s = min(tp, 256), min(ts, 256)
    tq_p, tq_s = min(tp, 512), min(ts, 512)
    tka_s = past
    assert past % CHUNK == 0 and ts % CHUNK == 0 and tp % tm == 0 and rows_s % tm == 0

    lw = _prep_weights(norm_mix, w_in, lru_conv_w, lru_conv_b, lru_wa, lru_ba, lru_wx, lru_bx, lru_lambda,
                       mla_q_norm, mla_wq_b, mla_qn_norm, mla_qr_norm, mla_kv_norm, mla_kr_norm, mla_wkv_b,
                       mla_kn_norm, mlstm_b_if, out_norm, w_out, norm_ffn, w_gu, w_down)

    pos = jnp.concatenate([jnp.arange(tp, dtype=F32), jnp.tile(jnp.arange(past, past + ts, dtype=F32), bs)])
    tabs_mla = _rope_tables(pos, ROPE_DIM // 2, 128, HEAD_DIM)
    tabs_ret = tuple(jnp.concatenate([t, t], axis=1) for t in _rope_tables(pos, HEAD_DIM // 2, HEAD_DIM, 0))

    k_past, v_past = _kv_past_call(cache_mla_ckv.reshape(depth, bs * past, KV_RANK),
                                   jnp.swapaxes(cache_mla_krope, 2, 3), lw,
                                   _tile_rows(past, 1024))

    zeros = lambda *shape: jnp.zeros((1, bp) + shape, F32)
    st_p = dict(conv=zeros(8, GROUP), h=zeros(1, GROUP), s=zeros(GROUP, HEAD_DIM), c=zeros(GROUP, HEAD_DIM),
                n=zeros(1, GROUP), m=zeros(1, 128))
    st_s = dict(conv=jnp.pad(state_lru_conv, ((0, 0), (0, 0), (8 - (CONV_W - 1), 0), (0, 0))),
                h=state_lru_h[:, :, None, :], s=state_ret.reshape(depth, bs, GROUP, HEAD_DIM),
                c=state_mlstm_C.reshape(depth, bs, GROUP, HEAD_DIM),
                n=state_mlstm_n.reshape(depth, bs, 1, GROUP),
                m=jnp.pad(state_mlstm_m, ((0, 0), (0, 0), (0, 128 - N_HEADS)))[:, :, None, :])

    x_p = x_prompt.reshape(rows_p, D_MODEL)
    x_s = x_sample.reshape(rows_s, D_MODEL)
    acc = {name: [] for name in ('ckv_p', 'ckv_s', 'kr_p', 'kr_s', 'h_p', 'h_s', 'conv_p', 'conv_s',
                                 's_p', 's_s', 'c_p', 'c_s', 'n_p', 'n_s', 'm_p', 'm_s')}
    for l in range(depth):
        proj = _in_proj(x_p, x_s, lw, l, tm)
        ya_p, h_p, conv_p = _lru_call(proj, st_p['conv'], st_p['h'], 0, lw, l, bp, tp, 0, tb_p)
        ya_s, h_s, conv_s = _lru_call(proj, st_s['conv'], st_s['h'], l, lw, l, bs, ts, rows_p, tb_s)
        q, k, v, ckv_p, ckv_s, kr_p, kr_s = _mla_pre_call(proj, tabs_mla, lw, l, tm, bp, tp)
        yb_p = _attn_call(q, k, v, None, None, lw, l, bp, tp, 0, tq_p, tq_p)
        yb_s = _attn_call(q, k, v, k_past, v_past, lw, l, bs, ts, rows_p, tq_s, tka_s)
        yc_p, s_p = _ret_call(proj, tabs_ret, lambda b, j: (j, 0), st_p['s'], 0, lw, l, bp, tp, 0, lc_p)
        yc_s, s_s = _ret_call(proj, tabs_ret, _row_block(tp, ts, lc_s)(0), st_s['s'], l, lw, l, bs, ts, rows_p, lc_s)
        yd_p, c_p, n_p, m_p = _mlstm_call(proj, st_p['c'], st_p['n'], st_p['m'], 0, lw, l, bp, tp, 0, lc_p)
        yd_s, c_s, n_s, m_s = _mlstm_call(proj, st_s['c'], st_s['n'], st_s['m'], l, lw, l, bs, ts, rows_p, lc_s)
        x_p, x_s = _out_proj_ffn(x_p, x_s, (ya_p, yb_p, yc_p, yd_p), (ya_s, yb_s, yc_s, yd_s), lw, l, tm)
        for name, val in (('ckv_p', ckv_p), ('ckv_s', ckv_s), ('kr_p', kr_p), ('kr_s', kr_s), ('h_p', h_p),
                          ('h_s', h_s), ('conv_p', conv_p), ('conv_s', conv_s), ('s_p', s_p), ('s_s', s_s),
                          ('c_p', c_p), ('c_s', c_s), ('n_p', n_p), ('n_s', n_s), ('m_p', m_p), ('m_s', m_s)):
            acc[name].append(val)

    st = {name: jnp.stack(vals) for name, vals in acc.items()}

    def outputs(tag, b, t):
        krope = jnp.swapaxes(st['kr_p'], 2, 3) if tag == 'p' else st['kr_s'].reshape(depth, b, t, ROPE_DIM)
        return (st['ckv_' + tag].reshape(depth, b, t, KV_RANK), krope,
                st['h_' + tag][:, :, 0], st['conv_' + tag][:, :, 8 - (CONV_W - 1):],
                st['s_' + tag].reshape(depth, b, N_HEADS, HEAD_DIM, HEAD_DIM),
                st['c_' + tag].reshape(depth, b, N_HEADS, HEAD_DIM, HEAD_DIM),
                st['n_' + tag].reshape(depth, b, N_HEADS, HEAD_DIM), st['m_' + tag][:, :, 0, :N_HEADS])

    return ((x_p.reshape(bp, tp, D_MODEL), x_s.reshape(bs, ts, D_MODEL))
            + outputs('p', bp, tp) + outputs('s', bs, ts))
```

```python
import functools
import math

import jax
import jax.numpy as jnp
import numpy as np
from jax import lax
from jax.experimental import pallas as pl
from jax.experimental.pallas import tpu as pltpu

F32 = jnp.float32
BF16 = jnp.bfloat16

D_MODEL = 1024
CHUNK = 64
HEAD_DIM = 64
GROUP = 256
N_HEADS = 4
RMS_EPS = 1e-6
ROPE_THETA = 10000.0
CONV_W = 4
LRU_C = 8.0
KV_RANK = 128
ROPE_DIM = 32
FFN_HIDDEN = 2816
IN_PAD = 3072

COL_LRU_X, COL_LRU_G, COL_Q_LAT = 0, 256, 512
COL_KV_LAT, COL_MISC = 768, 896
COL_R_Q, COL_R_K, COL_R_V, COL_R_G = 1024, 1280, 1536, 1792
COL_M_Q, COL_M_K, COL_M_V, COL_M_O = 2048, 2304, 2560, 2816
MISC_IG = 32
MISC_FG = MISC_IG + N_HEADS
ATT_SCALE = (HEAD_DIM + ROPE_DIM) ** -0.5
RET_LOG_DECAY = tuple(math.log(1.0 - 2.0 ** (-5.0 - h)) for h in range(N_HEADS))

VMEM_LIMIT = 56 * 1024 * 1024


def _dot(a, b):
    return jnp.dot(a.astype(BF16), b.astype(BF16), preferred_element_type=F32)


def _dot_nt(a, b):
    return lax.dot_general(a.astype(BF16), b.astype(BF16), (((1,), (1,)), ((), ())),
                           preferred_element_type=F32)


def _dot_tn(a, b):
    return lax.dot_general(a.astype(BF16), b.astype(BF16), (((0,), (0,)), ((), ())),
                           preferred_element_type=F32)


def _split3(x):
    hi = x.astype(BF16)
    r1 = x - hi.astype(F32)
    mid = r1.astype(BF16)
    lo = (r1 - mid.astype(F32)).astype(BF16)
    return hi, mid, lo


def _dot_f32_lhs(x, m, terms=2):
    parts = [jnp.dot(t, m, preferred_element_type=F32) for t in _split3(x)[:terms]]
    return functools.reduce(lambda a, b: a + b, parts)


def _pack_heads(s):
    r = lax.broadcasted_iota(jnp.int32, (HEAD_DIM, GROUP), 0)
    c = lax.broadcasted_iota(jnp.int32, (HEAD_DIM, GROUP), 1)
    tile = jnp.where(c % HEAD_DIM == r, 1.0, 0.0).astype(BF16)
    return jnp.where(_same_head(GROUP), _dot_f32_lhs(s, tile, terms=3), 0.0)


def _unpack_heads(s):
    r = lax.broadcasted_iota(jnp.int32, (GROUP, HEAD_DIM), 0)
    c = lax.broadcasted_iota(jnp.int32, (GROUP, HEAD_DIM), 1)
    fold = jnp.where(r % HEAD_DIM == c, 1.0, 0.0).astype(BF16)
    return _dot_f32_lhs(s, fold, terms=3)


def _rms_rows(x, g):
    return x * lax.rsqrt(jnp.mean(x * x, axis=-1, keepdims=True) + RMS_EPS) * g


def _lane_head(width, head_width):
    return lax.broadcasted_iota(jnp.int32, (1, width), 1) // head_width


def _same_head(n):
    r = lax.broadcasted_iota(jnp.int32, (n, n), 0) // HEAD_DIM
    c = lax.broadcasted_iota(jnp.int32, (n, n), 1) // HEAD_DIM
    return r == c


def _head_rms(y, g):
    mean_mat = jnp.where(_same_head(GROUP), 1.0 / HEAD_DIM, 0.0).astype(BF16)
    return y * lax.rsqrt(_dot_f32_lhs(y * y, mean_mat) + RMS_EPS) * g


def _shift_rows(x, d, fill):
    rows = lax.broadcasted_iota(jnp.int32, x.shape, 0)
    return jnp.where(rows >= d, pltpu.roll(x, d, 0), fill)


def _cumsum_rows(x):
    d = 1
    while d < x.shape[0]:
        x = x + _shift_rows(x, d, 0.0)
        d *= 2
    return x


def _softplus(z):
    return jnp.maximum(z, 0.0) + jnp.log1p(jnp.exp(-jnp.abs(z)))


def _sigmoid(z):
    return 0.5 * jnp.tanh(0.5 * z) + 0.5


def _gelu_tanh(z):
    return 0.5 * z * (1.0 + jnp.tanh(math.sqrt(2.0 / math.pi) * (z + 0.044715 * (z * z * z))))


def _rope_lanes(x, c, sp, sm, half):
    w = x.shape[1]
    return x * c + pltpu.roll(x, half, 1) * sp + pltpu.roll(x, w - half, 1) * sm


def _tile_lanes(t, n):
    return jnp.concatenate([t] * n, axis=1) if n > 1 else t


def _w_in_prep_kernel(wt_ref, o_ref):
    n_in, cols = wt_ref.shape
    n_if = 2 * N_HEADS
    split = COL_MISC + ROPE_DIM
    o_ref[:, :COL_MISC] = wt_ref[0:COL_MISC, :].T.astype(BF16)
    misc = jnp.concatenate([wt_ref[COL_MISC:split, :], wt_ref[n_in - n_if:n_in, :],
                            jnp.zeros((COL_R_Q - split - n_if, cols), F32)], axis=0)
    o_ref[:, COL_MISC:COL_R_Q] = misc.T.astype(BF16)
    o_ref[:, COL_R_Q:] = wt_ref[split:n_in - n_if, :].T.astype(BF16)


def _w_in_prep(w_in_t, cols=256):
    depth, n_in, d_model = w_in_t.shape
    return pl.pallas_call(
        _w_in_prep_kernel,
        grid=(depth, d_model // cols),
        in_specs=[pl.BlockSpec((None, n_in, cols), lambda l, i: (l, 0, i))],
        out_specs=pl.BlockSpec((None, cols, IN_PAD), lambda l, i: (l, i, 0)),
        out_shape=jax.ShapeDtypeStruct((depth, d_model, IN_PAD), BF16),
        compiler_params=_seq_params(),
        name="w_in_prep",
    )(w_in_t)


def _in_proj_kernel(xp_ref, xs_ref, g_ref, w_ref, op_ref, os_ref, *, n_p):
    is_prompt = pl.program_id(0) < n_p
    h = _rms_rows(jnp.where(is_prompt, xp_ref[...], xs_ref[...]), g_ref[...]).astype(BF16)

    @pl.when(is_prompt)
    def _():
        op_ref[...] = jnp.dot(h, w_ref[...], preferred_element_type=F32)

    @pl.when(jnp.logical_not(is_prompt))
    def _():
        os_ref[...] = jnp.dot(h, w_ref[...], preferred_element_type=F32)


def _two_source(n_p, tm, width):
    return (pl.BlockSpec((tm, width), lambda i: (jnp.minimum(i, n_p - 1), 0)),
            pl.BlockSpec((tm, width), lambda i: (jnp.maximum(i - n_p, 0), 0)))


def _in_proj(x_p, x_s, lw, layer, tm):
    n_p, n_s = x_p.shape[0] // tm, x_s.shape[0] // tm
    return pl.pallas_call(
        functools.partial(_in_proj_kernel, n_p=n_p),
        grid=(n_p + n_s,),
        in_specs=[*_two_source(n_p, tm, D_MODEL),
                  pl.BlockSpec((None, 1, D_MODEL), lambda i: (layer, 0, 0)),
                  pl.BlockSpec((None, D_MODEL, IN_PAD), lambda i: (layer, 0, 0))],
        out_specs=list(_two_source(n_p, tm, IN_PAD)),
        out_shape=[jax.ShapeDtypeStruct((x_p.shape[0], IN_PAD), F32),
                   jax.ShapeDtypeStruct((x_s.shape[0], IN_PAD), F32)],
        compiler_params=pltpu.CompilerParams(dimension_semantics=("arbitrary",),
                                             vmem_limit_bytes=VMEM_LIMIT),
        name="in_proj",
    )(x_p, x_s, lw['norm_mix'], lw['w_in'])


def _lru_kernel(x_ref, g_ref, conv0_ref, h0_ref, cw_ref, cb_ref, wg_ref, bg_ref, lam_ref, on_ref,
                y_ref, hout_ref, convout_ref, xp_scr, h_scr):
    j = pl.program_id(1)
    tb = x_ref.shape[0]

    @pl.when(j == 0)
    def _():
        xp_scr[...] = conv0_ref[...]
        h_scr[...] = h0_ref[...]

    x = x_ref[...]
    prev = xp_scr[...]
    row8 = lax.broadcasted_iota(jnp.int32, (8, GROUP), 0)

    def delayed(d):
        r = pltpu.roll(x, d, 0)
        head = jnp.where(row8 < d, pltpu.roll(prev, d, 0), r[0:8, :])
        return jnp.concatenate([head, r[8:, :]], axis=0)

    cw = cw_ref[...]
    xc = cb_ref[...] + cw[3:4] * x + cw[2:3] * delayed(1) + cw[1:2] * delayed(2) + cw[0:1] * delayed(3)
    tail = x[tb - 8:tb, :]
    xp_scr[...] = tail
    convout_ref[...] = tail

    gates = _dot(xc, wg_ref[...]) + bg_ref[...]
    r = _sigmoid(gates[:, :GROUP])
    i = _sigmoid(gates[:, GROUP:])
    log_a = -LRU_C * r * _softplus(-lam_ref[...])
    a = jnp.exp(log_a)
    u = jnp.sqrt(-jnp.tanh(log_a) * (a * a + 1.0)) * i * xc

    d = 1
    while d < tb:
        u = a * _shift_rows(u, d, 0.0) + u
        a = a * _shift_rows(a, d, 1.0)
        d *= 2
    h = u + a * h_scr[...]
    h_last = h[tb - 1:tb, :]
    h_scr[...] = h_last
    hout_ref[...] = h_last
    y_ref[...] = (_head_rms(h, on_ref[...]) * _gelu_tanh(g_ref[...])).astype(BF16)


def _mla_seg_rms(x, g, rope):
    if rope:
        r = lax.broadcasted_iota(jnp.int32, (256, 256), 0)
        c = lax.broadcasted_iota(jnp.int32, (256, 256), 1)
        same = (r // 128) == (c // 128)
        rl, cl = r % 128, c % 128
        in_nope = same & (rl < HEAD_DIM) & (cl < HEAD_DIM)
        in_rope = same & (rl >= HEAD_DIM) & (cl >= HEAD_DIM)
        seg = jnp.where(in_nope, 1.0 / HEAD_DIM, jnp.where(in_rope, 1.0 / ROPE_DIM, 0.0)).astype(BF16)
        ms = jnp.concatenate([_dot_f32_lhs(jnp.square(x[:, 256 * p:256 * p + 256]), seg)
                              for p in range(N_HEADS // 2)], axis=1)
        return lax.rsqrt(ms + RMS_EPS) if g is None else x * lax.rsqrt(ms + RMS_EPS) * g
    outs = []
    for h in range(N_HEADS):
        xh = x[:, 128 * h:128 * h + 128]
        outs.append(xh * lax.rsqrt(jnp.sum(xh * xh, axis=-1, keepdims=True) * (1.0 / HEAD_DIM) + RMS_EPS))
    return jnp.concatenate(outs, axis=1) * g


def _with_ones_lane(v):
    lane = lax.broadcasted_iota(jnp.int32, v.shape, 1) % 128
    return jnp.where(lane == HEAD_DIM, 1.0, v)


def _mla_pre_kernel(qlat_p, qlat_s, kvlat_p, kvlat_s, misc_p, misc_s, c_ref, sp_ref, sm_ref,
                    qn_ref, wq_ref, gq_ref, kvn_ref, krn_ref, wkv_ref, gk_ref,
                    q_out, k_out, v_out, ckv_p, ckv_s, kr_p, kr_s, *, n_p):
    c, sp, sm = c_ref[...], sp_ref[...], sm_ref[...]
    half = ROPE_DIM // 2
    is_prompt = pl.program_id(0) < n_p
    qlat = jnp.where(is_prompt, qlat_p[...], qlat_s[...])
    kvlat = jnp.where(is_prompt, kvlat_p[...], kvlat_s[...])
    misc = jnp.where(is_prompt, misc_p[...], misc_s[...])

    qboth = _dot(_rms_rows(qlat, qn_ref[...]), wq_ref[...])
    qraw, qswap = qboth[:, :512], qboth[:, 512:]
    gq, gq_swap = gq_ref[:, :512], gq_ref[:, 512:]
    inv = _mla_seg_rms(qraw, None, rope=True)
    q = inv * (qraw * (gq * _tile_lanes(c, 4)) + qswap * (gq_swap * _tile_lanes(sp + sm, 4)))
    q_out[...] = q.astype(BF16)

    ckv = _rms_rows(kvlat, kvn_ref[...])
    lane = lax.broadcasted_iota(jnp.int32, misc.shape, 1)
    kr = jnp.where(lane < ROPE_DIM, misc, 0.0)
    kr = kr * lax.rsqrt(jnp.sum(kr * kr, axis=-1, keepdims=True) * (1.0 / ROPE_DIM) + RMS_EPS) * krn_ref[...]
    kr = _rope_lanes(pltpu.roll(kr, HEAD_DIM, 1), c, sp, sm, half)
    kr_new = pltpu.roll(kr, 128 - HEAD_DIM, 1)

    @pl.when(is_prompt)
    def _():
        ckv_p[...] = ckv
        kr_p[...] = kr_new.T[:ROPE_DIM, :]

    @pl.when(jnp.logical_not(is_prompt))
    def _():
        ckv_s[...] = ckv
        kr_s[...] = kr_new[:, :ROPE_DIM]

    kv = _dot(ckv, wkv_ref[...])
    kn = _mla_seg_rms(kv[:, :512], gk_ref[...], rope=False)
    k_out[...] = (kn + _tile_lanes(kr, 4)).astype(BF16)
    v_out[...] = _with_ones_lane(kv[:, 512:]).astype(BF16)


def _kv_past_kernel(ckv_ref, kr_ref, wkv_ref, gk_ref, k_out, v_out):
    kv = _dot(ckv_ref[...], wkv_ref[...])
    kn = _mla_seg_rms(kv[:, :512], gk_ref[...], rope=False)
    r = lax.broadcasted_iota(jnp.int32, (ROPE_DIM, 128), 0)
    cidx = lax.broadcasted_iota(jnp.int32, (ROPE_DIM, 128), 1)
    place = jnp.where(cidx == r + HEAD_DIM, 1.0, 0.0).astype(BF16)
    kr = _dot_tn(kr_ref[...], place)
    k_out[...] = (kn + _tile_lanes(kr, 4)).astype(BF16)
    v_out[...] = _with_ones_lane(kv[:, 512:]).astype(BF16)


def _attn_kernel(q_ref, ka_ref, va_ref, kb_ref, vb_ref, on_ref, y_ref, m_scr, acc_scr, *, tka, n_past_static):
    j = pl.program_id(1)
    tq = q_ref.shape[0]
    n_past = j * (tq // tka) if n_past_static is None else n_past_static
    key_c = lax.broadcasted_iota(jnp.int32, (tq, tq), 0) // CHUNK
    qry_c = lax.broadcasted_iota(jnp.int32, (tq, tq), 1) // CHUNK
    visible = key_c <= qry_c
    heads = [slice(128 * h, 128 * h + 128) for h in range(N_HEADS)]
    c = ATT_SCALE * math.log2(math.e)

    s = [jnp.where(visible, _dot_nt(kb_ref[:, hs], q_ref[:, hs]) * c, -jnp.inf) for hs in heads]
    m = [jnp.max(s[h], axis=0, keepdims=True) for h in range(N_HEADS)]
    p = [jnp.exp2(s[h] - m[h]).astype(BF16) for h in range(N_HEADS)]
    pv = [_dot_tn(vb_ref[:, hs], p[h]) for h, hs in enumerate(heads)]
    for h in range(N_HEADS):
        m_scr[h] = m[h]
        acc_scr[h] = pv[h]

    def body(t, carry):
        off = pl.multiple_of(t * tka, tka)
        m_old = [m_scr[h] for h in range(N_HEADS)]
        s = [_dot_nt(ka_ref[pl.ds(off, tka), hs], q_ref[:, hs]) * c for hs in heads]
        m_new = [jnp.maximum(m_old[h], jnp.max(s[h], axis=0, keepdims=True)) for h in range(N_HEADS)]
        p = [jnp.exp2(s[h] - m_new[h]).astype(BF16) for h in range(N_HEADS)]
        pv = [_dot_tn(va_ref[pl.ds(off, tka), hs], p[h]) for h, hs in enumerate(heads)]
        for h in range(N_HEADS):
            acc_scr[h] = jnp.exp2(m_old[h] - m_new[h]) * acc_scr[h] + pv[h]
            m_scr[h] = m_new[h]
        return carry

    lax.fori_loop(0, n_past, body, 0)
    outs = []
    for h in range(N_HEADS):
        a = acc_scr[h]
        outs.append(a[:HEAD_DIM, :] / a[HEAD_DIM:HEAD_DIM + 1, :])
    o = jnp.concatenate(outs, axis=0).T
    y_ref[...] = _head_rms(o, on_ref[...]).astype(BF16)


def _ret_kernel(q_ref, k_ref, v_ref, g_ref, c_ref, sp_ref, sm_ref, s0_ref, on_ref,
                y_ref, sout_ref, s_scr, dmat_scr, cross_scr, tail_scr):
    nb, lc, _ = q_ref.shape
    seqs, hs = range(nb), range(N_HEADS)
    b, j = pl.program_id(0), pl.program_id(1)
    lane_head = _lane_head(GROUP, HEAD_DIM)
    lg_lane = jnp.zeros((1, GROUP), F32)
    for h in hs:
        lg_lane = jnp.where(lane_head == h, RET_LOG_DECAY[h], lg_lane)

    @pl.when((b == 0) & (j == 0))
    def _():
        t_col = lax.broadcasted_iota(jnp.int32, (lc, 1), 0).astype(F32)
        cross_scr[...] = jnp.exp((t_col + 1.0) * lg_lane)
        tail_scr[...] = jnp.exp((lc - 1.0 - t_col) * lg_lane)
        ti = lax.broadcasted_iota(jnp.int32, (lc, lc), 0)
        si = lax.broadcasted_iota(jnp.int32, (lc, lc), 1)
        causal = ti >= si
        diff = jnp.where(causal, ti - si, 0).astype(F32)
        for h in hs:
            dmat_scr[h] = jnp.where(causal, jnp.exp(diff * RET_LOG_DECAY[h]), 0.0)

    @pl.when(j == 0)
    def _():
        for s in seqs:
            s_scr[s] = _pack_heads(s0_ref[s])

    c, sp, sm = (_tile_lanes(t[...], 2) for t in (c_ref, sp_ref, sm_ref))
    half = HEAD_DIM // 2
    q = [_rope_lanes(q_ref[s], c, sp, sm, half) for s in seqs]
    k = [_rope_lanes(k_ref[s], c, sp, sm, half) * (HEAD_DIM ** -0.5) for s in seqs]
    v = [v_ref[s].astype(BF16) for s in seqs]
    kb = [k[s].astype(BF16) for s in seqs]
    s_old = [s_scr[s] for s in seqs]
    y = [_dot(q[s], s_old[s]) * cross_scr[...] for s in seqs]
    att = [[_dot_nt(jnp.where(lane_head == h, q[s], 0.0), kb[s]) for h in hs] for s in seqs]
    att = [[(att[s][h] * dmat_scr[h]).astype(BF16) for h in hs] for s in seqs]
    yh = [[_dot(att[s][h], v[s]) for h in hs] for s in seqs]
    s_upd = [_dot_tn(k[s] * tail_scr[...], v[s]) for s in seqs]
    for s in seqs:
        for h in hs:
            y[s] = y[s] + jnp.where(lane_head == h, yh[s][h], 0.0)
        s_scr[s] = s_old[s] * jnp.exp(float(lc) * lg_lane) + jnp.where(_same_head(GROUP), s_upd[s], 0.0)
        g = g_ref[s]
        y_ref[s] = (_head_rms(y[s], on_ref[...]) * (g * _sigmoid(g))).astype(BF16)

    @pl.when(j == pl.num_programs(1) - 1)
    def _():
        for s in seqs:
            sout_ref[s] = _unpack_heads(s_scr[s])


def _mlstm_kernel(q_ref, k_ref, v_ref, o_ref, misc_ref, bif_ref, c0_ref, n0_ref, m0_ref, on_ref,
                  y_ref, cout_ref, nout_ref, mout_ref, c_scr, n_scr, m_scr):
    nb, lc, _ = q_ref.shape
    seqs, hs = range(nb), range(N_HEADS)
    pairs = [(s, h) for s in seqs for h in hs]
    j = pl.program_id(1)

    @pl.when(j == 0)
    def _():
        for s in seqs:
            c_scr[s] = _pack_heads(c0_ref[s])
            n_scr[s] = n0_ref[s]
            m_scr[s] = m0_ref[s]

    lane_head = _lane_head(GROUP, HEAD_DIM)
    lane128 = lax.broadcasted_iota(jnp.int32, (lc, 128), 1)
    si = lax.broadcasted_iota(jnp.int32, (lc, lc), 0)
    ti = lax.broadcasted_iota(jnp.int32, (lc, lc), 1)
    causal = si <= ti
    row8 = lax.broadcasted_iota(jnp.int32, (8, GROUP), 0)
    row8s = lax.broadcasted_iota(jnp.int32, (8, lc), 0)

    q = [q_ref[s] for s in seqs]
    kb = [(k_ref[s] * (HEAD_DIM ** -0.5)).astype(BF16) for s in seqs]
    v = [v_ref[s] for s in seqs]
    vb = [v[s].astype(BF16) for s in seqs]
    gates = [misc_ref[s] + bif_ref[...] for s in seqs]
    bh_all = [_cumsum_rows(-_softplus(-gates[s])) for s in seqs]
    src_all = [gates[s] - pltpu.roll(bh_all[s], 128 - N_HEADS, 1) for s in seqs]
    gates_t = [gates[s].T for s in seqs]
    bh_t = [bh_all[s].T for s in seqs]
    c_old, n_old, m_old = [c_scr[s] for s in seqs], [n_scr[s] for s in seqs], [m_scr[s] for s in seqs]
    src3 = [[t.astype(F32) for t in _split3(src_all[s])] for s in seqs]
    bh3 = [[t.astype(F32) for t in _split3(bh_all[s])] for s in seqs]

    def decay_logits(s, h):
        a, b = lane128 == MISC_IG + h, lane128 == MISC_FG + h
        lhs = jnp.concatenate([jnp.where(a, t, jnp.where(b, 1.0, 0.0)) for t in src3[s]], axis=1)
        rhs = jnp.concatenate([jnp.where(a, 1.0, jnp.where(b, t, 0.0)) for t in bh3[s]], axis=1)
        return _dot_nt(lhs, rhs)

    ig = {(s, h): gates_t[s][MISC_IG + h:MISC_IG + h + 1, :] for s, h in pairs}
    bh = {(s, h): bh_t[s][MISC_FG + h:MISC_FG + h + 1, :] for s, h in pairs}
    m_prev = {(s, h): m_old[s][:, h:h + 1] for s, h in pairs}
    dm = {p: jnp.where(causal, decay_logits(*p), -jnp.inf) for p in pairs}
    kq = {(s, h): _dot_nt(kb[s], jnp.where(lane_head == h, q[s], 0.0)) for s, h in pairs}
    m_state = {p: bh[p] + m_prev[p] for p in pairs}
    m_t = {p: jnp.maximum(m_state[p], jnp.max(dm[p], axis=0, keepdims=True)) for p in pairs}
    sc = {p: kq[p] * jnp.exp(dm[p] - m_t[p]) for p in pairs}
    g = {p: jnp.exp(m_state[p] - m_t[p]) for p in pairs}
    qn = [_dot_nt(jnp.where(row8 == lane_head, n_old[s], 0.0), q[s]) for s in seqs]
    den = {(s, h): jnp.sum(sc[s, h], axis=0, keepdims=True) + g[s, h] * qn[s][h:h + 1, :] for s, h in pairs}
    q_c = [_dot_nt(c_old[s], q[s]) for s in seqs]
    num = {(s, h): _dot_tn(vb[s], sc[s, h]) for s, h in pairs}
    for s in seqs:
        parts = []
        for h in hs:
            rows = slice(HEAD_DIM * h, HEAD_DIM * (h + 1))
            parts.append((num[s, h][rows, :] + g[s, h] * q_c[s][rows, :])
                         / jnp.maximum(jnp.abs(den[s, h]), jnp.exp(-m_t[s, h])))
        hh = jnp.concatenate(parts, axis=0).T
        y_ref[s] = (_head_rms(hh, on_ref[...]) * _sigmoid(o_ref[s])).astype(BF16)

    m_new = {p: m_t[p][:, lc - 1:lc] for p in pairs}
    bh_last = {p: bh[p][:, lc - 1:lc] for p in pairs}
    ws = {p: jnp.exp(bh_last[p] - bh[p] + ig[p] - m_new[p]) for p in pairs}
    for s in seqs:
        gl_row = jnp.zeros((1, GROUP), F32)
        m_new_row = m_old[s]
        ws8 = jnp.zeros((8, lc), F32)
        for h in hs:
            gl_row = jnp.where(lane_head == h, jnp.exp(bh_last[s, h] + m_prev[s, h] - m_new[s, h]), gl_row)
            m_new_row = jnp.where(lane128[0:1, :] == h, m_new[s, h], m_new_row)
            ws8 = jnp.where(row8s == h, ws[s, h], ws8)
        n_mat = _dot(ws8, kb[s])
        n_upd = jnp.zeros((1, GROUP), F32)
        for h in hs:
            n_upd = jnp.where(lane_head == h, n_mat[h:h + 1, :], n_upd)
        w_rows = jnp.concatenate([jnp.broadcast_to(ws[s, h], (HEAD_DIM, lc)) for h in hs], axis=0)
        c_scr[s] = c_old[s] * gl_row + jnp.where(_same_head(GROUP), _dot(v[s].T * w_rows, kb[s]), 0.0)
        n_new = gl_row * n_old[s] + n_upd
        n_scr[s] = n_new
        m_scr[s] = m_new_row
        nout_ref[s] = n_new
        mout_ref[s] = m_new_row

    @pl.when(j == pl.num_programs(1) - 1)
    def _():
        for s in seqs:
            cout_ref[s] = _unpack_heads(c_scr[s])


FFN_CHUNKS = ((0, 1024), (1024, 2048), (2048, FFN_HIDDEN))


def _out_kernel(xp_ref, xs_ref, *refs, n_p):
    yp_refs, ys_refs = refs[0:4], refs[4:8]
    wo_ref, nf_ref, wgu_ref, wd_ref, op_ref, os_ref = refs[8:]
    is_prompt = pl.program_id(0) < n_p
    y = jnp.concatenate([jnp.where(is_prompt, a[...], b[...]) for a, b in zip(yp_refs, ys_refs)], axis=1)
    x1 = jnp.where(is_prompt, xp_ref[...], xs_ref[...]) + jnp.dot(y, wo_ref[...], preferred_element_type=F32)
    h = _rms_rows(x1, nf_ref[...]).astype(BF16)
    ffn = None
    for lo, hi in FFN_CHUNKS:
        g = jnp.dot(h, wgu_ref[:, lo:hi], preferred_element_type=F32)
        u = jnp.dot(h, wgu_ref[:, FFN_HIDDEN + lo:FFN_HIDDEN + hi], preferred_element_type=F32)
        a = (g * _sigmoid(g) * u).astype(BF16)
        d = jnp.dot(a, wd_ref[lo:hi, :], preferred_element_type=F32)
        ffn = d if ffn is None else ffn + d

    @pl.when(is_prompt)
    def _():
        op_ref[...] = x1 + ffn

    @pl.when(jnp.logical_not(is_prompt))
    def _():
        os_ref[...] = x1 + ffn


def _out_proj_ffn(x_p, x_s, ys_p, ys_s, lw, layer, tm):
    n_p, n_s = x_p.shape[0] // tm, x_s.shape[0] // tm
    yp_spec, ys_spec = _two_source(n_p, tm, GROUP)
    xp_spec, xs_spec = _two_source(n_p, tm, D_MODEL)
    return pl.pallas_call(
        functools.partial(_out_kernel, n_p=n_p),
        grid=(n_p + n_s,),
        in_specs=[xp_spec, xs_spec, *[yp_spec] * 4, *[ys_spec] * 4,
                  pl.BlockSpec((None, D_MODEL, D_MODEL), lambda i: (layer, 0, 0)),
                  pl.BlockSpec((None, 1, D_MODEL), lambda i: (layer, 0, 0)),
                  pl.BlockSpec((None, D_MODEL, 2 * FFN_HIDDEN), lambda i: (layer, 0, 0)),
                  pl.BlockSpec((None, FFN_HIDDEN, D_MODEL), lambda i: (layer, 0, 0))],
        out_specs=[xp_spec, xs_spec],
        out_shape=[jax.ShapeDtypeStruct(x_p.shape, F32), jax.ShapeDtypeStruct(x_s.shape, F32)],
        compiler_params=pltpu.CompilerParams(dimension_semantics=("arbitrary",),
                                             vmem_limit_bytes=VMEM_LIMIT),
        name="out_proj_ffn",
    )(x_p, x_s, *ys_p, *ys_s, lw['w_out'], lw['norm_ffn'], lw['w_gu'], lw['w_down'])


def _seq_params():
    return pltpu.CompilerParams(dimension_semantics=("arbitrary", "arbitrary"),
                                vmem_limit_bytes=VMEM_LIMIT)


def _row_block(row0, seq, tb):
    base, per_seq = row0 // tb, seq // tb
    return lambda col: (lambda b, j: (base + b * per_seq + j, col))


def _layer_spec(layer, shape):
    return pl.BlockSpec((None,) + shape, lambda b, j: (layer,) + (0,) * len(shape))


def _state_spec(state_layer, shape):
    return pl.BlockSpec((None, None) + shape, lambda b, j: (state_layer, b) + (0,) * len(shape))


def _lru_call(proj, conv0, h0, state_layer, lw, layer, nseq, seq, row0, tb):
    rb, ob = _row_block(row0, seq, tb), _row_block(0, seq, tb)
    wl = functools.partial(_layer_spec, layer)
    st = functools.partial(_state_spec, state_layer)
    out_state = lambda shape: pl.BlockSpec((None,) + shape, lambda b, j: (b,) + (0,) * len(shape))
    return pl.pallas_call(
        _lru_kernel,
        grid=(nseq, seq // tb),
        in_specs=[pl.BlockSpec((tb, GROUP), rb(COL_LRU_X // GROUP)),
                  pl.BlockSpec((tb, GROUP), rb(COL_LRU_G // GROUP)),
                  st((8, GROUP)), st((1, GROUP)),
                  wl((CONV_W, GROUP)), wl((1, GROUP)), wl((GROUP, 2 * GROUP)), wl((1, 2 * GROUP)),
                  wl((1, GROUP)), wl((1, GROUP))],
        out_specs=[pl.BlockSpec((tb, GROUP), ob(0)), out_state((1, GROUP)), out_state((8, GROUP))],
        out_shape=[jax.ShapeDtypeStruct((nseq * seq, GROUP), BF16),
                   jax.ShapeDtypeStruct((nseq, 1, GROUP), F32),
                   jax.ShapeDtypeStruct((nseq, 8, GROUP), F32)],
        scratch_shapes=[pltpu.VMEM((8, GROUP), F32), pltpu.VMEM((1, GROUP), F32)],
        compiler_params=_seq_params(),
        name="rglru",
    )(proj, proj, conv0, h0, lw['conv_w'], lw['conv_b'], lw['lru_wg'], lw['lru_bg'], lw['lru_lambda'], lw['on_a'])


def _mla_pre_call(proj_p, proj_s, tabs, lw, layer, tm, nseq_p, seq_p):
    rows_p = nseq_p * seq_p
    rows = rows_p + proj_s.shape[0]
    n_p = rows_p // tm

    def cols(width, col):
        return (pl.BlockSpec((tm, width), lambda i: (jnp.minimum(i, n_p - 1), col // width)),
                pl.BlockSpec((tm, width), lambda i: (jnp.maximum(i - n_p, 0), col // width)))

    pos_blocks = seq_p // tm
    wl = lambda shape: pl.BlockSpec((None,) + shape, lambda i: (layer,) + (0,) * len(shape))
    tab = pl.BlockSpec((tm, 128), lambda i: (jnp.where(i < n_p, i % pos_blocks, pos_blocks + i - n_p), 0))
    ckv_p, ckv_s = _two_source(n_p, tm, KV_RANK)
    _, kr_s = _two_source(n_p, tm, ROPE_DIM)

    def kr_p_map(i):
        t = jnp.minimum(i, n_p - 1)
        return (t // pos_blocks, 0, t % pos_blocks)

    kr_p = pl.BlockSpec((None, ROPE_DIM, tm), kr_p_map)
    return pl.pallas_call(
        functools.partial(_mla_pre_kernel, n_p=n_p),
        grid=(rows // tm,),
        in_specs=[*cols(GROUP, COL_Q_LAT), *cols(128, COL_KV_LAT), *cols(128, COL_MISC),
                  tab, tab, tab,
                  wl((1, GROUP)), wl((GROUP, 1024)), wl((1, 1024)), wl((1, KV_RANK)), wl((1, 128)),
                  wl((KV_RANK, 1024)), wl((1, 512))],
        out_specs=[pl.BlockSpec((tm, 512), lambda i: (i, 0))] * 3 + [ckv_p, ckv_s, kr_p, kr_s],
        out_shape=[jax.ShapeDtypeStruct((rows, 512), BF16)] * 3
                  + [jax.ShapeDtypeStruct((rows_p, KV_RANK), F32), jax.ShapeDtypeStruct((rows - rows_p, KV_RANK), F32),
                     jax.ShapeDtypeStruct((nseq_p, ROPE_DIM, seq_p), F32),
                     jax.ShapeDtypeStruct((rows - rows_p, ROPE_DIM), F32)],
        compiler_params=pltpu.CompilerParams(dimension_semantics=("arbitrary",),
                                             vmem_limit_bytes=VMEM_LIMIT),
        name="mla_pre",
    )(proj_p, proj_s, proj_p, proj_s, proj_p, proj_s, *tabs, lw['q_norm'], lw['wq'], lw['gq'], lw['kv_norm'], lw['kr_norm'],
      lw['wkv'], lw['gk'])


def _kv_past_call(ckv, krope_t, lw, tr):
    depth, rows, _ = ckv.shape
    per_seq = krope_t.shape[3] // tr
    wl = lambda shape: pl.BlockSpec((None,) + shape, lambda l, i: (l,) + (0,) * len(shape))
    return pl.pallas_call(
        _kv_past_kernel,
        grid=(depth, rows // tr),
        in_specs=[pl.BlockSpec((None, tr, KV_RANK), lambda l, i: (l, i, 0)),
                  pl.BlockSpec((None, None, ROPE_DIM, tr), lambda l, i: (l, i // per_seq, 0, i % per_seq)),
                  wl((KV_RANK, 1024)), wl((1, 512))],
        out_specs=[pl.BlockSpec((None, tr, 512), lambda l, i: (l, i, 0))] * 2,
        out_shape=[jax.ShapeDtypeStruct((depth, rows, 512), BF16)] * 2,
        compiler_params=_seq_params(),
        name="kv_past",
    )(ckv, krope_t, lw['wkv'], lw['gk'])


def _attn_call(q, k, v, k_past, v_past, lw, layer, nseq, seq, row0, tq, tka):
    rb, ob = _row_block(row0, seq, tq), _row_block(0, seq, tq)
    qspec = pl.BlockSpec((tq, 512), rb(0))
    if k_past is None:
        assert row0 == 0
        ka, va = k, v
        past_spec = pl.BlockSpec((seq, 512), lambda b, j: (b, 0))
        n_past_static = None
    else:
        ka, va = k_past, v_past
        past_len = k_past.shape[1] // nseq
        past_spec = pl.BlockSpec((None, past_len, 512), lambda b, j: (layer, b, 0))
        n_past_static = past_len // tka
    return pl.pallas_call(
        functools.partial(_attn_kernel, tka=tka, n_past_static=n_past_static),
        grid=(nseq, seq // tq),
        in_specs=[qspec, past_spec, past_spec, qspec, qspec, _layer_spec(layer, (1, GROUP))],
        out_specs=pl.BlockSpec((tq, GROUP), ob(0)),
        out_shape=jax.ShapeDtypeStruct((nseq * seq, GROUP), BF16),
        scratch_shapes=[pltpu.VMEM((N_HEADS, 1, tq), F32), pltpu.VMEM((N_HEADS, 128, tq), F32)],
        compiler_params=_seq_params(),
        name="mla_attn",
    )(q, ka, va, k, v, lw['on_b'])


def _seq_block(nb, shape):
    return pl.BlockSpec((nb,) + shape, lambda i, j: (i,) + (0,) * len(shape))


def _ret_call(proj, tabs, tab_map, s0, state_layer, lw, layer, lc, nb):
    nseq, seq, _ = proj.shape
    blk = lambda col: pl.BlockSpec((nb, lc, GROUP), lambda i, j: (i, j, col // GROUP))
    tab = pl.BlockSpec((lc, 128), tab_map)
    return pl.pallas_call(
        _ret_kernel,
        grid=(nseq // nb, seq // lc),
        in_specs=[blk(COL_R_Q), blk(COL_R_K), blk(COL_R_V), blk(COL_R_G), tab, tab, tab,
                  pl.BlockSpec((None, nb, GROUP, HEAD_DIM), lambda i, j: (state_layer, i, 0, 0)),
                  _layer_spec(layer, (1, GROUP))],
        out_specs=[pl.BlockSpec((nb, lc, GROUP), lambda i, j: (i, j, 0)), _seq_block(nb, (GROUP, HEAD_DIM))],
        out_shape=[jax.ShapeDtypeStruct((nseq, seq, GROUP), BF16),
                   jax.ShapeDtypeStruct((nseq, GROUP, HEAD_DIM), F32)],
        scratch_shapes=[pltpu.VMEM((nb, GROUP, GROUP), F32), pltpu.VMEM((N_HEADS, lc, lc), F32),
                        pltpu.VMEM((lc, GROUP), F32), pltpu.VMEM((lc, GROUP), F32)],
        compiler_params=_seq_params(),
        name="retention",
    )(proj, proj, proj, proj, *tabs, s0, lw['on_c'])


def _mlstm_call(proj, c0, n0, m0, state_layer, lw, layer, lc, nb):
    nseq, seq, _ = proj.shape
    blk = lambda col: pl.BlockSpec((nb, lc, GROUP), lambda i, j: (i, j, col // GROUP))
    st = lambda shape: pl.BlockSpec((None, nb) + shape, lambda i, j: (state_layer, i) + (0,) * len(shape))
    return pl.pallas_call(
        _mlstm_kernel,
        grid=(nseq // nb, seq // lc),
        in_specs=[blk(COL_M_Q), blk(COL_M_K), blk(COL_M_V), blk(COL_M_O),
                  pl.BlockSpec((nb, lc, 128), lambda i, j: (i, j, COL_MISC // 128)), _layer_spec(layer, (1, 128)),
                  st((GROUP, HEAD_DIM)), st((1, GROUP)), st((1, 128)), _layer_spec(layer, (1, GROUP))],
        out_specs=[pl.BlockSpec((nb, lc, GROUP), lambda i, j: (i, j, 0)), _seq_block(nb, (GROUP, HEAD_DIM)),
                   _seq_block(nb, (1, GROUP)), _seq_block(nb, (1, 128))],
        out_shape=[jax.ShapeDtypeStruct((nseq, seq, GROUP), BF16),
                   jax.ShapeDtypeStruct((nseq, GROUP, HEAD_DIM), F32),
                   jax.ShapeDtypeStruct((nseq, 1, GROUP), F32),
                   jax.ShapeDtypeStruct((nseq, 1, 128), F32)],
        scratch_shapes=[pltpu.VMEM((nb, GROUP, GROUP), F32), pltpu.VMEM((nb, 1, GROUP), F32),
                        pltpu.VMEM((nb, 1, 128), F32)],
        compiler_params=_seq_params(),
        name="mlstm",
    )(proj, proj, proj, proj, proj, lw['b_if'], c0, n0, m0, lw['on_d'])


def _block_diag(s):
    h, d, e = s.shape[-3:]
    eye = jnp.eye(h, dtype=s.dtype)
    return (s[..., :, :, None, :] * eye[:, None, :, None]).reshape(s.shape[:-3] + (h * d, h * e))


def _rope_tables(pos, half, lanes, lo):
    inv = ROPE_THETA ** (-jnp.arange(half, dtype=F32) / half)
    ang = pos[:, None] * inv[None, :]
    cos, sin = jnp.cos(ang), jnp.sin(ang)
    n = pos.shape[0]
    c = jnp.ones((n, lanes), F32).at[:, lo:lo + 2 * half].set(jnp.concatenate([cos, cos], axis=1))
    sp = jnp.zeros((n, lanes), F32).at[:, lo + half:lo + 2 * half].set(sin)
    sm = jnp.zeros((n, lanes), F32).at[:, lo:lo + half].set(-sin)
    return c, sp, sm


def _prep_weights(norm_mix, w_in, lru_conv_w, lru_conv_b, lru_wa, lru_ba, lru_wx, lru_bx, lru_lambda,
                  mla_q_norm, mla_wq_b, mla_qn_norm, mla_qr_norm, mla_kv_norm, mla_kr_norm, mla_wkv_b,
                  mla_kn_norm, mlstm_b_if, out_norm, w_out, norm_ffn, w_gu, w_down):
    depth = w_in.shape[0]
    row = lambda a: a.reshape(depth, 1, -1)
    n_if = 2 * N_HEADS
    w_in_pad = _w_in_prep(jnp.swapaxes(w_in, 1, 2))

    lru_wg = jnp.concatenate([_block_diag(lru_wa), _block_diag(lru_wx)], axis=2).astype(BF16)
    lru_bg = jnp.concatenate([lru_ba, lru_bx], axis=1)

    wq = mla_wq_b.reshape(depth, GROUP, N_HEADS, HEAD_DIM + ROPE_DIM)
    wq = jnp.pad(wq, ((0, 0), (0, 0), (0, 0), (0, 128 - HEAD_DIM - ROPE_DIM))).reshape(depth, GROUP, 512)
    gq = jnp.concatenate([mla_qn_norm, mla_qr_norm, jnp.zeros((depth, 32), F32)], axis=1)
    gq = jnp.tile(gq, (1, N_HEADS))
    lane = np.arange(512)
    in_rope = (lane % 128 >= HEAD_DIM) & (lane % 128 < HEAD_DIM + ROPE_DIM)
    first_half = (lane % 128 - HEAD_DIM) < ROPE_DIM // 2
    swap = np.where(in_rope, np.where(first_half, lane + ROPE_DIM // 2, lane - ROPE_DIM // 2), lane)
    wq = jnp.concatenate([wq, wq[:, :, swap]], axis=2)
    gq = jnp.concatenate([gq, gq[:, swap]], axis=1)
    wkv = mla_wkv_b.reshape(depth, KV_RANK, N_HEADS, 2 * HEAD_DIM)
    pad_head = lambda w: jnp.pad(w, ((0, 0), (0, 0), (0, 0), (0, 128 - HEAD_DIM))).reshape(depth, KV_RANK, 512)
    wkv = jnp.concatenate([pad_head(wkv[..., :HEAD_DIM]), pad_head(wkv[..., HEAD_DIM:])], axis=2)
    gk = jnp.tile(jnp.concatenate([mla_kn_norm, jnp.zeros((depth, 128 - HEAD_DIM), F32)], axis=1), (1, N_HEADS))
    kr_norm = jnp.pad(mla_kr_norm, ((0, 0), (0, 128 - ROPE_DIM)))
    b_if = jnp.pad(mlstm_b_if, ((0, 0), (MISC_IG, 128 - MISC_IG - n_if)))

    return dict(
        norm_mix=row(norm_mix), w_in=w_in_pad,
        conv_w=lru_conv_w, conv_b=row(lru_conv_b), lru_wg=lru_wg, lru_bg=row(lru_bg), lru_lambda=row(lru_lambda),
        q_norm=row(mla_q_norm), wq=wq.astype(BF16), gq=row(gq), kv_norm=row(mla_kv_norm), kr_norm=row(kr_norm),
        wkv=wkv.astype(BF16), gk=row(gk), b_if=row(b_if),
        on_a=row(out_norm[:, :GROUP]), on_b=row(out_norm[:, GROUP:2 * GROUP]),
        on_c=row(out_norm[:, 2 * GROUP:3 * GROUP]), on_d=row(out_norm[:, 3 * GROUP:]),
        w_out=w_out.astype(BF16), norm_ffn=row(norm_ffn), w_gu=w_gu.astype(BF16), w_down=w_down.astype(BF16))


def _tile_rows(rows, cap):
    t = cap
    while rows % t:
        t //= 2
    return t


def kernel(x_prompt, x_sample, cache_mla_ckv, cache_mla_krope, state_lru_h, state_lru_conv, state_ret, state_mlstm_C, state_mlstm_n, state_mlstm_m, norm_mix, w_in, lru_conv_w, lru_conv_b, lru_wa, lru_ba, lru_wx, lru_bx, lru_lambda, mla_q_norm, mla_wq_b, mla_qn_norm, mla_qr_norm, mla_kv_norm, mla_kr_norm, mla_wkv_b, mla_kn_norm, mlstm_b_if, out_norm, w_out, norm_ffn, w_gu, w_down):
    bp, tp, _ = x_prompt.shape
    bs, ts, _ = x_sample.shape
    depth, _, past, _ = cache_mla_ckv.shape
    rows_p, rows_s = bp * tp, bs * ts
    tm = math.gcd(_tile_rows(tp, 512), rows_s)
    tb_p, tb_s = min(tp, 512), min(ts, 512)
    lc_p, lc_s = min(tp, 256), min(ts, 256)
    tq_p, tq_s = min(tp, 512), min(ts, 512)
    tka_s = past
    nb_p, nb_s = math.gcd(bp, 2), math.gcd(bs, 4)
    assert past % CHUNK == 0 and ts % CHUNK == 0 and tp % tm == 0 and rows_s % tm == 0

    lw = _prep_weights(norm_mix, w_in, lru_conv_w, lru_conv_b, lru_wa, lru_ba, lru_wx, lru_bx, lru_lambda,
                       mla_q_norm, mla_wq_b, mla_qn_norm, mla_qr_norm, mla_kv_norm, mla_kr_norm, mla_wkv_b,
                       mla_kn_norm, mlstm_b_if, out_norm, w_out, norm_ffn, w_gu, w_down)

    pos = jnp.concatenate([jnp.arange(tp, dtype=F32), jnp.tile(jnp.arange(past, past + ts, dtype=F32), bs)])
    tabs_mla = _rope_tables(pos, ROPE_DIM // 2, 128, HEAD_DIM)
    tabs_ret = tuple(jnp.concatenate([t, t], axis=1) for t in _rope_tables(pos, HEAD_DIM // 2, HEAD_DIM, 0))

    k_past, v_past = _kv_past_call(cache_mla_ckv.reshape(depth, bs * past, KV_RANK),
                                   jnp.swapaxes(cache_mla_krope, 2, 3), lw,
                                   _tile_rows(past, 1024))

    zeros = lambda *shape: jnp.zeros((1, bp) + shape, F32)
    st_p = dict(conv=zeros(8, GROUP), h=zeros(1, GROUP), s=zeros(GROUP, HEAD_DIM), c=zeros(GROUP, HEAD_DIM),
                n=zeros(1, GROUP), m=zeros(1, 128))
    st_s = dict(conv=jnp.pad(state_lru_conv, ((0, 0), (0, 0), (8 - (CONV_W - 1), 0), (0, 0))),
                h=state_lru_h[:, :, None, :], s=state_ret.reshape(depth, bs, GROUP, HEAD_DIM),
                c=state_mlstm_C.reshape(depth, bs, GROUP, HEAD_DIM),
                n=state_mlstm_n.reshape(depth, bs, 1, GROUP),
                m=jnp.pad(state_mlstm_m, ((0, 0), (0, 0), (0, 128 - N_HEADS)))[:, :, None, :])

    x_p = x_prompt.reshape(rows_p, D_MODEL)
    x_s = x_sample.reshape(rows_s, D_MODEL)
    acc = {name: [] for name in ('ckv_p', 'ckv_s', 'kr_p', 'kr_s', 'h_p', 'h_s', 'conv_p', 'conv_s',
                                 's_p', 's_s', 'c_p', 'c_s', 'n_p', 'n_s', 'm_p', 'm_s')}
    for l in range(depth):
        proj_p, proj_s = _in_proj(x_p, x_s, lw, l, tm)
        proj3_p, proj3_s = proj_p.reshape(bp, tp, IN_PAD), proj_s.reshape(bs, ts, IN_PAD)
        ya_p, h_p, conv_p = _lru_call(proj_p, st_p['conv'], st_p['h'], 0, lw, l, bp, tp, 0, tb_p)
        ya_s, h_s, conv_s = _lru_call(proj_s, st_s['conv'], st_s['h'], l, lw, l, bs, ts, 0, tb_s)
        q, k, v, ckv_p, ckv_s, kr_p, kr_s = _mla_pre_call(proj_p, proj_s, tabs_mla, lw, l, tm, bp, tp)
        yb_p = _attn_call(q, k, v, None, None, lw, l, bp, tp, 0, tq_p, tq_p)
        yb_s = _attn_call(q, k, v, k_past, v_past, lw, l, bs, ts, rows_p, tq_s, tka_s)
        yc_p, s_p = _ret_call(proj3_p, tabs_ret, lambda i, j: (j, 0), st_p['s'], 0, lw, l, lc_p, nb_p)
        yc_s, s_s = _ret_call(proj3_s, tabs_ret, lambda i, j: (tp // lc_s + j, 0), st_s['s'], l, lw, l, lc_s, nb_s)
        yd_p, c_p, n_p, m_p = _mlstm_call(proj3_p, st_p['c'], st_p['n'], st_p['m'], 0, lw, l, lc_p, nb_p)
        yd_s, c_s, n_s, m_s = _mlstm_call(proj3_s, st_s['c'], st_s['n'], st_s['m'], l, lw, l, lc_s, nb_s)
        flat = lambda y: y.reshape(-1, GROUP)
        x_p, x_s = _out_proj_ffn(x_p, x_s, (ya_p, yb_p, flat(yc_p), flat(yd_p)),
                                 (ya_s, yb_s, flat(yc_s), flat(yd_s)), lw, l, tm)
        for name, val in (('ckv_p', ckv_p), ('ckv_s', ckv_s), ('kr_p', kr_p), ('kr_s', kr_s), ('h_p', h_p),
                          ('h_s', h_s), ('conv_p', conv_p), ('conv_s', conv_s), ('s_p', s_p), ('s_s', s_s),
                          ('c_p', c_p), ('c_s', c_s), ('n_p', n_p), ('n_s', n_s), ('m_p', m_p), ('m_s', m_s)):
            acc[name].append(val)

    st = {name: jnp.stack(vals) for name, vals in acc.items()}

    def outputs(tag, b, t):
        krope = jnp.swapaxes(st['kr_p'], 2, 3) if tag == 'p' else st['kr_s'].reshape(depth, b, t, ROPE_DIM)
        return (st['ckv_' + tag].reshape(depth, b, t, KV_RANK), krope,
                st['h_' + tag][:, :, 0], st['conv_' + tag][:, :, 8 - (CONV_W - 1):],
                st['s_' + tag].reshape(depth, b, N_HEADS, HEAD_DIM, HEAD_DIM),
                st['c_' + tag].reshape(depth, b, N_HEADS, HEAD_DIM, HEAD_DIM),
                st['n_' + tag].reshape(depth, b, N_HEADS, HEAD_DIM), st['m_' + tag][:, :, 0, :N_HEADS])

    return ((x_p.reshape(bp, tp, D_MODEL), x_s.reshape(bs, ts, D_MODEL))
            + outputs('p', bp, tp) + outputs('s', bs, ts))
```

```python
import functools
import math

import jax
import jax.numpy as jnp
import numpy as np
from jax import lax
from jax.experimental import pallas as pl
from jax.experimental.pallas import tpu as pltpu

F32 = jnp.float32
BF16 = jnp.bfloat16

D_MODEL = 1024
CHUNK = 64
HEAD_DIM = 64
GROUP = 256
N_HEADS = 4
RMS_EPS = 1e-6
ROPE_THETA = 10000.0
CONV_W = 4
LRU_C = 8.0
KV_RANK = 128
ROPE_DIM = 32
FFN_HIDDEN = 2816
IN_PAD = 3072

COL_LRU_X, COL_LRU_G, COL_Q_LAT = 0, 256, 512
COL_KV_LAT, COL_MISC = 768, 896
COL_R_Q, COL_R_K, COL_R_V, COL_R_G = 1024, 1280, 1536, 1792
COL_M_Q, COL_M_K, COL_M_V, COL_M_O = 2048, 2304, 2560, 2816
MISC_IG = 32
MISC_FG = MISC_IG + N_HEADS
ATT_SCALE = (HEAD_DIM + ROPE_DIM) ** -0.5
RET_LOG_DECAY = tuple(math.log(1.0 - 2.0 ** (-5.0 - h)) for h in range(N_HEADS))

VMEM_LIMIT = 56 * 1024 * 1024


def _dot(a, b):
    return jnp.dot(a.astype(BF16), b.astype(BF16), preferred_element_type=F32)


def _dot_nt(a, b):
    return lax.dot_general(a.astype(BF16), b.astype(BF16), (((1,), (1,)), ((), ())),
                           preferred_element_type=F32)


def _dot_tn(a, b):
    return lax.dot_general(a.astype(BF16), b.astype(BF16), (((0,), (0,)), ((), ())),
                           preferred_element_type=F32)


def _split3(x):
    hi = x.astype(BF16)
    r1 = x - hi.astype(F32)
    mid = r1.astype(BF16)
    lo = (r1 - mid.astype(F32)).astype(BF16)
    return hi, mid, lo


def _dot_f32_lhs(x, m, terms=2):
    parts = [jnp.dot(t, m, preferred_element_type=F32) for t in _split3(x)[:terms]]
    return functools.reduce(lambda a, b: a + b, parts)


def _pack_heads(s):
    r = lax.broadcasted_iota(jnp.int32, (HEAD_DIM, GROUP), 0)
    c = lax.broadcasted_iota(jnp.int32, (HEAD_DIM, GROUP), 1)
    tile = jnp.where(c % HEAD_DIM == r, 1.0, 0.0).astype(BF16)
    return jnp.where(_same_head(GROUP), _dot_f32_lhs(s, tile, terms=3), 0.0)


def _unpack_heads(s):
    r = lax.broadcasted_iota(jnp.int32, (GROUP, HEAD_DIM), 0)
    c = lax.broadcasted_iota(jnp.int32, (GROUP, HEAD_DIM), 1)
    fold = jnp.where(r % HEAD_DIM == c, 1.0, 0.0).astype(BF16)
    return _dot_f32_lhs(s, fold, terms=3)


def _rms_rows(x, g):
    return x * lax.rsqrt(jnp.mean(x * x, axis=-1, keepdims=True) + RMS_EPS) * g


def _lane_head(width, head_width):
    return lax.broadcasted_iota(jnp.int32, (1, width), 1) // head_width


def _same_head(n):
    r = lax.broadcasted_iota(jnp.int32, (n, n), 0) // HEAD_DIM
    c = lax.broadcasted_iota(jnp.int32, (n, n), 1) // HEAD_DIM
    return r == c


def _head_rms(y, g):
    mean_mat = jnp.where(_same_head(GROUP), 1.0 / HEAD_DIM, 0.0).astype(BF16)
    return y * lax.rsqrt(_dot_f32_lhs(y * y, mean_mat) + RMS_EPS) * g


def _shift_rows(x, d, fill):
    rows = lax.broadcasted_iota(jnp.int32, x.shape, 0)
    return jnp.where(rows >= d, pltpu.roll(x, d, 0), fill)


def _cumsum_rows(x):
    d = 1
    while d < x.shape[0]:
        x = x + _shift_rows(x, d, 0.0)
        d *= 2
    return x


def _softplus(z):
    return jnp.maximum(z, 0.0) + jnp.log1p(jnp.exp(-jnp.abs(z)))


def _sigmoid(z):
    return 0.5 * jnp.tanh(0.5 * z) + 0.5


def _gelu_tanh(z):
    return 0.5 * z * (1.0 + jnp.tanh(math.sqrt(2.0 / math.pi) * (z + 0.044715 * (z * z * z))))


def _rope_lanes(x, c, sp, sm, half):
    w = x.shape[1]
    return x * c + pltpu.roll(x, half, 1) * sp + pltpu.roll(x, w - half, 1) * sm


def _tile_lanes(t, n):
    return jnp.concatenate([t] * n, axis=1) if n > 1 else t


def _w_in_prep_kernel(wt_ref, o_ref):
    n_in, cols = wt_ref.shape
    n_if = 2 * N_HEADS
    split = COL_MISC + ROPE_DIM
    o_ref[:, :COL_MISC] = wt_ref[0:COL_MISC, :].T.astype(BF16)
    misc = jnp.concatenate([wt_ref[COL_MISC:split, :], wt_ref[n_in - n_if:n_in, :],
                            jnp.zeros((COL_R_Q - split - n_if, cols), F32)], axis=0)
    o_ref[:, COL_MISC:COL_R_Q] = misc.T.astype(BF16)
    o_ref[:, COL_R_Q:] = wt_ref[split:n_in - n_if, :].T.astype(BF16)


def _w_in_prep(w_in_t, cols=256):
    depth, n_in, d_model = w_in_t.shape
    return pl.pallas_call(
        _w_in_prep_kernel,
        grid=(depth, d_model // cols),
        in_specs=[pl.BlockSpec((None, n_in, cols), lambda l, i: (l, 0, i))],
        out_specs=pl.BlockSpec((None, cols, IN_PAD), lambda l, i: (l, i, 0)),
        out_shape=jax.ShapeDtypeStruct((depth, d_model, IN_PAD), BF16),
        compiler_params=_seq_params(),
        name="w_in_prep",
    )(w_in_t)


def _in_proj_kernel(xp_ref, xs_ref, g_ref, w_ref, op_ref, os_ref, *, n_p):
    is_prompt = pl.program_id(0) < n_p

    def project(x_ref, o_ref):
        o_ref[...] = jnp.dot(_rms_rows(x_ref[...], g_ref[...]).astype(BF16), w_ref[...],
                             preferred_element_type=F32)

    pl.when(is_prompt)(lambda: project(xp_ref, op_ref))
    pl.when(jnp.logical_not(is_prompt))(lambda: project(xs_ref, os_ref))


def _two_source(n_p, tm, width):
    return (pl.BlockSpec((tm, width), lambda i: (jnp.minimum(i, n_p - 1), 0)),
            pl.BlockSpec((tm, width), lambda i: (jnp.maximum(i - n_p, 0), 0)))


def _in_proj(x_p, x_s, lw, layer, tm):
    n_p, n_s = x_p.shape[0] // tm, x_s.shape[0] // tm
    return pl.pallas_call(
        functools.partial(_in_proj_kernel, n_p=n_p),
        grid=(n_p + n_s,),
        in_specs=[*_two_source(n_p, tm, D_MODEL),
                  pl.BlockSpec((None, 1, D_MODEL), lambda i: (layer, 0, 0)),
                  pl.BlockSpec((None, D_MODEL, IN_PAD), lambda i: (layer, 0, 0))],
        out_specs=list(_two_source(n_p, tm, IN_PAD)),
        out_shape=[jax.ShapeDtypeStruct((x_p.shape[0], IN_PAD), F32),
                   jax.ShapeDtypeStruct((x_s.shape[0], IN_PAD), F32)],
        compiler_params=pltpu.CompilerParams(dimension_semantics=("arbitrary",),
                                             vmem_limit_bytes=VMEM_LIMIT),
        name="in_proj",
    )(x_p, x_s, lw['norm_mix'], lw['w_in'])


def _lru_kernel(x_ref, g_ref, conv0_ref, h0_ref, cw_ref, cb_ref, wg_ref, bg_ref, lam_ref, on_ref,
                y_ref, hout_ref, convout_ref, xp_scr, h_scr):
    j = pl.program_id(1)
    tb = x_ref.shape[0]

    @pl.when(j == 0)
    def _():
        xp_scr[...] = conv0_ref[...]
        h_scr[...] = h0_ref[...]

    x = x_ref[...]
    prev = xp_scr[...]
    row8 = lax.broadcasted_iota(jnp.int32, (8, GROUP), 0)

    def delayed(d):
        r = pltpu.roll(x, d, 0)
        head = jnp.where(row8 < d, pltpu.roll(prev, d, 0), r[0:8, :])
        return jnp.concatenate([head, r[8:, :]], axis=0)

    cw = cw_ref[...]
    xc = cb_ref[...] + cw[3:4] * x + cw[2:3] * delayed(1) + cw[1:2] * delayed(2) + cw[0:1] * delayed(3)
    tail = x[tb - 8:tb, :]
    xp_scr[...] = tail
    convout_ref[...] = tail

    gates = _dot(xc, wg_ref[...]) + bg_ref[...]
    r = _sigmoid(gates[:, :GROUP])
    i = _sigmoid(gates[:, GROUP:])
    log_a = -LRU_C * r * _softplus(-lam_ref[...])
    a = jnp.exp(log_a)
    u = jnp.sqrt(-jnp.tanh(log_a) * (a * a + 1.0)) * i * xc

    d = 1
    while d < tb:
        u = a * _shift_rows(u, d, 0.0) + u
        a = a * _shift_rows(a, d, 1.0)
        d *= 2
    h = u + a * h_scr[...]
    h_last = h[tb - 1:tb, :]
    h_scr[...] = h_last
    hout_ref[...] = h_last
    y_ref[...] = (_head_rms(h, on_ref[...]) * _gelu_tanh(g_ref[...])).astype(BF16)


def _mla_seg_rms(x, g, rope):
    if rope:
        r = lax.broadcasted_iota(jnp.int32, (256, 256), 0)
        c = lax.broadcasted_iota(jnp.int32, (256, 256), 1)
        same = (r // 128) == (c // 128)
        rl, cl = r % 128, c % 128
        in_nope = same & (rl < HEAD_DIM) & (cl < HEAD_DIM)
        in_rope = same & (rl >= HEAD_DIM) & (cl >= HEAD_DIM)
        seg = jnp.where(in_nope, 1.0 / HEAD_DIM, jnp.where(in_rope, 1.0 / ROPE_DIM, 0.0)).astype(BF16)
        ms = jnp.concatenate([_dot_f32_lhs(jnp.square(x[:, 256 * p:256 * p + 256]), seg)
                              for p in range(N_HEADS // 2)], axis=1)
        return lax.rsqrt(ms + RMS_EPS) if g is None else x * lax.rsqrt(ms + RMS_EPS) * g
    outs = []
    for h in range(N_HEADS):
        xh = x[:, 128 * h:128 * h + 128]
        outs.append(xh * lax.rsqrt(jnp.sum(xh * xh, axis=-1, keepdims=True) * (1.0 / HEAD_DIM) + RMS_EPS))
    return jnp.concatenate(outs, axis=1) * g


def _with_ones_lane(v):
    lane = lax.broadcasted_iota(jnp.int32, v.shape, 1) % 128
    return jnp.where(lane == HEAD_DIM, 1.0, v)


def _mla_pre_kernel(qlat_p, qlat_s, kvlat_p, kvlat_s, misc_p, misc_s, c_ref, sp_ref, sm_ref,
                    qn_ref, wq_ref, gq_ref, kvn_ref, krn_ref, wkv_ref, gk_ref,
                    q_out, k_out, v_out, ckv_p, ckv_s, kr_p, kr_s, *, n_p):
    c, sp, sm = c_ref[...], sp_ref[...], sm_ref[...]
    half = ROPE_DIM // 2
    is_prompt = pl.program_id(0) < n_p
    qlat = jnp.where(is_prompt, qlat_p[...], qlat_s[...])
    kvlat = jnp.where(is_prompt, kvlat_p[...], kvlat_s[...])
    misc = jnp.where(is_prompt, misc_p[...], misc_s[...])

    qboth = _dot(_rms_rows(qlat, qn_ref[...]), wq_ref[...])
    qraw, qswap = qboth[:, :512], qboth[:, 512:]
    gq, gq_swap = gq_ref[:, :512], gq_ref[:, 512:]
    inv = _mla_seg_rms(qraw, None, rope=True)
    q = inv * (qraw * (gq * _tile_lanes(c, 4)) + qswap * (gq_swap * _tile_lanes(sp + sm, 4)))
    q_out[...] = q.astype(BF16)

    ckv = _rms_rows(kvlat, kvn_ref[...])
    lane = lax.broadcasted_iota(jnp.int32, misc.shape, 1)
    kr = jnp.where(lane < ROPE_DIM, misc, 0.0)
    kr = kr * lax.rsqrt(jnp.sum(kr * kr, axis=-1, keepdims=True) * (1.0 / ROPE_DIM) + RMS_EPS) * krn_ref[...]
    kr = _rope_lanes(pltpu.roll(kr, HEAD_DIM, 1), c, sp, sm, half)
    kr_new = pltpu.roll(kr, 128 - HEAD_DIM, 1)

    @pl.when(is_prompt)
    def _():
        ckv_p[...] = ckv
        kr_p[...] = kr_new.T[:ROPE_DIM, :]

    @pl.when(jnp.logical_not(is_prompt))
    def _():
        ckv_s[...] = ckv
        kr_s[...] = kr_new[:, :ROPE_DIM]

    kv = _dot(ckv, wkv_ref[...])
    kn = _mla_seg_rms(kv[:, :512], gk_ref[...], rope=False)
    k_out[...] = (kn + _tile_lanes(kr, 4)).astype(BF16)
    v_out[...] = _with_ones_lane(kv[:, 512:]).astype(BF16)


def _kv_past_kernel(ckv_ref, kr_ref, wkv_ref, gk_ref, k_out, v_out):
    kv = _dot(ckv_ref[...], wkv_ref[...])
    kn = _mla_seg_rms(kv[:, :512], gk_ref[...], rope=False)
    r = lax.broadcasted_iota(jnp.int32, (ROPE_DIM, 128), 0)
    cidx = lax.broadcasted_iota(jnp.int32, (ROPE_DIM, 128), 1)
    place = jnp.where(cidx == r + HEAD_DIM, 1.0, 0.0).astype(BF16)
    kr = _dot_tn(kr_ref[...], place)
    k_out[...] = (kn + _tile_lanes(kr, 4)).astype(BF16)
    v_out[...] = _with_ones_lane(kv[:, 512:]).astype(BF16)


def _attn_kernel(*refs, nb, tka, n_past_static):
    q_refs, (ka_ref, va_ref) = refs[:nb], refs[nb:nb + 2]
    kb_refs, vb_refs = refs[nb + 2:2 * nb + 2], refs[2 * nb + 2:3 * nb + 2]
    on_ref, y_ref, m_scr, acc_scr = refs[3 * nb + 2:]
    j = pl.program_id(1)
    tq = q_refs[0].shape[0]
    past_rows = ka_ref.shape[0] // nb
    n_past = j * (tq // tka) if n_past_static is None else n_past_static
    key_c = lax.broadcasted_iota(jnp.int32, (tq, tq), 0) // CHUNK
    qry_c = lax.broadcasted_iota(jnp.int32, (tq, tq), 1) // CHUNK
    visible = key_c <= qry_c
    pairs = [(s, h) for s in range(nb) for h in range(N_HEADS)]
    lanes = lambda h: slice(128 * h, 128 * h + 128)
    slot = lambda s, h: s * N_HEADS + h
    c = ATT_SCALE * math.log2(math.e)

    sc = {(s, h): jnp.where(visible, _dot_nt(kb_refs[s][:, lanes(h)], q_refs[s][:, lanes(h)]) * c, -jnp.inf)
          for s, h in pairs}
    m = {p: jnp.max(sc[p], axis=0, keepdims=True) for p in pairs}
    pr = {p: jnp.exp2(sc[p] - m[p]).astype(BF16) for p in pairs}
    pv = {(s, h): _dot_tn(vb_refs[s][:, lanes(h)], pr[s, h]) for s, h in pairs}
    for p in pairs:
        m_scr[slot(*p)] = m[p]
        acc_scr[slot(*p)] = pv[p]

    def body(t, carry):
        off = pl.multiple_of(t * tka, tka)
        rows = lambda s: pl.ds(s * past_rows + off, tka)
        m_old = {p: m_scr[slot(*p)] for p in pairs}
        sc = {(s, h): _dot_nt(ka_ref[rows(s), lanes(h)], q_refs[s][:, lanes(h)]) * c for s, h in pairs}
        m_new = {p: jnp.maximum(m_old[p], jnp.max(sc[p], axis=0, keepdims=True)) for p in pairs}
        pr = {p: jnp.exp2(sc[p] - m_new[p]).astype(BF16) for p in pairs}
        pv = {(s, h): _dot_tn(va_ref[rows(s), lanes(h)], pr[s, h]) for s, h in pairs}
        for p in pairs:
            acc_scr[slot(*p)] = jnp.exp2(m_old[p] - m_new[p]) * acc_scr[slot(*p)] + pv[p]
            m_scr[slot(*p)] = m_new[p]
        return carry

    lax.fori_loop(0, n_past, body, 0)
    for s in range(nb):
        outs = []
        for h in range(N_HEADS):
            a = acc_scr[slot(s, h)]
            outs.append(a[:HEAD_DIM, :] / a[HEAD_DIM:HEAD_DIM + 1, :])
        o = jnp.concatenate(outs, axis=0).T
        y_ref[s] = _head_rms(o, on_ref[...]).astype(BF16)


def _ret_kernel(q_ref, k_ref, v_ref, g_ref, c_ref, sp_ref, sm_ref, s0_ref, on_ref,
                y_ref, sout_ref, s_scr, dmat_scr, cross_scr, tail_scr):
    nb, lc, _ = q_ref.shape
    seqs, hs = range(nb), range(N_HEADS)
    b, j = pl.program_id(0), pl.program_id(1)
    lane_head = _lane_head(GROUP, HEAD_DIM)
    lg_lane = jnp.zeros((1, GROUP), F32)
    for h in hs:
        lg_lane = jnp.where(lane_head == h, RET_LOG_DECAY[h], lg_lane)

    @pl.when((b == 0) & (j == 0))
    def _():
        t_col = lax.broadcasted_iota(jnp.int32, (lc, 1), 0).astype(F32)
        cross_scr[...] = jnp.exp((t_col + 1.0) * lg_lane)
        tail_scr[...] = jnp.exp((lc - 1.0 - t_col) * lg_lane)
        ti = lax.broadcasted_iota(jnp.int32, (lc, lc), 0)
        si = lax.broadcasted_iota(jnp.int32, (lc, lc), 1)
        causal = ti >= si
        diff = jnp.where(causal, ti - si, 0).astype(F32)
        for h in hs:
            dmat_scr[h] = jnp.where(causal, jnp.exp(diff * RET_LOG_DECAY[h]), 0.0)

    @pl.when(j == 0)
    def _():
        for s in seqs:
            s_scr[s] = _pack_heads(s0_ref[s])

    c, sp, sm = (_tile_lanes(t[...], 2) for t in (c_ref, sp_ref, sm_ref))
    half = HEAD_DIM // 2
    q = [_rope_lanes(q_ref[s], c, sp, sm, half) for s in seqs]
    k = [_rope_lanes(k_ref[s], c, sp, sm, half) * (HEAD_DIM ** -0.5) for s in seqs]
    v = [v_ref[s].astype(BF16) for s in seqs]
    kb = [k[s].astype(BF16) for s in seqs]
    s_old = [s_scr[s] for s in seqs]
    y = [_dot(q[s], s_old[s]) * cross_scr[...] for s in seqs]
    att = [[_dot_nt(jnp.where(lane_head == h, q[s], 0.0), kb[s]) for h in hs] for s in seqs]
    att = [[(att[s][h] * dmat_scr[h]).astype(BF16) for h in hs] for s in seqs]
    yh = [[_dot(att[s][h], v[s]) for h in hs] for s in seqs]
    s_upd = [_dot_tn(k[s] * tail_scr[...], v[s]) for s in seqs]
    for s in seqs:
        for h in hs:
            y[s] = y[s] + jnp.where(lane_head == h, yh[s][h], 0.0)
        s_scr[s] = s_old[s] * jnp.exp(float(lc) * lg_lane) + jnp.where(_same_head(GROUP), s_upd[s], 0.0)
        g = g_ref[s]
        y_ref[s] = (_head_rms(y[s], on_ref[...]) * (g * _sigmoid(g))).astype(BF16)

    @pl.when(j == pl.num_programs(1) - 1)
    def _():
        for s in seqs:
            sout_ref[s] = _unpack_heads(s_scr[s])


def _mlstm_kernel(q_ref, k_ref, v_ref, o_ref, misc_ref, bif_ref, c0_ref, n0_ref, m0_ref, on_ref,
                  y_ref, cout_ref, nout_ref, mout_ref, c_scr, n_scr, m_scr):
    nb, lc, _ = q_ref.shape
    seqs, hs = range(nb), range(N_HEADS)
    pairs = [(s, h) for s in seqs for h in hs]
    j = pl.program_id(1)

    @pl.when(j == 0)
    def _():
        for s in seqs:
            c_scr[s] = _pack_heads(c0_ref[s])
            n_scr[s] = n0_ref[s]
            m_scr[s] = m0_ref[s]

    lane_head = _lane_head(GROUP, HEAD_DIM)
    lane128 = lax.broadcasted_iota(jnp.int32, (lc, 128), 1)
    si = lax.broadcasted_iota(jnp.int32, (lc, lc), 0)
    ti = lax.broadcasted_iota(jnp.int32, (lc, lc), 1)
    causal = si <= ti
    row8 = lax.broadcasted_iota(jnp.int32, (8, GROUP), 0)
    row8s = lax.broadcasted_iota(jnp.int32, (8, lc), 0)

    q = [q_ref[s] for s in seqs]
    kb = [(k_ref[s] * (HEAD_DIM ** -0.5)).astype(BF16) for s in seqs]
    v = [v_ref[s] for s in seqs]
    vb = [v[s].astype(BF16) for s in seqs]
    gates = [misc_ref[s] + bif_ref[...] for s in seqs]
    bh_all = [_cumsum_rows(-_softplus(-gates[s])) for s in seqs]
    src_all = [gates[s] - pltpu.roll(bh_all[s], 128 - N_HEADS, 1) for s in seqs]
    gates_t = [gates[s].T for s in seqs]
    bh_t = [bh_all[s].T for s in seqs]
    c_old, n_old, m_old = [c_scr[s] for s in seqs], [n_scr[s] for s in seqs], [m_scr[s] for s in seqs]
    src3 = [[t.astype(F32) for t in _split3(src_all[s])] for s in seqs]
    bh3 = [[t.astype(F32) for t in _split3(bh_all[s])] for s in seqs]

    def decay_logits(s, h):
        a, b = lane128 == MISC_IG + h, lane128 == MISC_FG + h
        lhs = jnp.concatenate([jnp.where(a, t, jnp.where(b, 1.0, 0.0)) for t in src3[s]], axis=1)
        rhs = jnp.concatenate([jnp.where(a, 1.0, jnp.where(b, t, 0.0)) for t in bh3[s]], axis=1)
        return _dot_nt(lhs, rhs)

    ig = {(s, h): gates_t[s][MISC_IG + h:MISC_IG + h + 1, :] for s, h in pairs}
    bh = {(s, h): bh_t[s][MISC_FG + h:MISC_FG + h + 1, :] for s, h in pairs}
    m_prev = {(s, h): m_old[s][:, h:h + 1] for s, h in pairs}
    dm = {p: jnp.where(causal, decay_logits(*p), -jnp.inf) for p in pairs}
    kq = {(s, h): _dot_nt(kb[s], jnp.where(lane_head == h, q[s], 0.0)) for s, h in pairs}
    m_state = {p: bh[p] + m_prev[p] for p in pairs}
    m_t = {p: jnp.maximum(m_state[p], jnp.max(dm[p], axis=0, keepdims=True)) for p in pairs}
    sc = {p: kq[p] * jnp.exp(dm[p] - m_t[p]) for p in pairs}
    g = {p: jnp.exp(m_state[p] - m_t[p]) for p in pairs}
    qn = [_dot_nt(jnp.where(row8 == lane_head, n_old[s], 0.0), q[s]) for s in seqs]
    den = {(s, h): jnp.sum(sc[s, h], axis=0, keepdims=True) + g[s, h] * qn[s][h:h + 1, :] for s, h in pairs}
    q_c = [_dot_nt(c_old[s], q[s]) for s in seqs]
    num = {(s, h): _dot_tn(vb[s], sc[s, h]) for s, h in pairs}
    for s in seqs:
        parts = []
        for h in hs:
            rows = slice(HEAD_DIM * h, HEAD_DIM * (h + 1))
            parts.append((num[s, h][rows, :] + g[s, h] * q_c[s][rows, :])
                         / jnp.maximum(jnp.abs(den[s, h]), jnp.exp(-m_t[s, h])))
        hh = jnp.concatenate(parts, axis=0).T
        y_ref[s] = (_head_rms(hh, on_ref[...]) * _sigmoid(o_ref[s])).astype(BF16)

    m_new = {p: m_t[p][:, lc - 1:lc] for p in pairs}
    bh_last = {p: bh[p][:, lc - 1:lc] for p in pairs}
    ws = {p: jnp.exp(bh_last[p] - bh[p] + ig[p] - m_new[p]) for p in pairs}
    for s in seqs:
        gl_row = jnp.zeros((1, GROUP), F32)
        m_new_row = m_old[s]
        ws8 = jnp.zeros((8, lc), F32)
        for h in hs:
            gl_row = jnp.where(lane_head == h, jnp.exp(bh_last[s, h] + m_prev[s, h] - m_new[s, h]), gl_row)
            m_new_row = jnp.where(lane128[0:1, :] == h, m_new[s, h], m_new_row)
            ws8 = jnp.where(row8s == h, ws[s, h], ws8)
        n_mat = _dot(ws8, kb[s])
        n_upd = jnp.zeros((1, GROUP), F32)
        for h in hs:
            n_upd = jnp.where(lane_head == h, n_mat[h:h + 1, :], n_upd)
        w_rows = jnp.concatenate([jnp.broadcast_to(ws[s, h], (HEAD_DIM, lc)) for h in hs], axis=0)
        c_scr[s] = c_old[s] * gl_row + jnp.where(_same_head(GROUP), _dot(v[s].T * w_rows, kb[s]), 0.0)
        n_new = gl_row * n_old[s] + n_upd
        n_scr[s] = n_new
        m_scr[s] = m_new_row
        nout_ref[s] = n_new
        mout_ref[s] = m_new_row

    @pl.when(j == pl.num_programs(1) - 1)
    def _():
        for s in seqs:
            cout_ref[s] = _unpack_heads(c_scr[s])


FFN_CHUNKS = ((0, 1024), (1024, 2048), (2048, FFN_HIDDEN))


def _out_kernel(xp_ref, xs_ref, *refs, n_p):
    yp_refs, ys_refs = refs[0:4], refs[4:8]
    wo_ref, nf_ref, wgu_ref, wd_ref, op_ref, os_ref = refs[8:]
    is_prompt = pl.program_id(0) < n_p
    y = jnp.concatenate([jnp.where(is_prompt, a[...], b[...]) for a, b in zip(yp_refs, ys_refs)], axis=1)
    x1 = jnp.where(is_prompt, xp_ref[...], xs_ref[...]) + jnp.dot(y, wo_ref[...], preferred_element_type=F32)
    h = _rms_rows(x1, nf_ref[...]).astype(BF16)
    ffn = None
    for lo, hi in FFN_CHUNKS:
        g = jnp.dot(h, wgu_ref[:, lo:hi], preferred_element_type=F32)
        u = jnp.dot(h, wgu_ref[:, FFN_HIDDEN + lo:FFN_HIDDEN + hi], preferred_element_type=F32)
        a = (g * _sigmoid(g) * u).astype(BF16)
        d = jnp.dot(a, wd_ref[lo:hi, :], preferred_element_type=F32)
        ffn = d if ffn is None else ffn + d

    @pl.when(is_prompt)
    def _():
        op_ref[...] = x1 + ffn

    @pl.when(jnp.logical_not(is_prompt))
    def _():
        os_ref[...] = x1 + ffn


def _out_proj_ffn(x_p, x_s, ys_p, ys_s, lw, layer, tm):
    n_p, n_s = x_p.shape[0] // tm, x_s.shape[0] // tm
    yp_spec, ys_spec = _two_source(n_p, tm, GROUP)
    xp_spec, xs_spec = _two_source(n_p, tm, D_MODEL)
    return pl.pallas_call(
        functools.partial(_out_kernel, n_p=n_p),
        grid=(n_p + n_s,),
        in_specs=[xp_spec, xs_spec, *[yp_spec] * 4, *[ys_spec] * 4,
                  pl.BlockSpec((None, D_MODEL, D_MODEL), lambda i: (layer, 0, 0)),
                  pl.BlockSpec((None, 1, D_MODEL), lambda i: (layer, 0, 0)),
                  pl.BlockSpec((None, D_MODEL, 2 * FFN_HIDDEN), lambda i: (layer, 0, 0)),
                  pl.BlockSpec((None, FFN_HIDDEN, D_MODEL), lambda i: (layer, 0, 0))],
        out_specs=[xp_spec, xs_spec],
        out_shape=[jax.ShapeDtypeStruct(x_p.shape, F32), jax.ShapeDtypeStruct(x_s.shape, F32)],
        compiler_params=pltpu.CompilerParams(dimension_semantics=("arbitrary",),
                                             vmem_limit_bytes=VMEM_LIMIT),
        name="out_proj_ffn",
    )(x_p, x_s, *ys_p, *ys_s, lw['w_out'], lw['norm_ffn'], lw['w_gu'], lw['w_down'])


def _seq_params():
    return pltpu.CompilerParams(dimension_semantics=("arbitrary", "arbitrary"),
                                vmem_limit_bytes=VMEM_LIMIT)


def _row_block(row0, seq, tb):
    base, per_seq = row0 // tb, seq // tb
    return lambda col: (lambda b, j: (base + b * per_seq + j, col))


def _layer_spec(layer, shape):
    return pl.BlockSpec((None,) + shape, lambda b, j: (layer,) + (0,) * len(shape))


def _state_spec(state_layer, shape):
    return pl.BlockSpec((None, None) + shape, lambda b, j: (state_layer, b) + (0,) * len(shape))


def _lru_call(proj, conv0, h0, state_layer, lw, layer, nseq, seq, row0, tb):
    rb, ob = _row_block(row0, seq, tb), _row_block(0, seq, tb)
    wl = functools.partial(_layer_spec, layer)
    st = functools.partial(_state_spec, state_layer)
    out_state = lambda shape: pl.BlockSpec((None,) + shape, lambda b, j: (b,) + (0,) * len(shape))
    return pl.pallas_call(
        _lru_kernel,
        grid=(nseq, seq // tb),
        in_specs=[pl.BlockSpec((tb, GROUP), rb(COL_LRU_X // GROUP)),
                  pl.BlockSpec((tb, GROUP), rb(COL_LRU_G // GROUP)),
                  st((8, GROUP)), st((1, GROUP)),
                  wl((CONV_W, GROUP)), wl((1, GROUP)), wl((GROUP, 2 * GROUP)), wl((1, 2 * GROUP)),
                  wl((1, GROUP)), wl((1, GROUP))],
        out_specs=[pl.BlockSpec((tb, GROUP), ob(0)), out_state((1, GROUP)), out_state((8, GROUP))],
        out_shape=[jax.ShapeDtypeStruct((nseq * seq, GROUP), BF16),
                   jax.ShapeDtypeStruct((nseq, 1, GROUP), F32),
                   jax.ShapeDtypeStruct((nseq, 8, GROUP), F32)],
        scratch_shapes=[pltpu.VMEM((8, GROUP), F32), pltpu.VMEM((1, GROUP), F32)],
        compiler_params=_seq_params(),
        name="rglru",
    )(proj, proj, conv0, h0, lw['conv_w'], lw['conv_b'], lw['lru_wg'], lw['lru_bg'], lw['lru_lambda'], lw['on_a'])


def _mla_pre_call(proj_p, proj_s, tabs, lw, layer, tm, nseq_p, seq_p):
    rows_p = nseq_p * seq_p
    rows = rows_p + proj_s.shape[0]
    n_p = rows_p // tm

    def cols(width, col):
        return (pl.BlockSpec((tm, width), lambda i: (jnp.minimum(i, n_p - 1), col // width)),
                pl.BlockSpec((tm, width), lambda i: (jnp.maximum(i - n_p, 0), col // width)))

    pos_blocks = seq_p // tm
    wl = lambda shape: pl.BlockSpec((None,) + shape, lambda i: (layer,) + (0,) * len(shape))
    tab = pl.BlockSpec((tm, 128), lambda i: (jnp.where(i < n_p, i % pos_blocks, pos_blocks + i - n_p), 0))
    ckv_p, ckv_s = _two_source(n_p, tm, KV_RANK)
    _, kr_s = _two_source(n_p, tm, ROPE_DIM)

    def kr_p_map(i):
        t = jnp.minimum(i, n_p - 1)
        return (t // pos_blocks, 0, t % pos_blocks)

    kr_p = pl.BlockSpec((None, ROPE_DIM, tm), kr_p_map)
    return pl.pallas_call(
        functools.partial(_mla_pre_kernel, n_p=n_p),
        grid=(rows // tm,),
        in_specs=[*cols(GROUP, COL_Q_LAT), *cols(128, COL_KV_LAT), *cols(128, COL_MISC),
                  tab, tab, tab,
                  wl((1, GROUP)), wl((GROUP, 1024)), wl((1, 1024)), wl((1, KV_RANK)), wl((1, 128)),
                  wl((KV_RANK, 1024)), wl((1, 512))],
        out_specs=[pl.BlockSpec((tm, 512), lambda i: (i, 0))] * 3 + [ckv_p, ckv_s, kr_p, kr_s],
        out_shape=[jax.ShapeDtypeStruct((rows, 512), BF16)] * 3
                  + [jax.ShapeDtypeStruct((rows_p, KV_RANK), F32), jax.ShapeDtypeStruct((rows - rows_p, KV_RANK), F32),
                     jax.ShapeDtypeStruct((nseq_p, ROPE_DIM, seq_p), F32),
                     jax.ShapeDtypeStruct((rows - rows_p, ROPE_DIM), F32)],
        compiler_params=pltpu.CompilerParams(dimension_semantics=("arbitrary",),
                                             vmem_limit_bytes=VMEM_LIMIT),
        name="mla_pre",
    )(proj_p, proj_s, proj_p, proj_s, proj_p, proj_s, *tabs, lw['q_norm'], lw['wq'], lw['gq'], lw['kv_norm'], lw['kr_norm'],
      lw['wkv'], lw['gk'])


def _kv_past_call(ckv, krope_t, lw, tr):
    depth, rows, _ = ckv.shape
    per_seq = krope_t.shape[3] // tr
    wl = lambda shape: pl.BlockSpec((None,) + shape, lambda l, i: (l,) + (0,) * len(shape))
    return pl.pallas_call(
        _kv_past_kernel,
        grid=(depth, rows // tr),
        in_specs=[pl.BlockSpec((None, tr, KV_RANK), lambda l, i: (l, i, 0)),
                  pl.BlockSpec((None, None, ROPE_DIM, tr), lambda l, i: (l, i // per_seq, 0, i % per_seq)),
                  wl((KV_RANK, 1024)), wl((1, 512))],
        out_specs=[pl.BlockSpec((None, tr, 512), lambda l, i: (l, i, 0))] * 2,
        out_shape=[jax.ShapeDtypeStruct((depth, rows, 512), BF16)] * 2,
        compiler_params=_seq_params(),
        name="kv_past",
    )(ckv, krope_t, lw['wkv'], lw['gk'])


def _attn_call(q, k, v, k_past, v_past, lw, layer, nseq, seq, row0, tq, tka, nb):
    base, per_seq = row0 // tq, seq // tq
    qspec = lambda s: pl.BlockSpec((tq, 512), lambda i, j: (base + (nb * i + s) * per_seq + j, 0))
    if k_past is None:
        assert row0 == 0
        ka, va = k, v
        past_spec = pl.BlockSpec((nb * seq, 512), lambda i, j: (i, 0))
        n_past_static = None
    else:
        ka, va = k_past, v_past
        past_len = k_past.shape[1] // nseq
        past_spec = pl.BlockSpec((None, nb * past_len, 512), lambda i, j: (layer, i, 0))
        n_past_static = past_len // tka
    own = [qspec(s) for s in range(nb)]
    return pl.pallas_call(
        functools.partial(_attn_kernel, nb=nb, tka=tka, n_past_static=n_past_static),
        grid=(nseq // nb, seq // tq),
        in_specs=[*own, past_spec, past_spec, *own, *own, _layer_spec(layer, (1, GROUP))],
        out_specs=pl.BlockSpec((nb, tq, GROUP), lambda i, j: (i, j, 0)),
        out_shape=jax.ShapeDtypeStruct((nseq, seq, GROUP), BF16),
        scratch_shapes=[pltpu.VMEM((nb * N_HEADS, 1, tq), F32), pltpu.VMEM((nb * N_HEADS, 128, tq), F32)],
        compiler_params=_seq_params(),
        name="mla_attn",
    )(*[q] * nb, ka, va, *[k] * nb, *[v] * nb, lw['on_b'])


def _seq_block(nb, shape):
    return pl.BlockSpec((nb,) + shape, lambda i, j: (i,) + (0,) * len(shape))


def _ret_call(proj, tabs, tab_map, s0, state_layer, lw, layer, lc, nb):
    nseq, seq, _ = proj.shape
    blk = lambda col: pl.BlockSpec((nb, lc, GROUP), lambda i, j: (i, j, col // GROUP))
    tab = pl.BlockSpec((lc, 128), tab_map)
    return pl.pallas_call(
        _ret_kernel,
        grid=(nseq // nb, seq // lc),
        in_specs=[blk(COL_R_Q), blk(COL_R_K), blk(COL_R_V), blk(COL_R_G), tab, tab, tab,
                  pl.BlockSpec((None, nb, GROUP, HEAD_DIM), lambda i, j: (state_layer, i, 0, 0)),
                  _layer_spec(layer, (1, GROUP))],
        out_specs=[pl.BlockSpec((nb, lc, GROUP), lambda i, j: (i, j, 0)), _seq_block(nb, (GROUP, HEAD_DIM))],
        out_shape=[jax.ShapeDtypeStruct((nseq, seq, GROUP), BF16),
                   jax.ShapeDtypeStruct((nseq, GROUP, HEAD_DIM), F32)],
        scratch_shapes=[pltpu.VMEM((nb, GROUP, GROUP), F32), pltpu.VMEM((N_HEADS, lc, lc), F32),
                        pltpu.VMEM((lc, GROUP), F32), pltpu.VMEM((lc, GROUP), F32)],
        compiler_params=_seq_params(),
        name="retention",
    )(proj, proj, proj, proj, *tabs, s0, lw['on_c'])


def _mlstm_call(proj, c0, n0, m0, state_layer, lw, layer, lc, nb):
    nseq, seq, _ = proj.shape
    blk = lambda col: pl.BlockSpec((nb, lc, GROUP), lambda i, j: (i, j, col // GROUP))
    st = lambda shape: pl.BlockSpec((None, nb) + shape, lambda i, j: (state_layer, i) + (0,) * len(shape))
    return pl.pallas_call(
        _mlstm_kernel,
        grid=(nseq // nb, seq // lc),
        in_specs=[blk(COL_M_Q), blk(COL_M_K), blk(COL_M_V), blk(COL_M_O),
                  pl.BlockSpec((nb, lc, 128), lambda i, j: (i, j, COL_MISC // 128)), _layer_spec(layer, (1, 128)),
                  st((GROUP, HEAD_DIM)), st((1, GROUP)), st((1, 128)), _layer_spec(layer, (1, GROUP))],
        out_specs=[pl.BlockSpec((nb, lc, GROUP), lambda i, j: (i, j, 0)), _seq_block(nb, (GROUP, HEAD_DIM)),
                   _seq_block(nb, (1, GROUP)), _seq_block(nb, (1, 128))],
        out_shape=[jax.ShapeDtypeStruct((nseq, seq, GROUP), BF16),
                   jax.ShapeDtypeStruct((nseq, GROUP, HEAD_DIM), F32),
                   jax.ShapeDtypeStruct((nseq, 1, GROUP), F32),
                   jax.ShapeDtypeStruct((nseq, 1, 128), F32)],
        scratch_shapes=[pltpu.VMEM((nb, GROUP, GROUP), F32), pltpu.VMEM((nb, 1, GROUP), F32),
                        pltpu.VMEM((nb, 1, 128), F32)],
        compiler_params=_seq_params(),
        name="mlstm",
    )(proj, proj, proj, proj, proj, lw['b_if'], c0, n0, m0, lw['on_d'])


def _block_diag(s):
    h, d, e = s.shape[-3:]
    eye = jnp.eye(h, dtype=s.dtype)
    return (s[..., :, :, None, :] * eye[:, None, :, None]).reshape(s.shape[:-3] + (h * d, h * e))


def _rope_tables(pos, half, lanes, lo):
    inv = ROPE_THETA ** (-jnp.arange(half, dtype=F32) / half)
    ang = pos[:, None] * inv[None, :]
    cos, sin = jnp.cos(ang), jnp.sin(ang)
    n = pos.shape[0]
    c = jnp.ones((n, lanes), F32).at[:, lo:lo + 2 * half].set(jnp.concatenate([cos, cos], axis=1))
    sp = jnp.zeros((n, lanes), F32).at[:, lo + half:lo + 2 * half].set(sin)
    sm = jnp.zeros((n, lanes), F32).at[:, lo:lo + half].set(-sin)
    return c, sp, sm


def _prep_weights(norm_mix, w_in, lru_conv_w, lru_conv_b, lru_wa, lru_ba, lru_wx, lru_bx, lru_lambda,
                  mla_q_norm, mla_wq_b, mla_qn_norm, mla_qr_norm, mla_kv_norm, mla_kr_norm, mla_wkv_b,
                  mla_kn_norm, mlstm_b_if, out_norm, w_out, norm_ffn, w_gu, w_down):
    depth = w_in.shape[0]
    row = lambda a: a.reshape(depth, 1, -1)
    n_if = 2 * N_HEADS
    w_in_pad = _w_in_prep(jnp.swapaxes(w_in, 1, 2))

    lru_wg = jnp.concatenate([_block_diag(lru_wa), _block_diag(lru_wx)], axis=2).astype(BF16)
    lru_bg = jnp.concatenate([lru_ba, lru_bx], axis=1)

    wq = mla_wq_b.reshape(depth, GROUP, N_HEADS, HEAD_DIM + ROPE_DIM)
    wq = jnp.pad(wq, ((0, 0), (0, 0), (0, 0), (0, 128 - HEAD_DIM - ROPE_DIM))).reshape(depth, GROUP, 512)
    gq = jnp.concatenate([mla_qn_norm, mla_qr_norm, jnp.zeros((depth, 32), F32)], axis=1)
    gq = jnp.tile(gq, (1, N_HEADS))
    lane = np.arange(512)
    in_rope = (lane % 128 >= HEAD_DIM) & (lane % 128 < HEAD_DIM + ROPE_DIM)
    first_half = (lane % 128 - HEAD_DIM) < ROPE_DIM // 2
    swap = np.where(in_rope, np.where(first_half, lane + ROPE_DIM // 2, lane - ROPE_DIM // 2), lane)
    wq = jnp.concatenate([wq, wq[:, :, swap]], axis=2)
    gq = jnp.concatenate([gq, gq[:, swap]], axis=1)
    wkv = mla_wkv_b.reshape(depth, KV_RANK, N_HEADS, 2 * HEAD_DIM)
    pad_head = lambda w: jnp.pad(w, ((0, 0), (0, 0), (0, 0), (0, 128 - HEAD_DIM))).reshape(depth, KV_RANK, 512)
    wkv = jnp.concatenate([pad_head(wkv[..., :HEAD_DIM]), pad_head(wkv[..., HEAD_DIM:])], axis=2)
    gk = jnp.tile(jnp.concatenate([mla_kn_norm, jnp.zeros((depth, 128 - HEAD_DIM), F32)], axis=1), (1, N_HEADS))
    kr_norm = jnp.pad(mla_kr_norm, ((0, 0), (0, 128 - ROPE_DIM)))
    b_if = jnp.pad(mlstm_b_if, ((0, 0), (MISC_IG, 128 - MISC_IG - n_if)))

    return dict(
        norm_mix=row(norm_mix), w_in=w_in_pad,
        conv_w=lru_conv_w, conv_b=row(lru_conv_b), lru_wg=lru_wg, lru_bg=row(lru_bg), lru_lambda=row(lru_lambda),
        q_norm=row(mla_q_norm), wq=wq.astype(BF16), gq=row(gq), kv_norm=row(mla_kv_norm), kr_norm=row(kr_norm),
        wkv=wkv.astype(BF16), gk=row(gk), b_if=row(b_if),
        on_a=row(out_norm[:, :GROUP]), on_b=row(out_norm[:, GROUP:2 * GROUP]),
        on_c=row(out_norm[:, 2 * GROUP:3 * GROUP]), on_d=row(out_norm[:, 3 * GROUP:]),
        w_out=w_out.astype(BF16), norm_ffn=row(norm_ffn), w_gu=w_gu.astype(BF16), w_down=w_down.astype(BF16))


def _tile_rows(rows, cap):
    t = cap
    while rows % t:
        t //= 2
    return t


def kernel(x_prompt, x_sample, cache_mla_ckv, cache_mla_krope, state_lru_h, state_lru_conv, state_ret, state_mlstm_C, state_mlstm_n, state_mlstm_m, norm_mix, w_in, lru_conv_w, lru_conv_b, lru_wa, lru_ba, lru_wx, lru_bx, lru_lambda, mla_q_norm, mla_wq_b, mla_qn_norm, mla_qr_norm, mla_kv_norm, mla_kr_norm, mla_wkv_b, mla_kn_norm, mlstm_b_if, out_norm, w_out, norm_ffn, w_gu, w_down):
    bp, tp, _ = x_prompt.shape
    bs, ts, _ = x_sample.shape
    depth, _, past, _ = cache_mla_ckv.shape
    rows_p, rows_s = bp * tp, bs * ts
    tm = math.gcd(_tile_rows(tp, 512), rows_s)
    tb_p, tb_s = min(tp, 512), min(ts, 512)
    lc_p, lc_s = min(tp, 256), min(ts, 256)
    tq_p, tq_s = min(tp, 512), min(ts, 512)
    tka_s = past
    nb_p, nb_s = math.gcd(bp, 2), math.gcd(bs, 4)
    assert past % CHUNK == 0 and ts % CHUNK == 0 and tp % tm == 0 and rows_s % tm == 0

    lw = _prep_weights(norm_mix, w_in, lru_conv_w, lru_conv_b, lru_wa, lru_ba, lru_wx, lru_bx, lru_lambda,
                       mla_q_norm, mla_wq_b, mla_qn_norm, mla_qr_norm, mla_kv_norm, mla_kr_norm, mla_wkv_b,
                       mla_kn_norm, mlstm_b_if, out_norm, w_out, norm_ffn, w_gu, w_down)

    pos = jnp.concatenate([jnp.arange(tp, dtype=F32), jnp.tile(jnp.arange(past, past + ts, dtype=F32), bs)])
    tabs_mla = _rope_tables(pos, ROPE_DIM // 2, 128, HEAD_DIM)
    tabs_ret = tuple(jnp.concatenate([t, t], axis=1) for t in _rope_tables(pos, HEAD_DIM // 2, HEAD_DIM, 0))

    k_past, v_past = _kv_past_call(cache_mla_ckv.reshape(depth, bs * past, KV_RANK),
                                   jnp.swapaxes(cache_mla_krope, 2, 3), lw,
                                   _tile_rows(past, 1024))

    zeros = lambda *shape: jnp.zeros((1, bp) + shape, F32)
    st_p = dict(conv=zeros(8, GROUP), h=zeros(1, GROUP), s=zeros(GROUP, HEAD_DIM), c=zeros(GROUP, HEAD_DIM),
                n=zeros(1, GROUP), m=zeros(1, 128))
    st_s = dict(conv=jnp.pad(state_lru_conv, ((0, 0), (0, 0), (8 - (CONV_W - 1), 0), (0, 0))),
                h=state_lru_h[:, :, None, :], s=state_ret.reshape(depth, bs, GROUP, HEAD_DIM),
                c=state_mlstm_C.reshape(depth, bs, GROUP, HEAD_DIM),
                n=state_mlstm_n.reshape(depth, bs, 1, GROUP),
                m=jnp.pad(state_mlstm_m, ((0, 0), (0, 0), (0, 128 - N_HEADS)))[:, :, None, :])

    x_p = x_prompt.reshape(rows_p, D_MODEL)
    x_s = x_sample.reshape(rows_s, D_MODEL)
    acc = {name: [] for name in ('ckv_p', 'ckv_s', 'kr_p', 'kr_s', 'h_p', 'h_s', 'conv_p', 'conv_s',
                                 's_p', 's_s', 'c_p', 'c_s', 'n_p', 'n_s', 'm_p', 'm_s')}
    for l in range(depth):
        proj_p, proj_s = _in_proj(x_p, x_s, lw, l, tm)
        proj3_p, proj3_s = proj_p.reshape(bp, tp, IN_PAD), proj_s.reshape(bs, ts, IN_PAD)
        ya_p, h_p, conv_p = _lru_call(proj_p, st_p['conv'], st_p['h'], 0, lw, l, bp, tp, 0, tb_p)
        ya_s, h_s, conv_s = _lru_call(proj_s, st_s['conv'], st_s['h'], l, lw, l, bs, ts, 0, tb_s)
        q, k, v, ckv_p, ckv_s, kr_p, kr_s = _mla_pre_call(proj_p, proj_s, tabs_mla, lw, l, tm, bp, tp)
        yb_p = _attn_call(q, k, v, None, None, lw, l, bp, tp, 0, tq_p, tq_p, nb_p)
        yb_s = _attn_call(q, k, v, k_past, v_past, lw, l, bs, ts, rows_p, tq_s, tka_s, 1)
        yc_p, s_p = _ret_call(proj3_p, tabs_ret, lambda i, j: (j, 0), st_p['s'], 0, lw, l, lc_p, nb_p)
        yc_s, s_s = _ret_call(proj3_s, tabs_ret, lambda i, j: (tp // lc_s + j, 0), st_s['s'], l, lw, l, lc_s, nb_s)
        yd_p, c_p, n_p, m_p = _mlstm_call(proj3_p, st_p['c'], st_p['n'], st_p['m'], 0, lw, l, lc_p, nb_p)
        yd_s, c_s, n_s, m_s = _mlstm_call(proj3_s, st_s['c'], st_s['n'], st_s['m'], l, lw, l, lc_s, nb_s)
        flat = lambda y: y.reshape(-1, GROUP)
        x_p, x_s = _out_proj_ffn(x_p, x_s, (ya_p, flat(yb_p), flat(yc_p), flat(yd_p)),
                                 (ya_s, flat(yb_s), flat(yc_s), flat(yd_s)), lw, l, tm)
        for name, val in (('ckv_p', ckv_p), ('ckv_s', ckv_s), ('kr_p', kr_p), ('kr_s', kr_s), ('h_p', h_p),
                          ('h_s', h_s), ('conv_p', conv_p), ('conv_s', conv_s), ('s_p', s_p), ('s_s', s_s),
                          ('c_p', c_p), ('c_s', c_s), ('n_p', n_p), ('n_s', n_s), ('m_p', m_p), ('m_s', m_s)):
            acc[name].append(val)

    st = {name: jnp.stack(vals) for name, vals in acc.items()}

    def outputs(tag, b, t):
        krope = jnp.swapaxes(st['kr_p'], 2, 3) if tag == 'p' else st['kr_s'].reshape(depth, b, t, ROPE_DIM)
        return (st['ckv_' + tag].reshape(depth, b, t, KV_RANK), krope,
                st['h_' + tag][:, :, 0], st['conv_' + tag][:, :, 8 - (CONV_W - 1):],
                st['s_' + tag].reshape(depth, b, N_HEADS, HEAD_DIM, HEAD_DIM),
                st['c_' + tag].reshape(depth, b, N_HEADS, HEAD_DIM, HEAD_DIM),
                st['n_' + tag].reshape(depth, b, N_HEADS, HEAD_DIM), st['m_' + tag][:, :, 0, :N_HEADS])

    return ((x_p.reshape(bp, tp, D_MODEL), x_s.reshape(bs, ts, D_MODEL))
            + outputs('p', bp, tp) + outputs('s', bs, ts))
```

```python
import functools
import math

import jax
import jax.numpy as jnp
import numpy as np
from jax import lax
from jax.experimental import pallas as pl
from jax.experimental.pallas import tpu as pltpu

F32 = jnp.float32
BF16 = jnp.bfloat16

D_MODEL = 1024
CHUNK = 64
HEAD_DIM = 64
GROUP = 256
N_HEADS = 4
RMS_EPS = 1e-6
ROPE_THETA = 10000.0
CONV_W = 4
LRU_C = 8.0
KV_RANK = 128
ROPE_DIM = 32
FFN_HIDDEN = 2816
IN_PAD = 3072

COL_LRU_X, COL_LRU_G, COL_Q_LAT = 0, 256, 512
COL_KV_LAT, COL_MISC = 768, 896
COL_R_Q, COL_R_K, COL_R_V, COL_R_G = 1024, 1280, 1536, 1792
COL_M_Q, COL_M_K, COL_M_V, COL_M_O = 2048, 2304, 2560, 2816
MISC_IG = 32
MISC_FG = MISC_IG + N_HEADS
ATT_SCALE = (HEAD_DIM + ROPE_DIM) ** -0.5
RET_LOG_DECAY = tuple(math.log(1.0 - 2.0 ** (-5.0 - h)) for h in range(N_HEADS))

VMEM_LIMIT = 56 * 1024 * 1024


def _dot(a, b):
    return jnp.dot(a.astype(BF16), b.astype(BF16), preferred_element_type=F32)


def _dot_nt(a, b):
    return lax.dot_general(a.astype(BF16), b.astype(BF16), (((1,), (1,)), ((), ())),
                           preferred_element_type=F32)


def _dot_tn(a, b):
    return lax.dot_general(a.astype(BF16), b.astype(BF16), (((0,), (0,)), ((), ())),
                           preferred_element_type=F32)


def _split3(x):
    hi = x.astype(BF16)
    r1 = x - hi.astype(F32)
    mid = r1.astype(BF16)
    lo = (r1 - mid.astype(F32)).astype(BF16)
    return hi, mid, lo


def _dot_f32_lhs(x, m, terms=2):
    parts = [jnp.dot(t, m, preferred_element_type=F32) for t in _split3(x)[:terms]]
    return functools.reduce(lambda a, b: a + b, parts)


def _pack_heads(s):
    r = lax.broadcasted_iota(jnp.int32, (HEAD_DIM, GROUP), 0)
    c = lax.broadcasted_iota(jnp.int32, (HEAD_DIM, GROUP), 1)
    tile = jnp.where(c % HEAD_DIM == r, 1.0, 0.0).astype(BF16)
    return jnp.where(_same_head(GROUP), _dot_f32_lhs(s, tile, terms=3), 0.0)


def _unpack_heads(s):
    r = lax.broadcasted_iota(jnp.int32, (GROUP, HEAD_DIM), 0)
    c = lax.broadcasted_iota(jnp.int32, (GROUP, HEAD_DIM), 1)
    fold = jnp.where(r % HEAD_DIM == c, 1.0, 0.0).astype(BF16)
    return _dot_f32_lhs(s, fold, terms=3)


def _rms_rows(x, g):
    return x * lax.rsqrt(jnp.mean(x * x, axis=-1, keepdims=True) + RMS_EPS) * g


def _lane_head(width, head_width):
    return lax.broadcasted_iota(jnp.int32, (1, width), 1) // head_width


def _same_head(n):
    r = lax.broadcasted_iota(jnp.int32, (n, n), 0) // HEAD_DIM
    c = lax.broadcasted_iota(jnp.int32, (n, n), 1) // HEAD_DIM
    return r == c


def _head_rms(y, g):
    mean_mat = jnp.where(_same_head(GROUP), 1.0 / HEAD_DIM, 0.0).astype(BF16)
    return y * lax.rsqrt(_dot_f32_lhs(y * y, mean_mat) + RMS_EPS) * g


def _shift_rows(x, d, fill):
    rows = lax.broadcasted_iota(jnp.int32, x.shape, 0)
    return jnp.where(rows >= d, pltpu.roll(x, d, 0), fill)


def _cumsum_rows(x):
    d = 1
    while d < x.shape[0]:
        x = x + _shift_rows(x, d, 0.0)
        d *= 2
    return x


def _softplus(z):
    return jnp.maximum(z, 0.0) + jnp.log1p(jnp.exp(-jnp.abs(z)))


def _sigmoid(z):
    return 0.5 * jnp.tanh(0.5 * z) + 0.5


def _gelu_tanh(z):
    return 0.5 * z * (1.0 + jnp.tanh(math.sqrt(2.0 / math.pi) * (z + 0.044715 * (z * z * z))))


def _rope_lanes(x, c, sp, sm, half):
    w = x.shape[1]
    return x * c + pltpu.roll(x, half, 1) * sp + pltpu.roll(x, w - half, 1) * sm


def _tile_lanes(t, n):
    return jnp.concatenate([t] * n, axis=1) if n > 1 else t


def _w_in_prep_kernel(wt_ref, o_ref):
    n_in, cols = wt_ref.shape
    n_if = 2 * N_HEADS
    split = COL_MISC + ROPE_DIM
    o_ref[:, :COL_MISC] = wt_ref[0:COL_MISC, :].T.astype(BF16)
    misc = jnp.concatenate([wt_ref[COL_MISC:split, :], wt_ref[n_in - n_if:n_in, :],
                            jnp.zeros((COL_R_Q - split - n_if, cols), F32)], axis=0)
    o_ref[:, COL_MISC:COL_R_Q] = misc.T.astype(BF16)
    o_ref[:, COL_R_Q:] = wt_ref[split:n_in - n_if, :].T.astype(BF16)


def _w_in_prep(w_in_t, cols=256):
    depth, n_in, d_model = w_in_t.shape
    return pl.pallas_call(
        _w_in_prep_kernel,
        grid=(depth, d_model // cols),
        in_specs=[pl.BlockSpec((None, n_in, cols), lambda l, i: (l, 0, i))],
        out_specs=pl.BlockSpec((None, cols, IN_PAD), lambda l, i: (l, i, 0)),
        out_shape=jax.ShapeDtypeStruct((depth, d_model, IN_PAD), BF16),
        compiler_params=_seq_params(),
        name="w_in_prep",
    )(w_in_t)


def _in_proj_kernel(xp_ref, xs_ref, g_ref, w_ref, op_ref, os_ref, *, n_p):
    is_prompt = pl.program_id(0) < n_p

    def project(x_ref, o_ref):
        o_ref[...] = jnp.dot(_rms_rows(x_ref[...], g_ref[...]).astype(BF16), w_ref[...],
                             preferred_element_type=F32)

    pl.when(is_prompt)(lambda: project(xp_ref, op_ref))
    pl.when(jnp.logical_not(is_prompt))(lambda: project(xs_ref, os_ref))


def _two_source(n_p, tm, width):
    return (pl.BlockSpec((tm, width), lambda i: (jnp.minimum(i, n_p - 1), 0)),
            pl.BlockSpec((tm, width), lambda i: (jnp.maximum(i - n_p, 0), 0)))


def _in_proj(x_p, x_s, lw, layer, tm):
    n_p, n_s = x_p.shape[0] // tm, x_s.shape[0] // tm
    return pl.pallas_call(
        functools.partial(_in_proj_kernel, n_p=n_p),
        grid=(n_p + n_s,),
        in_specs=[*_two_source(n_p, tm, D_MODEL),
                  pl.BlockSpec((None, 1, D_MODEL), lambda i: (layer, 0, 0)),
                  pl.BlockSpec((None, D_MODEL, IN_PAD), lambda i: (layer, 0, 0))],
        out_specs=list(_two_source(n_p, tm, IN_PAD)),
        out_shape=[jax.ShapeDtypeStruct((x_p.shape[0], IN_PAD), F32),
                   jax.ShapeDtypeStruct((x_s.shape[0], IN_PAD), F32)],
        compiler_params=pltpu.CompilerParams(dimension_semantics=("arbitrary",),
                                             vmem_limit_bytes=VMEM_LIMIT),
        name="in_proj",
    )(x_p, x_s, lw['norm_mix'], lw['w_in'])


def _lru_kernel(x_ref, g_ref, conv0_ref, h0_ref, cw_ref, cb_ref, wg_ref, bg_ref, lam_ref, on_ref,
                y_ref, hout_ref, convout_ref, xp_scr, h_scr):
    j = pl.program_id(1)
    tb = x_ref.shape[0]

    @pl.when(j == 0)
    def _():
        xp_scr[...] = conv0_ref[...]
        h_scr[...] = h0_ref[...]

    x = x_ref[...]
    prev = xp_scr[...]
    row8 = lax.broadcasted_iota(jnp.int32, (8, GROUP), 0)

    def delayed(d):
        r = pltpu.roll(x, d, 0)
        head = jnp.where(row8 < d, pltpu.roll(prev, d, 0), r[0:8, :])
        return jnp.concatenate([head, r[8:, :]], axis=0)

    cw = cw_ref[...]
    xc = cb_ref[...] + cw[3:4] * x + cw[2:3] * delayed(1) + cw[1:2] * delayed(2) + cw[0:1] * delayed(3)
    tail = x[tb - 8:tb, :]
    xp_scr[...] = tail
    convout_ref[...] = tail

    gates = _dot(xc, wg_ref[...]) + bg_ref[...]
    r = _sigmoid(gates[:, :GROUP])
    i = _sigmoid(gates[:, GROUP:])
    log_a = -LRU_C * r * _softplus(-lam_ref[...])
    a = jnp.exp(log_a)
    u = jnp.sqrt(-jnp.tanh(log_a) * (a * a + 1.0)) * i * xc

    d = 1
    while d < tb:
        u = a * _shift_rows(u, d, 0.0) + u
        a = a * _shift_rows(a, d, 1.0)
        d *= 2
    h = u + a * h_scr[...]
    h_last = h[tb - 1:tb, :]
    h_scr[...] = h_last
    hout_ref[...] = h_last
    y_ref[...] = (_head_rms(h, on_ref[...]) * _gelu_tanh(g_ref[...])).astype(BF16)


def _mla_seg_rms(x, g, rope):
    if rope:
        r = lax.broadcasted_iota(jnp.int32, (256, 256), 0)
        c = lax.broadcasted_iota(jnp.int32, (256, 256), 1)
        same = (r // 128) == (c // 128)
        rl, cl = r % 128, c % 128
        in_nope = same & (rl < HEAD_DIM) & (cl < HEAD_DIM)
        in_rope = same & (rl >= HEAD_DIM) & (cl >= HEAD_DIM)
        seg = jnp.where(in_nope, 1.0 / HEAD_DIM, jnp.where(in_rope, 1.0 / ROPE_DIM, 0.0)).astype(BF16)
        ms = jnp.concatenate([_dot_f32_lhs(jnp.square(x[:, 256 * p:256 * p + 256]), seg)
                              for p in range(N_HEADS // 2)], axis=1)
        return lax.rsqrt(ms + RMS_EPS) if g is None else x * lax.rsqrt(ms + RMS_EPS) * g
    outs = []
    for h in range(N_HEADS):
        xh = x[:, 128 * h:128 * h + 128]
        outs.append(xh * lax.rsqrt(jnp.sum(xh * xh, axis=-1, keepdims=True) * (1.0 / HEAD_DIM) + RMS_EPS))
    return jnp.concatenate(outs, axis=1) * g


def _with_ones_lane(v):
    lane = lax.broadcasted_iota(jnp.int32, v.shape, 1) % 128
    return jnp.where(lane == HEAD_DIM, 1.0, v)


def _mla_pre_kernel(qlat_p, qlat_s, kvlat_p, kvlat_s, misc_p, misc_s, c_ref, sp_ref, sm_ref,
                    qn_ref, wq_ref, gq_ref, kvn_ref, krn_ref, wkv_ref, gk_ref,
                    q_out, k_out, v_out, ckv_p, ckv_s, kr_p, kr_s, *, n_p):
    c, sp, sm = c_ref[...], sp_ref[...], sm_ref[...]
    half = ROPE_DIM // 2
    is_prompt = pl.program_id(0) < n_p
    qlat = jnp.where(is_prompt, qlat_p[...], qlat_s[...])
    kvlat = jnp.where(is_prompt, kvlat_p[...], kvlat_s[...])
    misc = jnp.where(is_prompt, misc_p[...], misc_s[...])

    qboth = _dot(_rms_rows(qlat, qn_ref[...]), wq_ref[...])
    qraw, qswap = qboth[:, :512], qboth[:, 512:]
    gq, gq_swap = gq_ref[:, :512], gq_ref[:, 512:]
    inv = _mla_seg_rms(qraw, None, rope=True)
    q = inv * (qraw * (gq * _tile_lanes(c, 4)) + qswap * (gq_swap * _tile_lanes(sp + sm, 4)))
    q_out[...] = q.astype(BF16)

    ckv = _rms_rows(kvlat, kvn_ref[...])
    lane = lax.broadcasted_iota(jnp.int32, misc.shape, 1)
    kr = jnp.where(lane < ROPE_DIM, misc, 0.0)
    kr = kr * lax.rsqrt(jnp.sum(kr * kr, axis=-1, keepdims=True) * (1.0 / ROPE_DIM) + RMS_EPS) * krn_ref[...]
    kr = _rope_lanes(pltpu.roll(kr, HEAD_DIM, 1), c, sp, sm, half)
    kr_new = pltpu.roll(kr, 128 - HEAD_DIM, 1)

    @pl.when(is_prompt)
    def _():
        ckv_p[...] = ckv
        kr_p[...] = kr_new.T[:ROPE_DIM, :]

    @pl.when(jnp.logical_not(is_prompt))
    def _():
        ckv_s[...] = ckv
        kr_s[...] = kr_new[:, :ROPE_DIM]

    kv = _dot(ckv, wkv_ref[...])
    kn = _mla_seg_rms(kv[:, :512], gk_ref[...], rope=False)
    k_out[...] = (kn + _tile_lanes(kr, 4)).astype(BF16)
    v_out[...] = _with_ones_lane(kv[:, 512:]).astype(BF16)


def _kv_past_kernel(ckv_ref, kr_ref, wkv_ref, gk_ref, k_out, v_out):
    kv = _dot(ckv_ref[...], wkv_ref[...])
    kn = _mla_seg_rms(kv[:, :512], gk_ref[...], rope=False)
    r = lax.broadcasted_iota(jnp.int32, (ROPE_DIM, 128), 0)
    cidx = lax.broadcasted_iota(jnp.int32, (ROPE_DIM, 128), 1)
    place = jnp.where(cidx == r + HEAD_DIM, 1.0, 0.0).astype(BF16)
    kr = _dot_tn(kr_ref[...], place)
    k_out[...] = (kn + _tile_lanes(kr, 4)).astype(BF16)
    v_out[...] = _with_ones_lane(kv[:, 512:]).astype(BF16)


def _attn_kernel(*refs, nb, tka, n_past_static):
    q_refs, (ka_ref, va_ref) = refs[:nb], refs[nb:nb + 2]
    kb_refs, vb_refs = refs[nb + 2:2 * nb + 2], refs[2 * nb + 2:3 * nb + 2]
    on_ref, y_ref, m_scr, acc_scr = refs[3 * nb + 2:]
    j = pl.program_id(1)
    tq = q_refs[0].shape[0]
    past_rows = ka_ref.shape[0] // nb
    n_past = j * (tq // tka) if n_past_static is None else n_past_static
    key_c = lax.broadcasted_iota(jnp.int32, (tq, tq), 0) // CHUNK
    qry_c = lax.broadcasted_iota(jnp.int32, (tq, tq), 1) // CHUNK
    visible = key_c <= qry_c
    pairs = [(s, h) for s in range(nb) for h in range(N_HEADS)]
    lanes = lambda h: slice(128 * h, 128 * h + 128)
    slot = lambda s, h: s * N_HEADS + h
    c = ATT_SCALE * math.log2(math.e)

    sc = {(s, h): jnp.where(visible, _dot_nt(kb_refs[s][:, lanes(h)], q_refs[s][:, lanes(h)]) * c, -jnp.inf)
          for s, h in pairs}
    m = {p: jnp.max(sc[p], axis=0, keepdims=True) for p in pairs}
    pr = {p: jnp.exp2(sc[p] - m[p]).astype(BF16) for p in pairs}
    pv = {(s, h): _dot_tn(vb_refs[s][:, lanes(h)], pr[s, h]) for s, h in pairs}
    for p in pairs:
        m_scr[slot(*p)] = m[p]
        acc_scr[slot(*p)] = pv[p]

    def body(t, carry):
        off = pl.multiple_of(t * tka, tka)
        rows = lambda s: pl.ds(s * past_rows + off, tka)
        m_old = {p: m_scr[slot(*p)] for p in pairs}
        sc = {(s, h): _dot_nt(ka_ref[rows(s), lanes(h)], q_refs[s][:, lanes(h)]) * c for s, h in pairs}
        m_new = {p: jnp.maximum(m_old[p], jnp.max(sc[p], axis=0, keepdims=True)) for p in pairs}
        pr = {p: jnp.exp2(sc[p] - m_new[p]).astype(BF16) for p in pairs}
        pv = {(s, h): _dot_tn(va_ref[rows(s), lanes(h)], pr[s, h]) for s, h in pairs}
        for p in pairs:
            acc_scr[slot(*p)] = jnp.exp2(m_old[p] - m_new[p]) * acc_scr[slot(*p)] + pv[p]
            m_scr[slot(*p)] = m_new[p]
        return carry

    lax.fori_loop(0, n_past, body, 0)
    for s in range(nb):
        outs = []
        for h in range(N_HEADS):
            a = acc_scr[slot(s, h)]
            outs.append(a[:HEAD_DIM, :] / a[HEAD_DIM:HEAD_DIM + 1, :])
        o = jnp.concatenate(outs, axis=0).T
        y_ref[s] = _head_rms(o, on_ref[...]).astype(BF16)


def _ret_kernel(q_ref, k_ref, v_ref, g_ref, c_ref, sp_ref, sm_ref, s0_ref, on_ref,
                y_ref, sout_ref, s_scr, dmat_scr, cross_scr, tail_scr):
    nb, lc, _ = q_ref.shape
    seqs, hs = range(nb), range(N_HEADS)
    b, j = pl.program_id(0), pl.program_id(1)
    lane_head = _lane_head(GROUP, HEAD_DIM)
    lg_lane = jnp.zeros((1, GROUP), F32)
    for h in hs:
        lg_lane = jnp.where(lane_head == h, RET_LOG_DECAY[h], lg_lane)

    @pl.when((b == 0) & (j == 0))
    def _():
        t_col = lax.broadcasted_iota(jnp.int32, (lc, 1), 0).astype(F32)
        cross_scr[...] = jnp.exp((t_col + 1.0) * lg_lane)
        tail_scr[...] = jnp.exp((lc - 1.0 - t_col) * lg_lane)
        ti = lax.broadcasted_iota(jnp.int32, (lc, lc), 0)
        si = lax.broadcasted_iota(jnp.int32, (lc, lc), 1)
        causal = ti >= si
        diff = jnp.where(causal, ti - si, 0).astype(F32)
        for h in hs:
            dmat_scr[h] = jnp.where(causal, jnp.exp(diff * RET_LOG_DECAY[h]), 0.0)

    @pl.when(j == 0)
    def _():
        for s in seqs:
            s_scr[s] = _pack_heads(s0_ref[s])

    c, sp, sm = (_tile_lanes(t[...], 2) for t in (c_ref, sp_ref, sm_ref))
    half = HEAD_DIM // 2
    q = [_rope_lanes(q_ref[s], c, sp, sm, half) for s in seqs]
    k = [_rope_lanes(k_ref[s], c, sp, sm, half) * (HEAD_DIM ** -0.5) for s in seqs]
    v = [v_ref[s].astype(BF16) for s in seqs]
    kb = [k[s].astype(BF16) for s in seqs]
    s_old = [s_scr[s] for s in seqs]
    y = [_dot(q[s], s_old[s]) * cross_scr[...] for s in seqs]
    att = [[_dot_nt(jnp.where(lane_head == h, q[s], 0.0), kb[s]) for h in hs] for s in seqs]
    att = [[(att[s][h] * dmat_scr[h]).astype(BF16) for h in hs] for s in seqs]
    yh = [[_dot(att[s][h], v[s]) for h in hs] for s in seqs]
    s_upd = [_dot_tn(k[s] * tail_scr[...], v[s]) for s in seqs]
    for s in seqs:
        for h in hs:
            y[s] = y[s] + jnp.where(lane_head == h, yh[s][h], 0.0)
        s_scr[s] = s_old[s] * jnp.exp(float(lc) * lg_lane) + jnp.where(_same_head(GROUP), s_upd[s], 0.0)
        g = g_ref[s]
        y_ref[s] = (_head_rms(y[s], on_ref[...]) * (g * _sigmoid(g))).astype(BF16)

    @pl.when(j == pl.num_programs(1) - 1)
    def _():
        for s in seqs:
            sout_ref[s] = _unpack_heads(s_scr[s])


def _mlstm_kernel(q_ref, k_ref, v_ref, o_ref, misc_ref, bif_ref, c0_ref, n0_ref, m0_ref, on_ref,
                  y_ref, cout_ref, nout_ref, mout_ref, c_scr, n_scr, m_scr):
    nb, lc, _ = q_ref.shape
    seqs, hs = range(nb), range(N_HEADS)
    pairs = [(s, h) for s in seqs for h in hs]
    j = pl.program_id(1)

    @pl.when(j == 0)
    def _():
        for s in seqs:
            c_scr[s] = _pack_heads(c0_ref[s])
            n_scr[s] = n0_ref[s]
            m_scr[s] = m0_ref[s]

    lane_head = _lane_head(GROUP, HEAD_DIM)
    lane128 = lax.broadcasted_iota(jnp.int32, (lc, 128), 1)
    si = lax.broadcasted_iota(jnp.int32, (lc, lc), 0)
    ti = lax.broadcasted_iota(jnp.int32, (lc, lc), 1)
    causal = si <= ti
    row8 = lax.broadcasted_iota(jnp.int32, (8, GROUP), 0)
    row8s = lax.broadcasted_iota(jnp.int32, (8, lc), 0)

    q = [q_ref[s] for s in seqs]
    kb = [(k_ref[s] * (HEAD_DIM ** -0.5)).astype(BF16) for s in seqs]
    v = [v_ref[s] for s in seqs]
    vb = [v[s].astype(BF16) for s in seqs]
    gates = [misc_ref[s] + bif_ref[...] for s in seqs]
    bh_all = [_cumsum_rows(-_softplus(-gates[s])) for s in seqs]
    src_all = [gates[s] - pltpu.roll(bh_all[s], 128 - N_HEADS, 1) for s in seqs]
    gates_t = [gates[s].T for s in seqs]
    bh_t = [bh_all[s].T for s in seqs]
    c_old, n_old, m_old = [c_scr[s] for s in seqs], [n_scr[s] for s in seqs], [m_scr[s] for s in seqs]
    src3 = [[t.astype(F32) for t in _split3(src_all[s])] for s in seqs]
    bh3 = [[t.astype(F32) for t in _split3(bh_all[s])] for s in seqs]

    def decay_logits(s, h):
        a, b = lane128 == MISC_IG + h, lane128 == MISC_FG + h
        lhs = jnp.concatenate([jnp.where(a, t, jnp.where(b, 1.0, 0.0)) for t in src3[s]], axis=1)
        rhs = jnp.concatenate([jnp.where(a, 1.0, jnp.where(b, t, 0.0)) for t in bh3[s]], axis=1)
        return _dot_nt(lhs, rhs)

    ig = {(s, h): gates_t[s][MISC_IG + h:MISC_IG + h + 1, :] for s, h in pairs}
    bh = {(s, h): bh_t[s][MISC_FG + h:MISC_FG + h + 1, :] for s, h in pairs}
    m_prev = {(s, h): m_old[s][:, h:h + 1] for s, h in pairs}
    dm = {p: jnp.where(causal, decay_logits(*p), -jnp.inf) for p in pairs}
    kq = {(s, h): _dot_nt(kb[s], jnp.where(lane_head == h, q[s], 0.0)) for s, h in pairs}
    m_state = {p: bh[p] + m_prev[p] for p in pairs}
    m_t = {p: jnp.maximum(m_state[p], jnp.max(dm[p], axis=0, keepdims=True)) for p in pairs}
    sc = {p: kq[p] * jnp.exp(dm[p] - m_t[p]) for p in pairs}
    g = {p: jnp.exp(m_state[p] - m_t[p]) for p in pairs}
    qn = [_dot_nt(jnp.where(row8 == lane_head, n_old[s], 0.0), q[s]) for s in seqs]
    den = {(s, h): jnp.sum(sc[s, h], axis=0, keepdims=True) + g[s, h] * qn[s][h:h + 1, :] for s, h in pairs}
    q_c = [_dot_nt(c_old[s], q[s]) for s in seqs]
    num = {(s, h): _dot_tn(vb[s], sc[s, h]) for s, h in pairs}
    for s in seqs:
        parts = []
        for h in hs:
            rows = slice(HEAD_DIM * h, HEAD_DIM * (h + 1))
            parts.append((num[s, h][rows, :] + g[s, h] * q_c[s][rows, :])
                         / jnp.maximum(jnp.abs(den[s, h]), jnp.exp(-m_t[s, h])))
        hh = jnp.concatenate(parts, axis=0).T
        y_ref[s] = (_head_rms(hh, on_ref[...]) * _sigmoid(o_ref[s])).astype(BF16)

    m_new = {p: m_t[p][:, lc - 1:lc] for p in pairs}
    bh_last = {p: bh[p][:, lc - 1:lc] for p in pairs}
    ws = {p: jnp.exp(bh_last[p] - bh[p] + ig[p] - m_new[p]) for p in pairs}
    for s in seqs:
        gl_row = jnp.zeros((1, GROUP), F32)
        m_new_row = m_old[s]
        ws8 = jnp.zeros((8, lc), F32)
        for h in hs:
            gl_row = jnp.where(lane_head == h, jnp.exp(bh_last[s, h] + m_prev[s, h] - m_new[s, h]), gl_row)
            m_new_row = jnp.where(lane128[0:1, :] == h, m_new[s, h], m_new_row)
            ws8 = jnp.where(row8s == h, ws[s, h], ws8)
        n_mat = _dot(ws8, kb[s])
        n_upd = jnp.zeros((1, GROUP), F32)
        for h in hs:
            n_upd = jnp.where(lane_head == h, n_mat[h:h + 1, :], n_upd)
        w_rows = jnp.concatenate([jnp.broadcast_to(ws[s, h], (HEAD_DIM, lc)) for h in hs], axis=0)
        c_scr[s] = c_old[s] * gl_row + jnp.where(_same_head(GROUP), _dot(v[s].T * w_rows, kb[s]), 0.0)
        n_new = gl_row * n_old[s] + n_upd
        n_scr[s] = n_new
        m_scr[s] = m_new_row
        nout_ref[s] = n_new
        mout_ref[s] = m_new_row

    @pl.when(j == pl.num_programs(1) - 1)
    def _():
        for s in seqs:
            cout_ref[s] = _unpack_heads(c_scr[s])


FFN_CHUNKS = ((0, 1024), (1024, 2048), (2048, FFN_HIDDEN))


def _out_kernel(xp_ref, xs_ref, *refs, n_p):
    yp_refs, ys_refs = refs[0:4], refs[4:8]
    wo_ref, nf_ref, wgu_ref, wd_ref, op_ref, os_ref = refs[8:]
    is_prompt = pl.program_id(0) < n_p
    y = jnp.concatenate([jnp.where(is_prompt, a[...], b[...]) for a, b in zip(yp_refs, ys_refs)], axis=1)
    x1 = jnp.where(is_prompt, xp_ref[...], xs_ref[...]) + jnp.dot(y, wo_ref[...], preferred_element_type=F32)
    h = _rms_rows(x1, nf_ref[...]).astype(BF16)
    ffn = None
    for lo, hi in FFN_CHUNKS:
        g = jnp.dot(h, wgu_ref[:, lo:hi], preferred_element_type=F32)
        u = jnp.dot(h, wgu_ref[:, FFN_HIDDEN + lo:FFN_HIDDEN + hi], preferred_element_type=F32)
        a = (g * _sigmoid(g) * u).astype(BF16)
        d = jnp.dot(a, wd_ref[lo:hi, :], preferred_element_type=F32)
        ffn = d if ffn is None else ffn + d

    @pl.when(is_prompt)
    def _():
        op_ref[...] = x1 + ffn

    @pl.when(jnp.logical_not(is_prompt))
    def _():
        os_ref[...] = x1 + ffn


def _out_proj_ffn(x_p, x_s, ys_p, ys_s, lw, layer, tm):
    n_p, n_s = x_p.shape[0] // tm, x_s.shape[0] // tm
    yp_spec, ys_spec = _two_source(n_p, tm, GROUP)
    xp_spec, xs_spec = _two_source(n_p, tm, D_MODEL)
    return pl.pallas_call(
        functools.partial(_out_kernel, n_p=n_p),
        grid=(n_p + n_s,),
        in_specs=[xp_spec, xs_spec, *[yp_spec] * 4, *[ys_spec] * 4,
                  pl.BlockSpec((None, D_MODEL, D_MODEL), lambda i: (layer, 0, 0)),
                  pl.BlockSpec((None, 1, D_MODEL), lambda i: (layer, 0, 0)),
                  pl.BlockSpec((None, D_MODEL, 2 * FFN_HIDDEN), lambda i: (layer, 0, 0)),
                  pl.BlockSpec((None, FFN_HIDDEN, D_MODEL), lambda i: (layer, 0, 0))],
        out_specs=[xp_spec, xs_spec],
        out_shape=[jax.ShapeDtypeStruct(x_p.shape, F32), jax.ShapeDtypeStruct(x_s.shape, F32)],
        compiler_params=pltpu.CompilerParams(dimension_semantics=("arbitrary",),
                                             vmem_limit_bytes=VMEM_LIMIT),
        name="out_proj_ffn",
    )(x_p, x_s, *ys_p, *ys_s, lw['w_out'], lw['norm_ffn'], lw['w_gu'], lw['w_down'])


def _seq_params():
    return pltpu.CompilerParams(dimension_semantics=("arbitrary", "arbitrary"),
                                vmem_limit_bytes=VMEM_LIMIT)


def _row_block(row0, seq, tb):
    base, per_seq = row0 // tb, seq // tb
    return lambda col: (lambda b, j: (base + b * per_seq + j, col))


def _layer_spec(layer, shape):
    return pl.BlockSpec((None,) + shape, lambda b, j: (layer,) + (0,) * len(shape))


def _state_spec(state_layer, shape):
    return pl.BlockSpec((None, None) + shape, lambda b, j: (state_layer, b) + (0,) * len(shape))


def _lru_call(proj, conv0, h0, state_layer, lw, layer, nseq, seq, row0, tb):
    rb, ob = _row_block(row0, seq, tb), _row_block(0, seq, tb)
    wl = functools.partial(_layer_spec, layer)
    st = functools.partial(_state_spec, state_layer)
    out_state = lambda shape: pl.BlockSpec((None,) + shape, lambda b, j: (b,) + (0,) * len(shape))
    return pl.pallas_call(
        _lru_kernel,
        grid=(nseq, seq // tb),
        in_specs=[pl.BlockSpec((tb, GROUP), rb(COL_LRU_X // GROUP)),
                  pl.BlockSpec((tb, GROUP), rb(COL_LRU_G // GROUP)),
                  st((8, GROUP)), st((1, GROUP)),
                  wl((CONV_W, GROUP)), wl((1, GROUP)), wl((GROUP, 2 * GROUP)), wl((1, 2 * GROUP)),
                  wl((1, GROUP)), wl((1, GROUP))],
        out_specs=[pl.BlockSpec((tb, GROUP), ob(0)), out_state((1, GROUP)), out_state((8, GROUP))],
        out_shape=[jax.ShapeDtypeStruct((nseq * seq, GROUP), BF16),
                   jax.ShapeDtypeStruct((nseq, 1, GROUP), F32),
                   jax.ShapeDtypeStruct((nseq, 8, GROUP), F32)],
        scratch_shapes=[pltpu.VMEM((8, GROUP), F32), pltpu.VMEM((1, GROUP), F32)],
        compiler_params=_seq_params(),
        name="rglru",
    )(proj, proj, conv0, h0, lw['conv_w'], lw['conv_b'], lw['lru_wg'], lw['lru_bg'], lw['lru_lambda'], lw['on_a'])


def _mla_pre_call(proj_p, proj_s, tabs, lw, layer, tm, nseq_p, seq_p):
    rows_p = nseq_p * seq_p
    rows = rows_p + proj_s.shape[0]
    n_p = rows_p // tm

    def cols(width, col):
        return (pl.BlockSpec((tm, width), lambda i: (jnp.minimum(i, n_p - 1), col // width)),
                pl.BlockSpec((tm, width), lambda i: (jnp.maximum(i - n_p, 0), col // width)))

    pos_blocks = seq_p // tm
    wl = lambda shape: pl.BlockSpec((None,) + shape, lambda i: (layer,) + (0,) * len(shape))
    tab = pl.BlockSpec((tm, 128), lambda i: (jnp.where(i < n_p, i % pos_blocks, pos_blocks + i - n_p), 0))
    ckv_p, ckv_s = _two_source(n_p, tm, KV_RANK)
    _, kr_s = _two_source(n_p, tm, ROPE_DIM)

    def kr_p_map(i):
        t = jnp.minimum(i, n_p - 1)
        return (t // pos_blocks, 0, t % pos_blocks)

    kr_p = pl.BlockSpec((None, ROPE_DIM, tm), kr_p_map)
    return pl.pallas_call(
        functools.partial(_mla_pre_kernel, n_p=n_p),
        grid=(rows // tm,),
        in_specs=[*cols(GROUP, COL_Q_LAT), *cols(128, COL_KV_LAT), *cols(128, COL_MISC),
                  tab, tab, tab,
                  wl((1, GROUP)), wl((GROUP, 1024)), wl((1, 1024)), wl((1, KV_RANK)), wl((1, 128)),
                  wl((KV_RANK, 1024)), wl((1, 512))],
        out_specs=[pl.BlockSpec((tm, 512), lambda i: (i, 0))] * 3 + [ckv_p, ckv_s, kr_p, kr_s],
        out_shape=[jax.ShapeDtypeStruct((rows, 512), BF16)] * 3
                  + [jax.ShapeDtypeStruct((rows_p, KV_RANK), F32), jax.ShapeDtypeStruct((rows - rows_p, KV_RANK), F32),
                     jax.ShapeDtypeStruct((nseq_p, ROPE_DIM, seq_p), F32),
                     jax.ShapeDtypeStruct((rows - rows_p, ROPE_DIM), F32)],
        compiler_params=pltpu.CompilerParams(dimension_semantics=("arbitrary",),
                                             vmem_limit_bytes=VMEM_LIMIT),
        name="mla_pre",
    )(proj_p, proj_s, proj_p, proj_s, proj_p, proj_s, *tabs, lw['q_norm'], lw['wq'], lw['gq'], lw['kv_norm'], lw['kr_norm'],
      lw['wkv'], lw['gk'])


def _kv_past_call(ckv, krope_t, lw, tr):
    depth, rows, _ = ckv.shape
    per_seq = krope_t.shape[3] // tr
    wl = lambda shape: pl.BlockSpec((None,) + shape, lambda l, i: (l,) + (0,) * len(shape))
    return pl.pallas_call(
        _kv_past_kernel,
        grid=(depth, rows // tr),
        in_specs=[pl.BlockSpec((None, tr, KV_RANK), lambda l, i: (l, i, 0)),
                  pl.BlockSpec((None, None, ROPE_DIM, tr), lambda l, i: (l, i // per_seq, 0, i % per_seq)),
                  wl((KV_RANK, 1024)), wl((1, 512))],
        out_specs=[pl.BlockSpec((None, tr, 512), lambda l, i: (l, i, 0))] * 2,
        out_shape=[jax.ShapeDtypeStruct((depth, rows, 512), BF16)] * 2,
        compiler_params=_seq_params(),
        name="kv_past",
    )(ckv, krope_t, lw['wkv'], lw['gk'])


def _attn_call(q, k, v, k_past, v_past, lw, layer, nseq, seq, row0, tq, tka, nb):
    base, per_seq = row0 // tq, seq // tq
    qspec = lambda s: pl.BlockSpec((tq, 512), lambda i, j: (base + (nb * i + s) * per_seq + j, 0))
    if k_past is None:
        assert row0 == 0
        ka, va = k, v
        past_spec = pl.BlockSpec((nb * seq, 512), lambda i, j: (i, 0))
        n_past_static = None
    else:
        ka, va = k_past, v_past
        past_len = k_past.shape[1] // nseq
        past_spec = pl.BlockSpec((None, nb * past_len, 512), lambda i, j: (layer, i, 0))
        n_past_static = past_len // tka
    own = [qspec(s) for s in range(nb)]
    return pl.pallas_call(
        functools.partial(_attn_kernel, nb=nb, tka=tka, n_past_static=n_past_static),
        grid=(nseq // nb, seq // tq),
        in_specs=[*own, past_spec, past_spec, *own, *own, _layer_spec(layer, (1, GROUP))],
        out_specs=pl.BlockSpec((nb, tq, GROUP), lambda i, j: (i, j, 0)),
        out_shape=jax.ShapeDtypeStruct((nseq, seq, GROUP), BF16),
        scratch_shapes=[pltpu.VMEM((nb * N_HEADS, 1, tq), F32), pltpu.VMEM((nb * N_HEADS, 128, tq), F32)],
        compiler_params=_seq_params(),
        name="mla_attn",
    )(*[q] * nb, ka, va, *[k] * nb, *[v] * nb, lw['on_b'])


def _seq_block(nb, shape):
    return pl.BlockSpec((nb,) + shape, lambda i, j: (i,) + (0,) * len(shape))


def _ret_call(proj, tabs, tab_map, s0, state_layer, lw, layer, lc, nb):
    nseq, seq, _ = proj.shape
    blk = lambda col: pl.BlockSpec((nb, lc, GROUP), lambda i, j: (i, j, col // GROUP))
    tab = pl.BlockSpec((lc, 128), tab_map)
    return pl.pallas_call(
        _ret_kernel,
        grid=(nseq // nb, seq // lc),
        in_specs=[blk(COL_R_Q), blk(COL_R_K), blk(COL_R_V), blk(COL_R_G), tab, tab, tab,
                  pl.BlockSpec((None, nb, GROUP, HEAD_DIM), lambda i, j: (state_layer, i, 0, 0)),
                  _layer_spec(layer, (1, GROUP))],
        out_specs=[pl.BlockSpec((nb, lc, GROUP), lambda i, j: (i, j, 0)), _seq_block(nb, (GROUP, HEAD_DIM))],
        out_shape=[jax.ShapeDtypeStruct((nseq, seq, GROUP), BF16),
                   jax.ShapeDtypeStruct((nseq, GROUP, HEAD_DIM), F32)],
        scratch_shapes=[pltpu.VMEM((nb, GROUP, GROUP), F32), pltpu.VMEM((N_HEADS, lc, lc), F32),
                        pltpu.VMEM((lc, GROUP), F32), pltpu.VMEM((lc, GROUP), F32)],
        compiler_params=_seq_params(),
        name="retention",
    )(proj, proj, proj, proj, *tabs, s0, lw['on_c'])


def _mlstm_call(proj, c0, n0, m0, state_layer, lw, layer, lc, nb):
    nseq, seq, _ = proj.shape
    blk = lambda col: pl.BlockSpec((nb, lc, GROUP), lambda i, j: (i, j, col // GROUP))
    st = lambda shape: pl.BlockSpec((None, nb) + shape, lambda i, j: (state_layer, i) + (0,) * len(shape))
    return pl.pallas_call(
        _mlstm_kernel,
        grid=(nseq // nb, seq // lc),
        in_specs=[blk(COL_M_Q), blk(COL_M_K), blk(COL_M_V), blk(COL_M_O),
                  pl.BlockSpec((nb, lc, 128), lambda i, j: (i, j, COL_MISC // 128)), _layer_spec(layer, (1, 128)),
                  st((GROUP, HEAD_DIM)), st((1, GROUP)), st((1, 128)), _layer_spec(layer, (1, GROUP))],
        out_specs=[pl.BlockSpec((nb, lc, GROUP), lambda i, j: (i, j, 0)), _seq_block(nb, (GROUP, HEAD_DIM)),
                   _seq_block(nb, (1, GROUP)), _seq_block(nb, (1, 128))],
        out_shape=[jax.ShapeDtypeStruct((nseq, seq, GROUP), BF16),
                   jax.ShapeDtypeStruct((nseq, GROUP, HEAD_DIM), F32),
                   jax.ShapeDtypeStruct((nseq, 1, GROUP), F32),
                   jax.ShapeDtypeStruct((nseq, 1, 128), F32)],
        scratch_shapes=[pltpu.VMEM((nb, GROUP, GROUP), F32), pltpu.VMEM((nb, 1, GROUP), F32),
                        pltpu.VMEM((nb, 1, 128), F32)],
        compiler_params=_seq_params(),
        name="mlstm",
    )(proj, proj, proj, proj, proj, lw['b_if'], c0, n0, m0, lw['on_d'])


def _block_diag(s):
    h, d, e = s.shape[-3:]
    eye = jnp.eye(h, dtype=s.dtype)
    return (s[..., :, :, None, :] * eye[:, None, :, None]).reshape(s.shape[:-3] + (h * d, h * e))


def _rope_tables(pos, half, lanes, lo):
    inv = ROPE_THETA ** (-jnp.arange(half, dtype=F32) / half)
    ang = pos[:, None] * inv[None, :]
    cos, sin = jnp.cos(ang), jnp.sin(ang)
    n = pos.shape[0]
    c = jnp.ones((n, lanes), F32).at[:, lo:lo + 2 * half].set(jnp.concatenate([cos, cos], axis=1))
    sp = jnp.zeros((n, lanes), F32).at[:, lo + half:lo + 2 * half].set(sin)
    sm = jnp.zeros((n, lanes), F32).at[:, lo:lo + half].set(-sin)
    return c, sp, sm


def _prep_weights(norm_mix, w_in, lru_conv_w, lru_conv_b, lru_wa, lru_ba, lru_wx, lru_bx, lru_lambda,
                  mla_q_norm, mla_wq_b, mla_qn_norm, mla_qr_norm, mla_kv_norm, mla_kr_norm, mla_wkv_b,
                  mla_kn_norm, mlstm_b_if, out_norm, w_out, norm_ffn, w_gu, w_down):
    depth = w_in.shape[0]
    row = lambda a: a.reshape(depth, 1, -1)
    n_if = 2 * N_HEADS
    w_in_pad = _w_in_prep(jnp.swapaxes(w_in, 1, 2))

    lru_wg = jnp.concatenate([_block_diag(lru_wa), _block_diag(lru_wx)], axis=2).astype(BF16)
    lru_bg = jnp.concatenate([lru_ba, lru_bx], axis=1)

    wq = mla_wq_b.reshape(depth, GROUP, N_HEADS, HEAD_DIM + ROPE_DIM)
    wq = jnp.pad(wq, ((0, 0), (0, 0), (0, 0), (0, 128 - HEAD_DIM - ROPE_DIM))).reshape(depth, GROUP, 512)
    gq = jnp.concatenate([mla_qn_norm, mla_qr_norm, jnp.zeros((depth, 32), F32)], axis=1)
    gq = jnp.tile(gq, (1, N_HEADS))
    lane = np.arange(512)
    in_rope = (lane % 128 >= HEAD_DIM) & (lane % 128 < HEAD_DIM + ROPE_DIM)
    first_half = (lane % 128 - HEAD_DIM) < ROPE_DIM // 2
    swap = np.where(in_rope, np.where(first_half, lane + ROPE_DIM // 2, lane - ROPE_DIM // 2), lane)
    wq = jnp.concatenate([wq, wq[:, :, swap]], axis=2)
    gq = jnp.concatenate([gq, gq[:, swap]], axis=1)
    wkv = mla_wkv_b.reshape(depth, KV_RANK, N_HEADS, 2 * HEAD_DIM)
    pad_head = lambda w: jnp.pad(w, ((0, 0), (0, 0), (0, 0), (0, 128 - HEAD_DIM))).reshape(depth, KV_RANK, 512)
    wkv = jnp.concatenate([pad_head(wkv[..., :HEAD_DIM]), pad_head(wkv[..., HEAD_DIM:])], axis=2)
    gk = jnp.tile(jnp.concatenate([mla_kn_norm, jnp.zeros((depth, 128 - HEAD_DIM), F32)], axis=1), (1, N_HEADS))
    kr_norm = jnp.pad(mla_kr_norm, ((0, 0), (0, 128 - ROPE_DIM)))
    b_if = jnp.pad(mlstm_b_if, ((0, 0), (MISC_IG, 128 - MISC_IG - n_if)))

    return dict(
        norm_mix=row(norm_mix), w_in=w_in_pad,
        conv_w=lru_conv_w, conv_b=row(lru_conv_b), lru_wg=lru_wg, lru_bg=row(lru_bg), lru_lambda=row(lru_lambda),
        q_norm=row(mla_q_norm), wq=wq.astype(BF16), gq=row(gq), kv_norm=row(mla_kv_norm), kr_norm=row(kr_norm),
        wkv=wkv.astype(BF16), gk=row(gk), b_if=row(b_if),
        on_a=row(out_norm[:, :GROUP]), on_b=row(out_norm[:, GROUP:2 * GROUP]),
        on_c=row(out_norm[:, 2 * GROUP:3 * GROUP]), on_d=row(out_norm[:, 3 * GROUP:]),
        w_out=w_out.astype(BF16), norm_ffn=row(norm_ffn), w_gu=w_gu.astype(BF16), w_down=w_down.astype(BF16))


def _tile_rows(rows, cap):
    t = cap
    while rows % t:
        t //= 2
    return t


def kernel(x_prompt, x_sample, cache_mla_ckv, cache_mla_krope, state_lru_h, state_lru_conv, state_ret, state_mlstm_C, state_mlstm_n, state_mlstm_m, norm_mix, w_in, lru_conv_w, lru_conv_b, lru_wa, lru_ba, lru_wx, lru_bx, lru_lambda, mla_q_norm, mla_wq_b, mla_qn_norm, mla_qr_norm, mla_kv_norm, mla_kr_norm, mla_wkv_b, mla_kn_norm, mlstm_b_if, out_norm, w_out, norm_ffn, w_gu, w_down):
    bp, tp, _ = x_prompt.shape
    bs, ts, _ = x_sample.shape
    depth, _, past, _ = cache_mla_ckv.shape
    rows_p, rows_s = bp * tp, bs * ts
    tm = math.gcd(_tile_rows(tp, 512), rows_s)
    tb_p, tb_s = min(tp, 512), min(ts, 512)
    lc_p, lc_s = min(tp, 256), min(ts, 256)
    tq_p, tq_s = min(tp, 512), min(ts, 512)
    tka_s = past
    nb_p, nb_s = math.gcd(bp, 4), math.gcd(bs, 4)
    nb_att = math.gcd(bp, 2)
    assert past % CHUNK == 0 and ts % CHUNK == 0 and tp % tm == 0 and rows_s % tm == 0

    lw = _prep_weights(norm_mix, w_in, lru_conv_w, lru_conv_b, lru_wa, lru_ba, lru_wx, lru_bx, lru_lambda,
                       mla_q_norm, mla_wq_b, mla_qn_norm, mla_qr_norm, mla_kv_norm, mla_kr_norm, mla_wkv_b,
                       mla_kn_norm, mlstm_b_if, out_norm, w_out, norm_ffn, w_gu, w_down)

    pos = jnp.concatenate([jnp.arange(tp, dtype=F32), jnp.tile(jnp.arange(past, past + ts, dtype=F32), bs)])
    tabs_mla = _rope_tables(pos, ROPE_DIM // 2, 128, HEAD_DIM)
    tabs_ret = tuple(jnp.concatenate([t, t], axis=1) for t in _rope_tables(pos, HEAD_DIM // 2, HEAD_DIM, 0))

    k_past, v_past = _kv_past_call(cache_mla_ckv.reshape(depth, bs * past, KV_RANK),
                                   jnp.swapaxes(cache_mla_krope, 2, 3), lw,
                                   _tile_rows(past, 1024))

    zeros = lambda *shape: jnp.zeros((1, bp) + shape, F32)
    st_p = dict(conv=zeros(8, GROUP), h=zeros(1, GROUP), s=zeros(GROUP, HEAD_DIM), c=zeros(GROUP, HEAD_DIM),
                n=zeros(1, GROUP), m=zeros(1, 128))
    st_s = dict(conv=jnp.pad(state_lru_conv, ((0, 0), (0, 0), (8 - (CONV_W - 1), 0), (0, 0))),
                h=state_lru_h[:, :, None, :], s=state_ret.reshape(depth, bs, GROUP, HEAD_DIM),
                c=state_mlstm_C.reshape(depth, bs, GROUP, HEAD_DIM),
                n=state_mlstm_n.reshape(depth, bs, 1, GROUP),
                m=jnp.pad(state_mlstm_m, ((0, 0), (0, 0), (0, 128 - N_HEADS)))[:, :, None, :])

    x_p = x_prompt.reshape(rows_p, D_MODEL)
    x_s = x_sample.reshape(rows_s, D_MODEL)
    acc = {name: [] for name in ('ckv_p', 'ckv_s', 'kr_p', 'kr_s', 'h_p', 'h_s', 'conv_p', 'conv_s',
                                 's_p', 's_s', 'c_p', 'c_s', 'n_p', 'n_s', 'm_p', 'm_s')}
    for l in range(depth):
        proj_p, proj_s = _in_proj(x_p, x_s, lw, l, tm)
        proj3_p, proj3_s = proj_p.reshape(bp, tp, IN_PAD), proj_s.reshape(bs, ts, IN_PAD)
        ya_p, h_p, conv_p = _lru_call(proj_p, st_p['conv'], st_p['h'], 0, lw, l, bp, tp, 0, tb_p)
        ya_s, h_s, conv_s = _lru_call(proj_s, st_s['conv'], st_s['h'], l, lw, l, bs, ts, 0, tb_s)
        q, k, v, ckv_p, ckv_s, kr_p, kr_s = _mla_pre_call(proj_p, proj_s, tabs_mla, lw, l, tm, bp, tp)
        yb_p = _attn_call(q, k, v, None, None, lw, l, bp, tp, 0, tq_p, tq_p, nb_att)
        yb_s = _attn_call(q, k, v, k_past, v_past, lw, l, bs, ts, rows_p, tq_s, tka_s, 1)
        yc_p, s_p = _ret_call(proj3_p, tabs_ret, lambda i, j: (j, 0), st_p['s'], 0, lw, l, lc_p, nb_p)
        yc_s, s_s = _ret_call(proj3_s, tabs_ret, lambda i, j: (tp // lc_s + j, 0), st_s['s'], l, lw, l, lc_s, nb_s)
        yd_p, c_p, n_p, m_p = _mlstm_call(proj3_p, st_p['c'], st_p['n'], st_p['m'], 0, lw, l, lc_p, nb_p)
        yd_s, c_s, n_s, m_s = _mlstm_call(proj3_s, st_s['c'], st_s['n'], st_s['m'], l, lw, l, lc_s, nb_s)
        flat = lambda y: y.reshape(-1, GROUP)
        x_p, x_s = _out_proj_ffn(x_p, x_s, (ya_p, flat(yb_p), flat(yc_p), flat(yd_p)),
                                 (ya_s, flat(yb_s), flat(yc_s), flat(yd_s)), lw, l, tm)
        for name, val in (('ckv_p', ckv_p), ('ckv_s', ckv_s), ('kr_p', kr_p), ('kr_s', kr_s), ('h_p', h_p),
                          ('h_s', h_s), ('conv_p', conv_p), ('conv_s', conv_s), ('s_p', s_p), ('s_s', s_s),
                          ('c_p', c_p), ('c_s', c_s), ('n_p', n_p), ('n_s', n_s), ('m_p', m_p), ('m_s', m_s)):
            acc[name].append(val)

    st = {name: jnp.stack(vals) for name, vals in acc.items()}

    def outputs(tag, b, t):
        krope = jnp.swapaxes(st['kr_p'], 2, 3) if tag == 'p' else st['kr_s'].reshape(depth, b, t, ROPE_DIM)
        return (st['ckv_' + tag].reshape(depth, b, t, KV_RANK), krope,
                st['h_' + tag][:, :, 0], st['conv_' + tag][:, :, 8 - (CONV_W - 1):],
                st['s_' + tag].reshape(depth, b, N_HEADS, HEAD_DIM, HEAD_DIM),
                st['c_' + tag].reshape(depth, b, N_HEADS, HEAD_DIM, HEAD_DIM),
                st['n_' + tag].reshape(depth, b, N_HEADS, HEAD_DIM), st['m_' + tag][:, :, 0, :N_HEADS])

    return ((x_p.reshape(bp, tp, D_MODEL), x_s.reshape(bs, ts, D_MODEL))
            + outputs('p', bp, tp) + outputs('s', bs, ts))
```

```python
import functools
import math

import jax
import jax.numpy as jnp
import numpy as np
from jax import lax
from jax.experimental import pallas as pl
from jax.experimental.pallas import tpu as pltpu

F32 = jnp.float32
BF16 = jnp.bfloat16

D_MODEL = 1024
CHUNK = 64
HEAD_DIM = 64
GROUP = 256
N_HEADS = 4
RMS_EPS = 1e-6
ROPE_THETA = 10000.0
CONV_W = 4
LRU_C = 8.0
KV_RANK = 128
ROPE_DIM = 32
FFN_HIDDEN = 2816
IN_PAD = 3072

COL_LRU_X, COL_LRU_G, COL_Q_LAT = 0, 256, 512
COL_KV_LAT, COL_MISC = 768, 896
COL_R_Q, COL_R_K, COL_R_V, COL_R_G = 1024, 1280, 1536, 1792
COL_M_Q, COL_M_K, COL_M_V, COL_M_O = 2048, 2304, 2560, 2816
MISC_IG = 32
MISC_FG = MISC_IG + N_HEADS
ATT_SCALE = (HEAD_DIM + ROPE_DIM) ** -0.5
RET_LOG_DECAY = tuple(math.log(1.0 - 2.0 ** (-5.0 - h)) for h in range(N_HEADS))

VMEM_LIMIT = 56 * 1024 * 1024


def _dot(a, b):
    return jnp.dot(a.astype(BF16), b.astype(BF16), preferred_element_type=F32)


def _dot_nt(a, b):
    return lax.dot_general(a.astype(BF16), b.astype(BF16), (((1,), (1,)), ((), ())),
                           preferred_element_type=F32)


def _dot_tn(a, b):
    return lax.dot_general(a.astype(BF16), b.astype(BF16), (((0,), (0,)), ((), ())),
                           preferred_element_type=F32)


def _split3(x):
    hi = x.astype(BF16)
    r1 = x - hi.astype(F32)
    mid = r1.astype(BF16)
    lo = (r1 - mid.astype(F32)).astype(BF16)
    return hi, mid, lo


def _dot_f32_lhs(x, m, terms=2):
    parts = [jnp.dot(t, m, preferred_element_type=F32) for t in _split3(x)[:terms]]
    return functools.reduce(lambda a, b: a + b, parts)


def _pack_heads(s):
    r = lax.broadcasted_iota(jnp.int32, (HEAD_DIM, GROUP), 0)
    c = lax.broadcasted_iota(jnp.int32, (HEAD_DIM, GROUP), 1)
    tile = jnp.where(c % HEAD_DIM == r, 1.0, 0.0).astype(BF16)
    return jnp.where(_same_head(GROUP), _dot_f32_lhs(s, tile, terms=3), 0.0)


def _unpack_heads(s):
    r = lax.broadcasted_iota(jnp.int32, (GROUP, HEAD_DIM), 0)
    c = lax.broadcasted_iota(jnp.int32, (GROUP, HEAD_DIM), 1)
    fold = jnp.where(r % HEAD_DIM == c, 1.0, 0.0).astype(BF16)
    return _dot_f32_lhs(s, fold, terms=3)


def _rms_rows(x, g):
    return x * lax.rsqrt(jnp.mean(x * x, axis=-1, keepdims=True) + RMS_EPS) * g


def _lane_head(width, head_width):
    return lax.broadcasted_iota(jnp.int32, (1, width), 1) // head_width


def _same_head(n):
    r = lax.broadcasted_iota(jnp.int32, (n, n), 0) // HEAD_DIM
    c = lax.broadcasted_iota(jnp.int32, (n, n), 1) // HEAD_DIM
    return r == c


def _head_rms(y, g):
    mean_mat = jnp.where(_same_head(GROUP), 1.0 / HEAD_DIM, 0.0).astype(BF16)
    return y * lax.rsqrt(_dot_f32_lhs(y * y, mean_mat) + RMS_EPS) * g


def _shift_rows(x, d, fill):
    rows = lax.broadcasted_iota(jnp.int32, x.shape, 0)
    return jnp.where(rows >= d, pltpu.roll(x, d, 0), fill)


def _cumsum_rows(x):
    d = 1
    while d < x.shape[0]:
        x = x + _shift_rows(x, d, 0.0)
        d *= 2
    return x


def _softplus(z):
    return jnp.maximum(z, 0.0) + jnp.log1p(jnp.exp(-jnp.abs(z)))


def _sigmoid(z):
    return 0.5 * jnp.tanh(0.5 * z) + 0.5


def _gelu_tanh(z):
    return 0.5 * z * (1.0 + jnp.tanh(math.sqrt(2.0 / math.pi) * (z + 0.044715 * (z * z * z))))


def _rope_lanes(x, c, sp, sm, half):
    w = x.shape[1]
    return x * c + pltpu.roll(x, half, 1) * sp + pltpu.roll(x, w - half, 1) * sm


def _tile_lanes(t, n):
    return jnp.concatenate([t] * n, axis=1) if n > 1 else t


def _w_in_prep_kernel(wt_ref, o_ref):
    n_in, cols = wt_ref.shape
    n_if = 2 * N_HEADS
    split = COL_MISC + ROPE_DIM
    o_ref[:, :COL_MISC] = wt_ref[0:COL_MISC, :].T.astype(BF16)
    misc = jnp.concatenate([wt_ref[COL_MISC:split, :], wt_ref[n_in - n_if:n_in, :],
                            jnp.zeros((COL_R_Q - split - n_if, cols), F32)], axis=0)
    o_ref[:, COL_MISC:COL_R_Q] = misc.T.astype(BF16)
    o_ref[:, COL_R_Q:] = wt_ref[split:n_in - n_if, :].T.astype(BF16)


def _w_in_prep(w_in_t, cols=256):
    depth, n_in, d_model = w_in_t.shape
    return pl.pallas_call(
        _w_in_prep_kernel,
        grid=(depth, d_model // cols),
        in_specs=[pl.BlockSpec((None, n_in, cols), lambda l, i: (l, 0, i))],
        out_specs=pl.BlockSpec((None, cols, IN_PAD), lambda l, i: (l, i, 0)),
        out_shape=jax.ShapeDtypeStruct((depth, d_model, IN_PAD), BF16),
        compiler_params=_seq_params(),
        name="w_in_prep",
    )(w_in_t)


def _in_proj_kernel(xp_ref, xs_ref, g_ref, w_ref, op_ref, os_ref, *, n_p):
    is_prompt = pl.program_id(0) < n_p

    def project(x_ref, o_ref):
        o_ref[...] = jnp.dot(_rms_rows(x_ref[...], g_ref[...]).astype(BF16), w_ref[...],
                             preferred_element_type=F32)

    pl.when(is_prompt)(lambda: project(xp_ref, op_ref))
    pl.when(jnp.logical_not(is_prompt))(lambda: project(xs_ref, os_ref))


def _two_source(n_p, tm, width):
    return (pl.BlockSpec((tm, width), lambda i: (jnp.minimum(i, n_p - 1), 0)),
            pl.BlockSpec((tm, width), lambda i: (jnp.maximum(i - n_p, 0), 0)))


def _in_proj(x_p, x_s, lw, layer, tm):
    n_p, n_s = x_p.shape[0] // tm, x_s.shape[0] // tm
    return pl.pallas_call(
        functools.partial(_in_proj_kernel, n_p=n_p),
        grid=(n_p + n_s,),
        in_specs=[*_two_source(n_p, tm, D_MODEL),
                  pl.BlockSpec((None, 1, D_MODEL), lambda i: (layer, 0, 0)),
                  pl.BlockSpec((None, D_MODEL, IN_PAD), lambda i: (layer, 0, 0))],
        out_specs=list(_two_source(n_p, tm, IN_PAD)),
        out_shape=[jax.ShapeDtypeStruct((x_p.shape[0], IN_PAD), F32),
                   jax.ShapeDtypeStruct((x_s.shape[0], IN_PAD), F32)],
        compiler_params=pltpu.CompilerParams(dimension_semantics=("arbitrary",),
                                             vmem_limit_bytes=VMEM_LIMIT),
        name="in_proj",
    )(x_p, x_s, lw['norm_mix'], lw['w_in'])


def _lru_kernel(x_ref, g_ref, conv0_ref, h0_ref, cw_ref, cb_ref, wg_ref, bg_ref, lam_ref, on_ref,
                y_ref, hout_ref, convout_ref, xp_scr, h_scr):
    j = pl.program_id(1)
    tb = x_ref.shape[0]

    @pl.when(j == 0)
    def _():
        xp_scr[...] = conv0_ref[...]
        h_scr[...] = h0_ref[...]

    x = x_ref[...]
    prev = xp_scr[...]
    row8 = lax.broadcasted_iota(jnp.int32, (8, GROUP), 0)

    def delayed(d):
        r = pltpu.roll(x, d, 0)
        head = jnp.where(row8 < d, pltpu.roll(prev, d, 0), r[0:8, :])
        return jnp.concatenate([head, r[8:, :]], axis=0)

    cw = cw_ref[...]
    xc = cb_ref[...] + cw[3:4] * x + cw[2:3] * delayed(1) + cw[1:2] * delayed(2) + cw[0:1] * delayed(3)
    tail = x[tb - 8:tb, :]
    xp_scr[...] = tail
    convout_ref[...] = tail

    gates = _dot(xc, wg_ref[...]) + bg_ref[...]
    r = _sigmoid(gates[:, :GROUP])
    i = _sigmoid(gates[:, GROUP:])
    log_a = -LRU_C * r * _softplus(-lam_ref[...])
    a = jnp.exp(log_a)
    u = jnp.sqrt(-jnp.tanh(log_a) * (a * a + 1.0)) * i * xc

    d = 1
    while d < tb:
        u = a * _shift_rows(u, d, 0.0) + u
        a = a * _shift_rows(a, d, 1.0)
        d *= 2
    h = u + a * h_scr[...]
    h_last = h[tb - 1:tb, :]
    h_scr[...] = h_last
    hout_ref[...] = h_last
    y_ref[...] = (_head_rms(h, on_ref[...]) * _gelu_tanh(g_ref[...])).astype(BF16)


def _mla_seg_rms(x, g, rope):
    if rope:
        r = lax.broadcasted_iota(jnp.int32, (256, 256), 0)
        c = lax.broadcasted_iota(jnp.int32, (256, 256), 1)
        same = (r // 128) == (c // 128)
        rl, cl = r % 128, c % 128
        in_nope = same & (rl < HEAD_DIM) & (cl < HEAD_DIM)
        in_rope = same & (rl >= HEAD_DIM) & (cl >= HEAD_DIM)
        seg = jnp.where(in_nope, 1.0 / HEAD_DIM, jnp.where(in_rope, 1.0 / ROPE_DIM, 0.0)).astype(BF16)
        ms = jnp.concatenate([_dot_f32_lhs(jnp.square(x[:, 256 * p:256 * p + 256]), seg)
                              for p in range(N_HEADS // 2)], axis=1)
        return lax.rsqrt(ms + RMS_EPS) if g is None else x * lax.rsqrt(ms + RMS_EPS) * g
    outs = []
    for h in range(N_HEADS):
        xh = x[:, 128 * h:128 * h + 128]
        outs.append(xh * lax.rsqrt(jnp.sum(xh * xh, axis=-1, keepdims=True) * (1.0 / HEAD_DIM) + RMS_EPS))
    return jnp.concatenate(outs, axis=1) * g


def _with_ones_lane(v):
    lane = lax.broadcasted_iota(jnp.int32, v.shape, 1) % 128
    return jnp.where(lane == HEAD_DIM, 1.0, v)


def _mla_pre_kernel(qlat_p, qlat_s, kvlat_p, kvlat_s, misc_p, misc_s, c_ref, sp_ref, sm_ref,
                    qn_ref, wq_ref, gq_ref, kvn_ref, krn_ref, wkv_ref, gk_ref,
                    q_out, k_out, v_out, ckv_p, ckv_s, kr_p, kr_s, *, n_p):
    c, sp, sm = c_ref[...], sp_ref[...], sm_ref[...]
    half = ROPE_DIM // 2
    is_prompt = pl.program_id(0) < n_p
    qlat = jnp.where(is_prompt, qlat_p[...], qlat_s[...])
    kvlat = jnp.where(is_prompt, kvlat_p[...], kvlat_s[...])
    misc = jnp.where(is_prompt, misc_p[...], misc_s[...])

    qboth = _dot(_rms_rows(qlat, qn_ref[...]), wq_ref[...])
    qraw, qswap = qboth[:, :512], qboth[:, 512:]
    gq, gq_swap = gq_ref[:, :512], gq_ref[:, 512:]
    inv = _mla_seg_rms(qraw, None, rope=True)
    q = inv * (qraw * (gq * _tile_lanes(c, 4)) + qswap * (gq_swap * _tile_lanes(sp + sm, 4)))
    q_out[...] = q.astype(BF16)

    ckv = _rms_rows(kvlat, kvn_ref[...])
    lane = lax.broadcasted_iota(jnp.int32, misc.shape, 1)
    kr = jnp.where(lane < ROPE_DIM, misc, 0.0)
    kr = kr * lax.rsqrt(jnp.sum(kr * kr, axis=-1, keepdims=True) * (1.0 / ROPE_DIM) + RMS_EPS) * krn_ref[...]
    kr = _rope_lanes(pltpu.roll(kr, HEAD_DIM, 1), c, sp, sm, half)
    kr_new = pltpu.roll(kr, 128 - HEAD_DIM, 1)

    @pl.when(is_prompt)
    def _():
        ckv_p[...] = ckv
        kr_p[...] = kr_new.T[:ROPE_DIM, :]

    @pl.when(jnp.logical_not(is_prompt))
    def _():
        ckv_s[...] = ckv
        kr_s[...] = kr_new[:, :ROPE_DIM]

    kv = _dot(ckv, wkv_ref[...])
    kn = _mla_seg_rms(kv[:, :512], gk_ref[...], rope=False)
    k_out[...] = (kn + _tile_lanes(kr, 4)).astype(BF16)
    v_out[...] = _with_ones_lane(kv[:, 512:]).astype(BF16)


def _kv_past_kernel(ckv_ref, kr_ref, wkv_ref, gk_ref, k_out, v_out):
    kv = _dot(ckv_ref[...], wkv_ref[...])
    kn = _mla_seg_rms(kv[:, :512], gk_ref[...], rope=False)
    r = lax.broadcasted_iota(jnp.int32, (ROPE_DIM, 128), 0)
    cidx = lax.broadcasted_iota(jnp.int32, (ROPE_DIM, 128), 1)
    place = jnp.where(cidx == r + HEAD_DIM, 1.0, 0.0).astype(BF16)
    kr = _dot_tn(kr_ref[...], place)
    k_out[...] = (kn + _tile_lanes(kr, 4)).astype(BF16)
    v_out[...] = _with_ones_lane(kv[:, 512:]).astype(BF16)


def _attn_kernel(*refs, nb, tka, n_past_static):
    q_refs, (ka_ref, va_ref) = refs[:nb], refs[nb:nb + 2]
    kb_refs, vb_refs = refs[nb + 2:2 * nb + 2], refs[2 * nb + 2:3 * nb + 2]
    on_ref, y_ref, m_scr, acc_scr = refs[3 * nb + 2:]
    j = pl.program_id(1)
    tq = q_refs[0].shape[0]
    past_rows = ka_ref.shape[0] // nb
    n_past = j * (tq // tka) if n_past_static is None else n_past_static
    key_c = lax.broadcasted_iota(jnp.int32, (tq, tq), 0) // CHUNK
    qry_c = lax.broadcasted_iota(jnp.int32, (tq, tq), 1) // CHUNK
    visible = key_c <= qry_c
    pairs = [(s, h) for s in range(nb) for h in range(N_HEADS)]
    lanes = lambda h: slice(128 * h, 128 * h + 128)
    slot = lambda s, h: s * N_HEADS + h
    c = ATT_SCALE * math.log2(math.e)

    sc = {(s, h): jnp.where(visible, _dot_nt(kb_refs[s][:, lanes(h)], q_refs[s][:, lanes(h)]) * c, -jnp.inf)
          for s, h in pairs}
    m = {p: jnp.max(sc[p], axis=0, keepdims=True) for p in pairs}
    pr = {p: jnp.exp2(sc[p] - m[p]).astype(BF16) for p in pairs}
    pv = {(s, h): _dot_tn(vb_refs[s][:, lanes(h)], pr[s, h]) for s, h in pairs}
    for p in pairs:
        m_scr[slot(*p)] = m[p]
        acc_scr[slot(*p)] = pv[p]

    def body(t, carry):
        off = pl.multiple_of(t * tka, tka)
        rows = lambda s: pl.ds(s * past_rows + off, tka)
        m_old = {p: m_scr[slot(*p)] for p in pairs}
        sc = {(s, h): _dot_nt(ka_ref[rows(s), lanes(h)], q_refs[s][:, lanes(h)]) * c for s, h in pairs}
        m_new = {p: jnp.maximum(m_old[p], jnp.max(sc[p], axis=0, keepdims=True)) for p in pairs}
        pr = {p: jnp.exp2(sc[p] - m_new[p]).astype(BF16) for p in pairs}
        pv = {(s, h): _dot_tn(va_ref[rows(s), lanes(h)], pr[s, h]) for s, h in pairs}
        for p in pairs:
            acc_scr[slot(*p)] = jnp.exp2(m_old[p] - m_new[p]) * acc_scr[slot(*p)] + pv[p]
            m_scr[slot(*p)] = m_new[p]
        return carry

    lax.fori_loop(0, n_past, body, 0)
    for s in range(nb):
        outs = []
        for h in range(N_HEADS):
            a = acc_scr[slot(s, h)]
            outs.append(a[:HEAD_DIM, :] / a[HEAD_DIM:HEAD_DIM + 1, :])
        o = jnp.concatenate(outs, axis=0).T
        y_ref[s] = _head_rms(o, on_ref[...]).astype(BF16)


def _ret_kernel(q_ref, k_ref, v_ref, g_ref, c_ref, sp_ref, sm_ref, s0_ref, on_ref,
                y_ref, sout_ref, s_scr, dmat_scr, cross_scr, tail_scr):
    nb, lc, _ = q_ref.shape
    seqs, hs = range(nb), range(N_HEADS)
    b, j = pl.program_id(0), pl.program_id(1)
    lane_head = _lane_head(GROUP, HEAD_DIM)
    lg_lane = jnp.zeros((1, GROUP), F32)
    for h in hs:
        lg_lane = jnp.where(lane_head == h, RET_LOG_DECAY[h], lg_lane)

    @pl.when((b == 0) & (j == 0))
    def _():
        t_col = lax.broadcasted_iota(jnp.int32, (lc, 1), 0).astype(F32)
        cross_scr[...] = jnp.exp((t_col + 1.0) * lg_lane)
        tail_scr[...] = jnp.exp((lc - 1.0 - t_col) * lg_lane)
        ti = lax.broadcasted_iota(jnp.int32, (lc, lc), 0)
        si = lax.broadcasted_iota(jnp.int32, (lc, lc), 1)
        causal = ti >= si
        diff = jnp.where(causal, ti - si, 0).astype(F32)
        for h in hs:
            dmat_scr[h] = jnp.where(causal, jnp.exp(diff * RET_LOG_DECAY[h]), 0.0)

    @pl.when(j == 0)
    def _():
        for s in seqs:
            s_scr[s] = _pack_heads(s0_ref[s])

    c, sp, sm = (_tile_lanes(t[...], 2) for t in (c_ref, sp_ref, sm_ref))
    half = HEAD_DIM // 2
    q = [_rope_lanes(q_ref[s], c, sp, sm, half) for s in seqs]
    k = [_rope_lanes(k_ref[s], c, sp, sm, half) * (HEAD_DIM ** -0.5) for s in seqs]
    v = [v_ref[s].astype(BF16) for s in seqs]
    kb = [k[s].astype(BF16) for s in seqs]
    s_old = [s_scr[s] for s in seqs]
    y = [_dot(q[s], s_old[s]) * cross_scr[...] for s in seqs]
    att = [[_dot_nt(jnp.where(lane_head == h, q[s], 0.0), kb[s]) for h in hs] for s in seqs]
    att = [[(att[s][h] * dmat_scr[h]).astype(BF16) for h in hs] for s in seqs]
    yh = [[_dot(att[s][h], v[s]) for h in hs] for s in seqs]
    s_upd = [_dot_tn(k[s] * tail_scr[...], v[s]) for s in seqs]
    for s in seqs:
        for h in hs:
            y[s] = y[s] + jnp.where(lane_head == h, yh[s][h], 0.0)
        s_scr[s] = s_old[s] * jnp.exp(float(lc) * lg_lane) + jnp.where(_same_head(GROUP), s_upd[s], 0.0)
        g = g_ref[s]
        y_ref[s] = (_head_rms(y[s], on_ref[...]) * (g * _sigmoid(g))).astype(BF16)

    @pl.when(j == pl.num_programs(1) - 1)
    def _():
        for s in seqs:
            sout_ref[s] = _unpack_heads(s_scr[s])


def _mlstm_kernel(q_ref, k_ref, v_ref, o_ref, misc_ref, bif_ref, c0_ref, n0_ref, m0_ref, on_ref,
                  y_ref, cout_ref, nout_ref, mout_ref, c_scr, n_scr, m_scr):
    nb, lc, _ = q_ref.shape
    seqs, hs = range(nb), range(N_HEADS)
    pairs = [(s, h) for s in seqs for h in hs]
    j = pl.program_id(1)

    @pl.when(j == 0)
    def _():
        for s in seqs:
            c_scr[s] = _pack_heads(c0_ref[s])
            n_scr[s] = n0_ref[s]
            m_scr[s] = m0_ref[s]

    lane_head = _lane_head(GROUP, HEAD_DIM)
    lane128 = lax.broadcasted_iota(jnp.int32, (lc, 128), 1)
    si = lax.broadcasted_iota(jnp.int32, (lc, lc), 0)
    ti = lax.broadcasted_iota(jnp.int32, (lc, lc), 1)
    causal = si <= ti
    row8 = lax.broadcasted_iota(jnp.int32, (8, GROUP), 0)
    row8s = lax.broadcasted_iota(jnp.int32, (8, lc), 0)

    q = [q_ref[s] for s in seqs]
    kb = [(k_ref[s] * (HEAD_DIM ** -0.5)).astype(BF16) for s in seqs]
    v = [v_ref[s] for s in seqs]
    vb = [v[s].astype(BF16) for s in seqs]
    gates = [misc_ref[s] + bif_ref[...] for s in seqs]
    bh_all = [_cumsum_rows(-_softplus(-gates[s])) for s in seqs]
    src_all = [gates[s] - pltpu.roll(bh_all[s], 128 - N_HEADS, 1) for s in seqs]
    gates_t = [gates[s].T for s in seqs]
    bh_t = [bh_all[s].T for s in seqs]
    c_old, n_old, m_old = [c_scr[s] for s in seqs], [n_scr[s] for s in seqs], [m_scr[s] for s in seqs]
    src3 = [[t.astype(F32) for t in _split3(src_all[s])] for s in seqs]
    bh3 = [[t.astype(F32) for t in _split3(bh_all[s])] for s in seqs]

    def decay_logits(s, h):
        a, b = lane128 == MISC_IG + h, lane128 == MISC_FG + h
        lhs = jnp.concatenate([jnp.where(a, t, jnp.where(b, 1.0, 0.0)) for t in src3[s]], axis=1)
        rhs = jnp.concatenate([jnp.where(a, 1.0, jnp.where(b, t, 0.0)) for t in bh3[s]], axis=1)
        return _dot_nt(lhs, rhs)

    ig = {(s, h): gates_t[s][MISC_IG + h:MISC_IG + h + 1, :] for s, h in pairs}
    bh = {(s, h): bh_t[s][MISC_FG + h:MISC_FG + h + 1, :] for s, h in pairs}
    m_prev = {(s, h): m_old[s][:, h:h + 1] for s, h in pairs}
    dm = {p: jnp.where(causal, decay_logits(*p), -jnp.inf) for p in pairs}
    kq = {(s, h): _dot_nt(kb[s], jnp.where(lane_head == h, q[s], 0.0)) for s, h in pairs}
    m_state = {p: bh[p] + m_prev[p] for p in pairs}
    m_t = {p: jnp.maximum(m_state[p], jnp.max(dm[p], axis=0, keepdims=True)) for p in pairs}
    sc = {p: kq[p] * jnp.exp(dm[p] - m_t[p]) for p in pairs}
    g = {p: jnp.exp(m_state[p] - m_t[p]) for p in pairs}
    qn = [_dot_nt(jnp.where(row8 == lane_head, n_old[s], 0.0), q[s]) for s in seqs]
    den = {(s, h): jnp.sum(sc[s, h], axis=0, keepdims=True) + g[s, h] * qn[s][h:h + 1, :] for s, h in pairs}
    q_c = [_dot_nt(c_old[s], q[s]) for s in seqs]
    num = {(s, h): _dot_tn(vb[s], sc[s, h]) for s, h in pairs}
    for s in seqs:
        parts = []
        for h in hs:
            rows = slice(HEAD_DIM * h, HEAD_DIM * (h + 1))
            parts.append((num[s, h][rows, :] + g[s, h] * q_c[s][rows, :])
                         / jnp.maximum(jnp.abs(den[s, h]), jnp.exp(-m_t[s, h])))
        hh = jnp.concatenate(parts, axis=0).T
        y_ref[s] = (_head_rms(hh, on_ref[...]) * _sigmoid(o_ref[s])).astype(BF16)

    m_new = {p: m_t[p][:, lc - 1:lc] for p in pairs}
    bh_last = {p: bh[p][:, lc - 1:lc] for p in pairs}
    ws = {p: jnp.exp(bh_last[p] - bh[p] + ig[p] - m_new[p]) for p in pairs}
    for s in seqs:
        gl_row = jnp.zeros((1, GROUP), F32)
        m_new_row = m_old[s]
        ws8 = jnp.zeros((8, lc), F32)
        for h in hs:
            gl_row = jnp.where(lane_head == h, jnp.exp(bh_last[s, h] + m_prev[s, h] - m_new[s, h]), gl_row)
            m_new_row = jnp.where(lane128[0:1, :] == h, m_new[s, h], m_new_row)
            ws8 = jnp.where(row8s == h, ws[s, h], ws8)
        n_mat = _dot(ws8, kb[s])
        n_upd = jnp.zeros((1, GROUP), F32)
        for h in hs:
            n_upd = jnp.where(lane_head == h, n_mat[h:h + 1, :], n_upd)
        w_rows = jnp.concatenate([jnp.broadcast_to(ws[s, h], (HEAD_DIM, lc)) for h in hs], axis=0)
        c_scr[s] = c_old[s] * gl_row + jnp.where(_same_head(GROUP), _dot(v[s].T * w_rows, kb[s]), 0.0)
        n_new = gl_row * n_old[s] + n_upd
        n_scr[s] = n_new
        m_scr[s] = m_new_row
        nout_ref[s] = n_new
        mout_ref[s] = m_new_row

    @pl.when(j == pl.num_programs(1) - 1)
    def _():
        for s in seqs:
            cout_ref[s] = _unpack_heads(c_scr[s])


FFN_CHUNKS = ((0, 1024), (1024, 2048), (2048, FFN_HIDDEN))


def _out_kernel(xp_ref, xs_ref, *refs, n_p):
    yp_refs, ys_refs = refs[0:4], refs[4:8]
    wo_ref, nf_ref, wgu_ref, wd_ref, op_ref, os_ref = refs[8:]
    is_prompt = pl.program_id(0) < n_p
    y = jnp.concatenate([jnp.where(is_prompt, a[...], b[...]) for a, b in zip(yp_refs, ys_refs)], axis=1)
    x1 = jnp.where(is_prompt, xp_ref[...], xs_ref[...]) + jnp.dot(y, wo_ref[...], preferred_element_type=F32)
    h = _rms_rows(x1, nf_ref[...]).astype(BF16)
    ffn = None
    for lo, hi in FFN_CHUNKS:
        g = jnp.dot(h, wgu_ref[:, lo:hi], preferred_element_type=F32)
        u = jnp.dot(h, wgu_ref[:, FFN_HIDDEN + lo:FFN_HIDDEN + hi], preferred_element_type=F32)
        a = (g * _sigmoid(g) * u).astype(BF16)
        d = jnp.dot(a, wd_ref[lo:hi, :], preferred_element_type=F32)
        ffn = d if ffn is None else ffn + d

    @pl.when(is_prompt)
    def _():
        op_ref[...] = x1 + ffn

    @pl.when(jnp.logical_not(is_prompt))
    def _():
        os_ref[...] = x1 + ffn


def _out_proj_ffn(x_p, x_s, ys_p, ys_s, lw, layer, tm):
    n_p, n_s = x_p.shape[0] // tm, x_s.shape[0] // tm
    yp_spec, ys_spec = _two_source(n_p, tm, GROUP)
    xp_spec, xs_spec = _two_source(n_p, tm, D_MODEL)
    return pl.pallas_call(
        functools.partial(_out_kernel, n_p=n_p),
        grid=(n_p + n_s,),
        in_specs=[xp_spec, xs_spec, *[yp_spec] * 4, *[ys_spec] * 4,
                  pl.BlockSpec((None, D_MODEL, D_MODEL), lambda i: (layer, 0, 0)),
                  pl.BlockSpec((None, 1, D_MODEL), lambda i: (layer, 0, 0)),
                  pl.BlockSpec((None, D_MODEL, 2 * FFN_HIDDEN), lambda i: (layer, 0, 0)),
                  pl.BlockSpec((None, FFN_HIDDEN, D_MODEL), lambda i: (layer, 0, 0))],
        out_specs=[xp_spec, xs_spec],
        out_shape=[jax.ShapeDtypeStruct(x_p.shape, F32), jax.ShapeDtypeStruct(x_s.shape, F32)],
        compiler_params=pltpu.CompilerParams(dimension_semantics=("arbitrary",),
                                             vmem_limit_bytes=VMEM_LIMIT),
        name="out_proj_ffn",
    )(x_p, x_s, *ys_p, *ys_s, lw['w_out'], lw['norm_ffn'], lw['w_gu'], lw['w_down'])


def _seq_params():
    return pltpu.CompilerParams(dimension_semantics=("arbitrary", "arbitrary"),
                                vmem_limit_bytes=VMEM_LIMIT)


def _row_block(row0, seq, tb):
    base, per_seq = row0 // tb, seq // tb
    return lambda col: (lambda b, j: (base + b * per_seq + j, col))


def _layer_spec(layer, shape):
    return pl.BlockSpec((None,) + shape, lambda b, j: (layer,) + (0,) * len(shape))


def _state_spec(state_layer, shape):
    return pl.BlockSpec((None, None) + shape, lambda b, j: (state_layer, b) + (0,) * len(shape))


def _lru_call(proj, conv0, h0, state_layer, lw, layer, nseq, seq, row0, tb):
    rb, ob = _row_block(row0, seq, tb), _row_block(0, seq, tb)
    wl = functools.partial(_layer_spec, layer)
    st = functools.partial(_state_spec, state_layer)
    out_state = lambda shape: pl.BlockSpec((None,) + shape, lambda b, j: (b,) + (0,) * len(shape))
    return pl.pallas_call(
        _lru_kernel,
        grid=(nseq, seq // tb),
        in_specs=[pl.BlockSpec((tb, GROUP), rb(COL_LRU_X // GROUP)),
                  pl.BlockSpec((tb, GROUP), rb(COL_LRU_G // GROUP)),
                  st((8, GROUP)), st((1, GROUP)),
                  wl((CONV_W, GROUP)), wl((1, GROUP)), wl((GROUP, 2 * GROUP)), wl((1, 2 * GROUP)),
                  wl((1, GROUP)), wl((1, GROUP))],
        out_specs=[pl.BlockSpec((tb, GROUP), ob(0)), out_state((1, GROUP)), out_state((8, GROUP))],
        out_shape=[jax.ShapeDtypeStruct((nseq * seq, GROUP), BF16),
                   jax.ShapeDtypeStruct((nseq, 1, GROUP), F32),
                   jax.ShapeDtypeStruct((nseq, 8, GROUP), F32)],
        scratch_shapes=[pltpu.VMEM((8, GROUP), F32), pltpu.VMEM((1, GROUP), F32)],
        compiler_params=_seq_params(),
        name="rglru",
    )(proj, proj, conv0, h0, lw['conv_w'], lw['conv_b'], lw['lru_wg'], lw['lru_bg'], lw['lru_lambda'], lw['on_a'])


def _mla_pre_call(proj_p, proj_s, tabs, lw, layer, tm, nseq_p, seq_p):
    rows_p = nseq_p * seq_p
    rows = rows_p + proj_s.shape[0]
    n_p = rows_p // tm

    def cols(width, col):
        return (pl.BlockSpec((tm, width), lambda i: (jnp.minimum(i, n_p - 1), col // width)),
                pl.BlockSpec((tm, width), lambda i: (jnp.maximum(i - n_p, 0), col // width)))

    pos_blocks = seq_p // tm
    wl = lambda shape: pl.BlockSpec((None,) + shape, lambda i: (layer,) + (0,) * len(shape))
    tab = pl.BlockSpec((tm, 128), lambda i: (jnp.where(i < n_p, i % pos_blocks, pos_blocks + i - n_p), 0))
    ckv_p, ckv_s = _two_source(n_p, tm, KV_RANK)
    _, kr_s = _two_source(n_p, tm, ROPE_DIM)

    def kr_p_map(i):
        t = jnp.minimum(i, n_p - 1)
        return (t // pos_blocks, 0, t % pos_blocks)

    kr_p = pl.BlockSpec((None, ROPE_DIM, tm), kr_p_map)
    return pl.pallas_call(
        functools.partial(_mla_pre_kernel, n_p=n_p),
        grid=(rows // tm,),
        in_specs=[*cols(GROUP, COL_Q_LAT), *cols(128, COL_KV_LAT), *cols(128, COL_MISC),
                  tab, tab, tab,
                  wl((1, GROUP)), wl((GROUP, 1024)), wl((1, 1024)), wl((1, KV_RANK)), wl((1, 128)),
                  wl((KV_RANK, 1024)), wl((1, 512))],
        out_specs=[pl.BlockSpec((tm, 512), lambda i: (i, 0))] * 3 + [ckv_p, ckv_s, kr_p, kr_s],
        out_shape=[jax.ShapeDtypeStruct((rows, 512), BF16)] * 3
                  + [jax.ShapeDtypeStruct((rows_p, KV_RANK), F32), jax.ShapeDtypeStruct((rows - rows_p, KV_RANK), F32),
                     jax.ShapeDtypeStruct((nseq_p, ROPE_DIM, seq_p), F32),
                     jax.ShapeDtypeStruct((rows - rows_p, ROPE_DIM), F32)],
        compiler_params=pltpu.CompilerParams(dimension_semantics=("arbitrary",),
                                             vmem_limit_bytes=VMEM_LIMIT),
        name="mla_pre",
    )(proj_p, proj_s, proj_p, proj_s, proj_p, proj_s, *tabs, lw['q_norm'], lw['wq'], lw['gq'], lw['kv_norm'], lw['kr_norm'],
      lw['wkv'], lw['gk'])


def _kv_past_call(ckv, krope_t, lw, tr):
    depth, rows, _ = ckv.shape
    per_seq = krope_t.shape[3] // tr
    wl = lambda shape: pl.BlockSpec((None,) + shape, lambda l, i: (l,) + (0,) * len(shape))
    return pl.pallas_call(
        _kv_past_kernel,
        grid=(depth, rows // tr),
        in_specs=[pl.BlockSpec((None, tr, KV_RANK), lambda l, i: (l, i, 0)),
                  pl.BlockSpec((None, None, ROPE_DIM, tr), lambda l, i: (l, i // per_seq, 0, i % per_seq)),
                  wl((KV_RANK, 1024)), wl((1, 512))],
        out_specs=[pl.BlockSpec((None, tr, 512), lambda l, i: (l, i, 0))] * 2,
        out_shape=[jax.ShapeDtypeStruct((depth, rows, 512), BF16)] * 2,
        compiler_params=_seq_params(),
        name="kv_past",
    )(ckv, krope_t, lw['wkv'], lw['gk'])


def _attn_call(q, k, v, k_past, v_past, lw, layer, nseq, seq, row0, tq, tka, nb):
    base, per_seq = row0 // tq, seq // tq
    qspec = lambda s: pl.BlockSpec((tq, 512), lambda i, j: (base + (nb * i + s) * per_seq + j, 0))
    if k_past is None:
        assert row0 == 0
        ka, va = k, v
        past_spec = pl.BlockSpec((nb * seq, 512), lambda i, j: (i, 0))
        n_past_static = None
    else:
        ka, va = k_past, v_past
        past_len = k_past.shape[1] // nseq
        past_spec = pl.BlockSpec((None, nb * past_len, 512), lambda i, j: (layer, i, 0))
        n_past_static = past_len // tka
    own = [qspec(s) for s in range(nb)]
    return pl.pallas_call(
        functools.partial(_attn_kernel, nb=nb, tka=tka, n_past_static=n_past_static),
        grid=(nseq // nb, seq // tq),
        in_specs=[*own, past_spec, past_spec, *own, *own, _layer_spec(layer, (1, GROUP))],
        out_specs=pl.BlockSpec((nb, tq, GROUP), lambda i, j: (i, j, 0)),
        out_shape=jax.ShapeDtypeStruct((nseq, seq, GROUP), BF16),
        scratch_shapes=[pltpu.VMEM((nb * N_HEADS, 1, tq), F32), pltpu.VMEM((nb * N_HEADS, 128, tq), F32)],
        compiler_params=_seq_params(),
        name="mla_attn",
    )(*[q] * nb, ka, va, *[k] * nb, *[v] * nb, lw['on_b'])


def _seq_block(nb, shape):
    return pl.BlockSpec((nb,) + shape, lambda i, j: (i,) + (0,) * len(shape))


def _ret_call(proj, tabs, tab_map, s0, state_layer, lw, layer, lc, nb):
    nseq, seq, _ = proj.shape
    blk = lambda col: pl.BlockSpec((nb, lc, GROUP), lambda i, j: (i, j, col // GROUP))
    tab = pl.BlockSpec((lc, 128), tab_map)
    return pl.pallas_call(
        _ret_kernel,
        grid=(nseq // nb, seq // lc),
        in_specs=[blk(COL_R_Q), blk(COL_R_K), blk(COL_R_V), blk(COL_R_G), tab, tab, tab,
                  pl.BlockSpec((None, nb, GROUP, HEAD_DIM), lambda i, j: (state_layer, i, 0, 0)),
                  _layer_spec(layer, (1, GROUP))],
        out_specs=[pl.BlockSpec((nb, lc, GROUP), lambda i, j: (i, j, 0)), _seq_block(nb, (GROUP, HEAD_DIM))],
        out_shape=[jax.ShapeDtypeStruct((nseq, seq, GROUP), BF16),
                   jax.ShapeDtypeStruct((nseq, GROUP, HEAD_DIM), F32)],
        scratch_shapes=[pltpu.VMEM((nb, GROUP, GROUP), F32), pltpu.VMEM((N_HEADS, lc, lc), F32),
                        pltpu.VMEM((lc, GROUP), F32), pltpu.VMEM((lc, GROUP), F32)],
        compiler_params=_seq_params(),
        name="retention",
    )(proj, proj, proj, proj, *tabs, s0, lw['on_c'])


def _mlstm_call(proj, c0, n0, m0, state_layer, lw, layer, lc, nb):
    nseq, seq, _ = proj.shape
    blk = lambda col: pl.BlockSpec((nb, lc, GROUP), lambda i, j: (i, j, col // GROUP))
    st = lambda shape: pl.BlockSpec((None, nb) + shape, lambda i, j: (state_layer, i) + (0,) * len(shape))
    return pl.pallas_call(
        _mlstm_kernel,
        grid=(nseq // nb, seq // lc),
        in_specs=[blk(COL_M_Q), blk(COL_M_K), blk(COL_M_V), blk(COL_M_O),
                  pl.BlockSpec((nb, lc, 128), lambda i, j: (i, j, COL_MISC // 128)), _layer_spec(layer, (1, 128)),
                  st((GROUP, HEAD_DIM)), st((1, GROUP)), st((1, 128)), _layer_spec(layer, (1, GROUP))],
        out_specs=[pl.BlockSpec((nb, lc, GROUP), lambda i, j: (i, j, 0)), _seq_block(nb, (GROUP, HEAD_DIM)),
                   _seq_block(nb, (1, GROUP)), _seq_block(nb, (1, 128))],
        out_shape=[jax.ShapeDtypeStruct((nseq, seq, GROUP), BF16),
                   jax.ShapeDtypeStruct((nseq, GROUP, HEAD_DIM), F32),
                   jax.ShapeDtypeStruct((nseq, 1, GROUP), F32),
                   jax.ShapeDtypeStruct((nseq, 1, 128), F32)],
        scratch_shapes=[pltpu.VMEM((nb, GROUP, GROUP), F32), pltpu.VMEM((nb, 1, GROUP), F32),
                        pltpu.VMEM((nb, 1, 128), F32)],
        compiler_params=_seq_params(),
        name="mlstm",
    )(proj, proj, proj, proj, proj, lw['b_if'], c0, n0, m0, lw['on_d'])


def _block_diag(s):
    h, d, e = s.shape[-3:]
    eye = jnp.eye(h, dtype=s.dtype)
    return (s[..., :, :, None, :] * eye[:, None, :, None]).reshape(s.shape[:-3] + (h * d, h * e))


def _rope_tables(pos, half, lanes, lo):
    inv = ROPE_THETA ** (-jnp.arange(half, dtype=F32) / half)
    ang = pos[:, None] * inv[None, :]
    cos, sin = jnp.cos(ang), jnp.sin(ang)
    n = pos.shape[0]
    c = jnp.ones((n, lanes), F32).at[:, lo:lo + 2 * half].set(jnp.concatenate([cos, cos], axis=1))
    sp = jnp.zeros((n, lanes), F32).at[:, lo + half:lo + 2 * half].set(sin)
    sm = jnp.zeros((n, lanes), F32).at[:, lo:lo + half].set(-sin)
    return c, sp, sm


def _prep_weights(norm_mix, w_in, lru_conv_w, lru_conv_b, lru_wa, lru_ba, lru_wx, lru_bx, lru_lambda,
                  mla_q_norm, mla_wq_b, mla_qn_norm, mla_qr_norm, mla_kv_norm, mla_kr_norm, mla_wkv_b,
                  mla_kn_norm, mlstm_b_if, out_norm, w_out, norm_ffn, w_gu, w_down):
    depth = w_in.shape[0]
    row = lambda a: a.reshape(depth, 1, -1)
    n_if = 2 * N_HEADS
    w_in_pad = _w_in_prep(jnp.swapaxes(w_in, 1, 2))

    lru_wg = jnp.concatenate([_block_diag(lru_wa), _block_diag(lru_wx)], axis=2).astype(BF16)
    lru_bg = jnp.concatenate([lru_ba, lru_bx], axis=1)

    wq = mla_wq_b.reshape(depth, GROUP, N_HEADS, HEAD_DIM + ROPE_DIM)
    wq = jnp.pad(wq, ((0, 0), (0, 0), (0, 0), (0, 128 - HEAD_DIM - ROPE_DIM))).reshape(depth, GROUP, 512)
    gq = jnp.concatenate([mla_qn_norm, mla_qr_norm, jnp.zeros((depth, 32), F32)], axis=1)
    gq = jnp.tile(gq, (1, N_HEADS))
    lane = np.arange(512)
    in_rope = (lane % 128 >= HEAD_DIM) & (lane % 128 < HEAD_DIM + ROPE_DIM)
    first_half = (lane % 128 - HEAD_DIM) < ROPE_DIM // 2
    swap = np.where(in_rope, np.where(first_half, lane + ROPE_DIM // 2, lane - ROPE_DIM // 2), lane)
    wq = jnp.concatenate([wq, wq[:, :, swap]], axis=2)
    gq = jnp.concatenate([gq, gq[:, swap]], axis=1)
    wkv = mla_wkv_b.reshape(depth, KV_RANK, N_HEADS, 2 * HEAD_DIM)
    pad_head = lambda w: jnp.pad(w, ((0, 0), (0, 0), (0, 0), (0, 128 - HEAD_DIM))).reshape(depth, KV_RANK, 512)
    wkv = jnp.concatenate([pad_head(wkv[..., :HEAD_DIM]), pad_head(wkv[..., HEAD_DIM:])], axis=2)
    gk = jnp.tile(jnp.concatenate([mla_kn_norm, jnp.zeros((depth, 128 - HEAD_DIM), F32)], axis=1), (1, N_HEADS))
    kr_norm = jnp.pad(mla_kr_norm, ((0, 0), (0, 128 - ROPE_DIM)))
    b_if = jnp.pad(mlstm_b_if, ((0, 0), (MISC_IG, 128 - MISC_IG - n_if)))

    return dict(
        norm_mix=row(norm_mix), w_in=w_in_pad,
        conv_w=lru_conv_w, conv_b=row(lru_conv_b), lru_wg=lru_wg, lru_bg=row(lru_bg), lru_lambda=row(lru_lambda),
        q_norm=row(mla_q_norm), wq=wq.astype(BF16), gq=row(gq), kv_norm=row(mla_kv_norm), kr_norm=row(kr_norm),
        wkv=wkv.astype(BF16), gk=row(gk), b_if=row(b_if),
        on_a=row(out_norm[:, :GROUP]), on_b=row(out_norm[:, GROUP:2 * GROUP]),
        on_c=row(out_norm[:, 2 * GROUP:3 * GROUP]), on_d=row(out_norm[:, 3 * GROUP:]),
        w_out=w_out.astype(BF16), norm_ffn=row(norm_ffn), w_gu=w_gu.astype(BF16), w_down=w_down.astype(BF16))


def _tile_rows(rows, cap):
    t = cap
    while rows % t:
        t //= 2
    return t


def kernel(x_prompt, x_sample, cache_mla_ckv, cache_mla_krope, state_lru_h, state_lru_conv, state_ret, state_mlstm_C, state_mlstm_n, state_mlstm_m, norm_mix, w_in, lru_conv_w, lru_conv_b, lru_wa, lru_ba, lru_wx, lru_bx, lru_lambda, mla_q_norm, mla_wq_b, mla_qn_norm, mla_qr_norm, mla_kv_norm, mla_kr_norm, mla_wkv_b, mla_kn_norm, mlstm_b_if, out_norm, w_out, norm_ffn, w_gu, w_down):
    bp, tp, _ = x_prompt.shape
    bs, ts, _ = x_sample.shape
    depth, _, past, _ = cache_mla_ckv.shape
    rows_p, rows_s = bp * tp, bs * ts
    tm = math.gcd(_tile_rows(tp, 512), rows_s)
    tb_p, tb_s = min(tp, 512), min(ts, 512)
    lc_p, lc_s = min(tp, 256), min(ts, 256)
    tq_p, tq_s = min(tp, 512), min(ts, 512)
    tka_s = past
    nb_p, nb_s = math.gcd(bp, 8), math.gcd(bs, 4)
    nb_att = math.gcd(bp, 2)
    assert past % CHUNK == 0 and ts % CHUNK == 0 and tp % tm == 0 and rows_s % tm == 0

    lw = _prep_weights(norm_mix, w_in, lru_conv_w, lru_conv_b, lru_wa, lru_ba, lru_wx, lru_bx, lru_lambda,
                       mla_q_norm, mla_wq_b, mla_qn_norm, mla_qr_norm, mla_kv_norm, mla_kr_norm, mla_wkv_b,
                       mla_kn_norm, mlstm_b_if, out_norm, w_out, norm_ffn, w_gu, w_down)

    pos = jnp.concatenate([jnp.arange(tp, dtype=F32), jnp.tile(jnp.arange(past, past + ts, dtype=F32), bs)])
    tabs_mla = _rope_tables(pos, ROPE_DIM // 2, 128, HEAD_DIM)
    tabs_ret = tuple(jnp.concatenate([t, t], axis=1) for t in _rope_tables(pos, HEAD_DIM // 2, HEAD_DIM, 0))

    k_past, v_past = _kv_past_call(cache_mla_ckv.reshape(depth, bs * past, KV_RANK),
                                   jnp.swapaxes(cache_mla_krope, 2, 3), lw,
                                   _tile_rows(past, 1024))

    zeros = lambda *shape: jnp.zeros((1, bp) + shape, F32)
    st_p = dict(conv=zeros(8, GROUP), h=zeros(1, GROUP), s=zeros(GROUP, HEAD_DIM), c=zeros(GROUP, HEAD_DIM),
                n=zeros(1, GROUP), m=zeros(1, 128))
    st_s = dict(conv=jnp.pad(state_lru_conv, ((0, 0), (0, 0), (8 - (CONV_W - 1), 0), (0, 0))),
                h=state_lru_h[:, :, None, :], s=state_ret.reshape(depth, bs, GROUP, HEAD_DIM),
                c=state_mlstm_C.reshape(depth, bs, GROUP, HEAD_DIM),
                n=state_mlstm_n.reshape(depth, bs, 1, GROUP),
                m=jnp.pad(state_mlstm_m, ((0, 0), (0, 0), (0, 128 - N_HEADS)))[:, :, None, :])

    x_p = x_prompt.reshape(rows_p, D_MODEL)
    x_s = x_sample.reshape(rows_s, D_MODEL)
    acc = {name: [] for name in ('ckv_p', 'ckv_s', 'kr_p', 'kr_s', 'h_p', 'h_s', 'conv_p', 'conv_s',
                                 's_p', 's_s', 'c_p', 'c_s', 'n_p', 'n_s', 'm_p', 'm_s')}
    for l in range(depth):
        proj_p, proj_s = _in_proj(x_p, x_s, lw, l, tm)
        proj3_p, proj3_s = proj_p.reshape(bp, tp, IN_PAD), proj_s.reshape(bs, ts, IN_PAD)
        ya_p, h_p, conv_p = _lru_call(proj_p, st_p['conv'], st_p['h'], 0, lw, l, bp, tp, 0, tb_p)
        ya_s, h_s, conv_s = _lru_call(proj_s, st_s['conv'], st_s['h'], l, lw, l, bs, ts, 0, tb_s)
        q, k, v, ckv_p, ckv_s, kr_p, kr_s = _mla_pre_call(proj_p, proj_s, tabs_mla, lw, l, tm, bp, tp)
        yb_p = _attn_call(q, k, v, None, None, lw, l, bp, tp, 0, tq_p, tq_p, nb_att)
        yb_s = _attn_call(q, k, v, k_past, v_past, lw, l, bs, ts, rows_p, tq_s, tka_s, 1)
        yc_p, s_p = _ret_call(proj3_p, tabs_ret, lambda i, j: (j, 0), st_p['s'], 0, lw, l, lc_p, nb_p)
        yc_s, s_s = _ret_call(proj3_s, tabs_ret, lambda i, j: (tp // lc_s + j, 0), st_s['s'], l, lw, l, lc_s, nb_s)
        yd_p, c_p, n_p, m_p = _mlstm_call(proj3_p, st_p['c'], st_p['n'], st_p['m'], 0, lw, l, lc_p, nb_p)
        yd_s, c_s, n_s, m_s = _mlstm_call(proj3_s, st_s['c'], st_s['n'], st_s['m'], l, lw, l, lc_s, nb_s)
        flat = lambda y: y.reshape(-1, GROUP)
        x_p, x_s = _out_proj_ffn(x_p, x_s, (ya_p, flat(yb_p), flat(yc_p), flat(yd_p)),
                                 (ya_s, flat(yb_s), flat(yc_s), flat(yd_s)), lw, l, tm)
        for name, val in (('ckv_p', ckv_p), ('ckv_s', ckv_s), ('kr_p', kr_p), ('kr_s', kr_s), ('h_p', h_p),
                          ('h_s', h_s), ('conv_p', conv_p), ('conv_s', conv_s), ('s_p', s_p), ('s_s', s_s),
                          ('c_p', c_p), ('c_s', c_s), ('n_p', n_p), ('n_s', n_s), ('m_p', m_p), ('m_s', m_s)):
            acc[name].append(val)

    st = {name: jnp.stack(vals) for name, vals in acc.items()}

    def outputs(tag, b, t):
        krope = jnp.swapaxes(st['kr_p'], 2, 3) if tag == 'p' else st['kr_s'].reshape(depth, b, t, ROPE_DIM)
        return (st['ckv_' + tag].reshape(depth, b, t, KV_RANK), krope,
                st['h_' + tag][:, :, 0], st['conv_' + tag][:, :, 8 - (CONV_W - 1):],
                st['s_' + tag].reshape(depth, b, N_HEADS, HEAD_DIM, HEAD_DIM),
                st['c_' + tag].reshape(depth, b, N_HEADS, HEAD_DIM, HEAD_DIM),
                st['n_' + tag].reshape(depth, b, N_HEADS, HEAD_DIM), st['m_' + tag][:, :, 0, :N_HEADS])

    return ((x_p.reshape(bp, tp, D_MODEL), x_s.reshape(bs, ts, D_MODEL))
            + outputs('p', bp, tp) + outputs('s', bs, ts))
```
